```python
import jax, jax.numpy as jnp
from jax import lax
import numpy as np

D_MODEL = 2048
BATCH = 2
SEQ = 4096
DEPTH = 4
DEC_BATCH = 32
DEC_SEQ = 4
PAST_LEN = 16384
PAGE_SIZE = 128

N_META = 16
ATT_HEAD_DIM = 64
ATT_HEADS = (D_MODEL // 2) // ATT_HEAD_DIM
ATT_KV_HEADS = ATT_HEADS // 8
ATT_GROUP = ATT_HEADS // ATT_KV_HEADS
WINDOW = 128
BLOCK = 128
GLA_HEADS = 4
GLA_DV = (D_MODEL // 2) // GLA_HEADS
GLA_DK = GLA_DV // 2
GLA_RANK = 16
GLA_GATE_NORM = 16.0
GLA_CHUNK = 64
D_FF = 5632
EPS = 1e-6

ATT_WIDTH = ATT_HEADS * ATT_HEAD_DIM
KV_WIDTH = ATT_KV_HEADS * ATT_HEAD_DIM
GLA_KWIDTH = GLA_HEADS * GLA_DK
GLA_WIDTH = GLA_HEADS * GLA_DV
MIX_WIDTH = ATT_WIDTH + GLA_WIDTH
IN_WIDTH = ATT_WIDTH + 2 * KV_WIDTH + 2 * GLA_KWIDTH + 2 * GLA_WIDTH + GLA_RANK

kernel_name = "hymba_swa_sink_gla_macaron_step"


def rmsnorm(x, g):
    xf = x.astype(jnp.float32)
    y = xf * lax.rsqrt(jnp.mean(xf * xf, axis=-1, keepdims=True) + EPS)
    return (y * g.astype(jnp.float32)).astype(x.dtype)


def swiglu(x, wg, wu, wd):
    return (jax.nn.silu(x @ wg) * (x @ wu)) @ wd


def project(n, w_in, w_a2, b_a):
    B, L, _ = n.shape
    sizes = (ATT_WIDTH, KV_WIDTH, KV_WIDTH, GLA_KWIDTH, GLA_KWIDTH, GLA_WIDTH, GLA_WIDTH, GLA_RANK)
    idx = [int(v) for v in np.cumsum(sizes)[:-1]]
    qa, ka, va, qg, kg, vg, rg, lr = jnp.split(n @ w_in, idx, axis=-1)
    qa = qa.reshape(B, L, ATT_KV_HEADS, ATT_GROUP, ATT_HEAD_DIM) * (ATT_HEAD_DIM ** -0.5)
    ka = ka.reshape(B, L, ATT_KV_HEADS, ATT_HEAD_DIM)
    va = va.reshape(B, L, ATT_KV_HEADS, ATT_HEAD_DIM)
    qg = qg.reshape(B, L, GLA_HEADS, GLA_DK) * (GLA_DK ** -0.5)
    kg = kg.reshape(B, L, GLA_HEADS, GLA_DK)
    vg = vg.reshape(B, L, GLA_HEADS, GLA_DV)
    rg = rg.reshape(B, L, GLA_HEADS, GLA_DV)
    log_a = jax.nn.log_sigmoid((lr @ w_a2 + b_a).astype(jnp.float32)) / GLA_GATE_NORM
    log_a = log_a.reshape(B, L, GLA_HEADS, GLA_DK)
    return qa, ka, va, qg, kg, vg, rg, log_a


def sink_softmax(s, sink_b, mask):
    s = jnp.where(mask, s, -jnp.inf)
    m = jnp.maximum(jnp.max(s, axis=-1, keepdims=True), sink_b)
    p = jnp.exp(s - m)
    return p / (jnp.sum(p, axis=-1, keepdims=True) + jnp.exp(sink_b - m))


def banded_window_attention(q, k, v, valid, sinks):
    B, Lp = q.shape[:2]
    nb = Lp // BLOCK
    qb = q.reshape(B, nb, BLOCK, ATT_KV_HEADS, ATT_GROUP, ATT_HEAD_DIM).astype(jnp.float32)

    def with_prev(t):
        tb = t.reshape(B, nb, BLOCK, ATT_KV_HEADS, ATT_HEAD_DIM).astype(jnp.float32)
        prev = jnp.pad(tb[:, :-1], ((0, 0), (1, 0), (0, 0), (0, 0), (0, 0)))
        return jnp.concatenate([prev, tb], axis=2)

    kk, vv = with_prev(k), with_prev(v)
    vb = valid.reshape(nb, BLOCK)
    kvalid = jnp.concatenate([jnp.pad(vb[:-1], ((1, 0), (0, 0))), vb], axis=1)
    i = jnp.arange(BLOCK)[:, None]
    j = jnp.arange(2 * BLOCK)[None, :]
    diff = i + BLOCK - j
    band = (diff >= 0) & (diff <= WINDOW)
    mask = (band[None] & kvalid[:, None, :])[None, :, None, None]
    s = jnp.einsum('bnqkgd,bnskd->bnkgqs', qb, kk)
    sink_b = sinks.astype(jnp.float32).reshape(1, 1, ATT_KV_HEADS, ATT_GROUP, 1, 1)
    p = sink_softmax(s, sink_b, mask)
    o = jnp.einsum('bnkgqs,bnskd->bnqkgd', p, vv)
    return o.reshape(B, Lp, ATT_WIDTH).astype(q.dtype)


def window_attention_step(q, k, v, kbuf, vbuf, sinks):
    Bd, S = q.shape[:2]
    Wb = kbuf.shape[1]
    kk = jnp.concatenate([kbuf.astype(jnp.float32), k.astype(jnp.float32)], axis=1)
    vv = jnp.concatenate([vbuf.astype(jnp.float32), v.astype(jnp.float32)], axis=1)
    qoff = jnp.arange(S)[:, None]
    koff = jnp.concatenate([jnp.arange(Wb) - Wb, jnp.arange(S)])[None, :]
    diff = qoff - koff
    mask = (diff >= 0) & (diff <= WINDOW)
    s = jnp.einsum('bqkgd,bskd->bkgqs', q.astype(jnp.float32), kk)
    sink_b = sinks.astype(jnp.float32).reshape(1, ATT_KV_HEADS, ATT_GROUP, 1, 1)
    p = sink_softmax(s, sink_b, mask)
    o = jnp.einsum('bkgqs,bskd->bqkgd', p, vv)
    return o.reshape(Bd, S, ATT_WIDTH).astype(q.dtype)


def gla_chunked(q, k, v, log_a, s0, chunk):
    B, L, H, DK = q.shape
    DV = v.shape[-1]
    n = L // chunk

    def to_chunks(t):
        return jnp.moveaxis(t.astype(jnp.float32).reshape(B, n, chunk, H, t.shape[-1]), 1, 0)

    causal = jnp.tril(jnp.ones((chunk, chunk), dtype=bool))[None, :, :, None, None]

    def step(S, inp):
        qc, kc, vc, gc = inp
        b = jnp.cumsum(gc, axis=1)
        diff = b[:, :, None] - b[:, None, :]
        decay = jnp.exp(jnp.where(causal, diff, -jnp.inf))
        A = jnp.einsum('bthd,bshd,btshd->bths', qc, kc, decay)
        o = jnp.einsum('bthd,bhdv->bthv', qc * jnp.exp(b), S) + jnp.einsum('bths,bshv->bthv', A, vc)
        bl = b[:, -1]
        S = jnp.exp(bl)[..., None] * S + jnp.einsum('bshd,bshv->bhdv', kc * jnp.exp(bl[:, None] - b), vc)
        return S, o

    sT, o = lax.scan(step, s0.astype(jnp.float32), (to_chunks(q), to_chunks(k), to_chunks(v), to_chunks(log_a)))
    return jnp.moveaxis(o, 0, 1).reshape(B, L, H, DV), sT


def merge_groups(att_o, gla_o, rg, g_gla, w_out):
    B, L = att_o.shape[:2]
    go = rmsnorm(gla_o, g_gla) * jax.nn.silu(rg.astype(jnp.float32))
    go = go.astype(att_o.dtype).reshape(B, L, GLA_WIDTH)
    return jnp.concatenate([att_o, go], axis=-1) @ w_out


def prompt_mixer(n, valid, w_in, w_a2, b_a, sinks, g_gla, w_out):
    qa, ka, va, qg, kg, vg, rg, log_a = project(n, w_in, w_a2, b_a)
    att = banded_window_attention(qa, ka, va, valid, sinks)
    kg = kg * valid.astype(kg.dtype)[None, :, None, None]
    s0 = jnp.zeros((n.shape[0], GLA_HEADS, GLA_DK, GLA_DV), jnp.float32)
    go, sT = gla_chunked(qg, kg, vg, log_a, s0, GLA_CHUNK)
    out = merge_groups(att, go, rg, g_gla, w_out)
    return out, ka[:, -WINDOW:], va[:, -WINDOW:], sT


def sample_mixer(n, kbuf, vbuf, s0, w_in, w_a2, b_a, sinks, g_gla, w_out):
    qa, ka, va, qg, kg, vg, rg, log_a = project(n, w_in, w_a2, b_a)
    att = window_attention_step(qa, ka, va, kbuf, vbuf, sinks)
    go, sT = gla_chunked(qg, kg, vg, log_a, s0, n.shape[1])
    out = merge_groups(att, go, rg, g_gla, w_out)
    Wb = kbuf.shape[1]
    kcat = jnp.concatenate([kbuf.astype(ka.dtype), ka], axis=1)
    vcat = jnp.concatenate([vbuf.astype(va.dtype), va], axis=1)
    return out, kcat[:, -Wb:], vcat[:, -Wb:], sT


def setup_inputs(seed: int = 0) -> dict:
    key = jax.random.key(seed)
    ks = jax.random.split(key, 20)
    f32 = jnp.float32
    win_rows = min(WINDOW, PAST_LEN)
    nrm = lambda k, shape, scale: jax.random.normal(k, shape, f32) * scale
    return {
        "x_prompt": nrm(ks[0], (BATCH, SEQ, D_MODEL), 1.0),
        "x_sample": nrm(ks[1], (DEC_BATCH, DEC_SEQ, D_MODEL), 1.0),
        "cache_k_win": nrm(ks[2], (DEPTH, DEC_BATCH, win_rows, ATT_KV_HEADS, ATT_HEAD_DIM), 1.0),
        "cache_v_win": nrm(ks[3], (DEPTH, DEC_BATCH, win_rows, ATT_KV_HEADS, ATT_HEAD_DIM), 1.0),
        "state_gla": nrm(ks[4], (DEPTH, DEC_BATCH, GLA_HEADS, GLA_DK, GLA_DV), 1.0),
        "meta_tokens": nrm(ks[5], (N_META, D_MODEL), 1.0),
        "norm_gains": 1.0 + nrm(ks[6], (DEPTH, 6, D_MODEL), 0.1),
        "w_ffn_gate": nrm(ks[7], (DEPTH, 2, D_MODEL, D_FF), D_MODEL ** -0.5),
        "w_ffn_up": nrm(ks[8], (DEPTH, 2, D_MODEL, D_FF), D_MODEL ** -0.5),
        "w_ffn_down": nrm(ks[9], (DEPTH, 2, D_FF, D_MODEL), D_FF ** -0.5),
        "w_in": nrm(ks[10], (DEPTH, D_MODEL, IN_WIDTH), D_MODEL ** -0.5),
        "w_gate_up": nrm(ks[11], (DEPTH, GLA_RANK, GLA_KWIDTH), GLA_RANK ** -0.5),
        "b_gate": nrm(ks[12], (DEPTH, GLA_KWIDTH), 0.1),
        "attn_sinks": nrm(ks[13], (DEPTH, ATT_HEADS), 0.5),
        "gla_norm": 1.0 + nrm(ks[14], (DEPTH, GLA_DV), 0.1),
        "w_out": nrm(ks[15], (DEPTH, MIX_WIDTH, D_MODEL), MIX_WIDTH ** -0.5),
    }


def reference(x_prompt, x_sample, cache_k_win, cache_v_win, state_gla, meta_tokens, norm_gains,
              w_ffn_gate, w_ffn_up, w_ffn_down, w_in, w_gate_up, b_gate, attn_sinks, gla_norm, w_out):
    B = x_prompt.shape[0]
    pad = BLOCK - N_META
    hp = jnp.concatenate([
        jnp.zeros((B, pad, D_MODEL), x_prompt.dtype),
        jnp.broadcast_to(meta_tokens.astype(x_prompt.dtype)[None], (B, N_META, D_MODEL)),
        x_prompt], axis=1)
    valid = jnp.arange(hp.shape[1]) >= pad
    hs = x_sample
    pk, pv, ps, sk, sv, ss = [], [], [], [], [], []
    for l in range(DEPTH):
        g = norm_gains[l]
        hp = hp + 0.5 * rmsnorm(swiglu(rmsnorm(hp, g[0]), w_ffn_gate[l, 0], w_ffn_up[l, 0], w_ffn_down[l, 0]), g[1])
        hs = hs + 0.5 * rmsnorm(swiglu(rmsnorm(hs, g[0]), w_ffn_gate[l, 0], w_ffn_up[l, 0], w_ffn_down[l, 0]), g[1])
        o, kw, vw, st = prompt_mixer(rmsnorm(hp, g[2]), valid, w_in[l], w_gate_up[l], b_gate[l],
                                     attn_sinks[l], gla_norm[l], w_out[l])
        hp = hp + rmsnorm(o, g[3])
        pk.append(kw); pv.append(vw); ps.append(st)
        o, kw, vw, st = sample_mixer(rmsnorm(hs, g[2]), cache_k_win[l], cache_v_win[l], state_gla[l],
                                     w_in[l], w_gate_up[l], b_gate[l], attn_sinks[l], gla_norm[l], w_out[l])
        hs = hs + rmsnorm(o, g[3])
        sk.append(kw); sv.append(vw); ss.append(st)
        hp = hp + 0.5 * rmsnorm(swiglu(rmsnorm(hp, g[4]), w_ffn_gate[l, 1], w_ffn_up[l, 1], w_ffn_down[l, 1]), g[5])
        hs = hs + 0.5 * rmsnorm(swiglu(rmsnorm(hs, g[4]), w_ffn_gate[l, 1], w_ffn_up[l, 1], w_ffn_down[l, 1]), g[5])
    y_prompt = hp[:, BLOCK:]
    y_sample = hs
    return (y_prompt, y_sample, jnp.stack(pk), jnp.stack(pv), jnp.stack(ps), jnp.stack(sk), jnp.stack(sv), jnp.stack(ss))
```

```python
import functools

import jax
import jax.numpy as jnp
from jax import lax
from jax.experimental import pallas as pl
from jax.experimental.pallas import tpu as pltpu

F32 = jnp.float32
BF16 = jnp.bfloat16

D_MODEL = 2048
D_FF = 5632
N_META = 16
BLOCK = 128
WINDOW = 128
ATT_HEADS = 16
ATT_KV_HEADS = 2
ATT_GROUP = 8
ATT_HEAD_DIM = 64
ATT_WIDTH = 1024
KV_WIDTH = 128
GLA_HEADS = 4
GLA_DK = 128
GLA_DV = 256
GLA_KWIDTH = 512
GLA_WIDTH = 1024
GLA_RANK = 16
GLA_GATE_NORM = 16.0
GLA_CHUNK = 64
GLA_SUB = 16
EPS = 1e-6
PAD = BLOCK - N_META

COL_QA, COL_VG, COL_RG, COL_QG, COL_KG, COL_KA, COL_VA, COL_LR = 0, 1024, 2048, 3072, 3584, 4096, 4224, 4352
Z_WIDTH = 4480
LR_PAD = 128

VMEM_LIMIT = 56 * 1024 * 1024

TM = 512
TF = 512
TN = 640


def _rms(x, gain):
    ms = jnp.mean(x * x, axis=-1, keepdims=True)
    return x * lax.rsqrt(ms + EPS) * gain


def _dot(a, b):
    return jnp.dot(a, b, preferred_element_type=F32)


def _dot_nt(a, b):
    return lax.dot_general(a, b, (((1,), (1,)), ((), ())), preferred_element_type=F32)


def _dot_tn(a, b):
    return lax.dot_general(a, b, (((0,), (0,)), ((), ())), preferred_element_type=F32)


def _ffn_kernel(x_ref, gpre_ref, gpost_ref, wg_ref, wu_ref, wd_ref, o_ref, xn_ref, acc_ref):
    j = pl.program_id(1)

    @pl.when(j == 0)
    def _():
        xn_ref[...] = _rms(x_ref[...], gpre_ref[...]).astype(BF16)

    xn = xn_ref[...]
    g = _dot(xn, wg_ref[...])
    u = _dot(xn, wu_ref[...])
    h = (g * jax.nn.sigmoid(g) * u).astype(BF16)
    part = _dot(h, wd_ref[...])

    @pl.when(j == 0)
    def _():
        acc_ref[...] = part

    @pl.when(j > 0)
    def _():
        acc_ref[...] += part

    @pl.when(j == pl.num_programs(1) - 1)
    def _():
        o_ref[...] = x_ref[...] + 0.5 * _rms(acc_ref[...], gpost_ref[...])


def _ffn(x, gpre, gpost, wg, wu, wd):
    m = x.shape[0]
    return pl.pallas_call(
        _ffn_kernel,
        grid=(m // TM, D_FF // TF),
        in_specs=[
            pl.BlockSpec((TM, D_MODEL), lambda i, j: (i, 0)),
            pl.BlockSpec((1, D_MODEL), lambda i, j: (0, 0)),
            pl.BlockSpec((1, D_MODEL), lambda i, j: (0, 0)),
            pl.BlockSpec((D_MODEL, TF), lambda i, j: (0, j)),
            pl.BlockSpec((D_MODEL, TF), lambda i, j: (0, j)),
            pl.BlockSpec((TF, D_MODEL), lambda i, j: (j, 0)),
        ],
        out_specs=pl.BlockSpec((TM, D_MODEL), lambda i, j: (i, 0)),
        out_shape=jax.ShapeDtypeStruct((m, D_MODEL), F32),
        scratch_shapes=[pltpu.VMEM((TM, D_MODEL), BF16), pltpu.VMEM((TM, D_MODEL), F32)],
        compiler_params=pltpu.CompilerParams(
            dimension_semantics=("parallel", "arbitrary"), vmem_limit_bytes=VMEM_LIMIT),
        name="ffn",
    )(x, gpre, gpost, wg, wu, wd)


def _proj_kernel(x_ref, g_ref, w_ref, wa_ref, ba_ref, z_ref, la_ref, xn_ref):
    j = pl.program_id(1)

    @pl.when(j == 0)
    def _():
        xn_ref[...] = _rms(x_ref[...], g_ref[...]).astype(BF16)

    z = _dot(xn_ref[...], w_ref[...])
    z_ref[...] = z

    @pl.when(j == pl.num_programs(1) - 1)
    def _():
        lr = z[:, TN - LR_PAD:].astype(BF16)
        logit = _dot(lr, wa_ref[...]) + ba_ref[...]
        log_sig = jnp.minimum(logit, 0.0) - jnp.log1p(jnp.exp(-jnp.abs(logit)))
        la_ref[...] = log_sig * (1.0 / GLA_GATE_NORM)


def _proj(x, gain, w_in_p, wa_p, ba):
    m = x.shape[0]
    return pl.pallas_call(
        _proj_kernel,
        grid=(m // TM, Z_WIDTH // TN),
        in_specs=[
            pl.BlockSpec((TM, D_MODEL), lambda i, j: (i, 0)),
            pl.BlockSpec((1, D_MODEL), lambda i, j: (0, 0)),
            pl.BlockSpec((D_MODEL, TN), lambda i, j: (0, j)),
            pl.BlockSpec((LR_PAD, GLA_KWIDTH), lambda i, j: (0, 0)),
            pl.BlockSpec((1, GLA_KWIDTH), lambda i, j: (0, 0)),
        ],
        out_specs=[
            pl.BlockSpec((TM, TN), lambda i, j: (i, j)),
            pl.BlockSpec((TM, GLA_KWIDTH), lambda i, j: (i, 0)),
        ],
        out_shape=[
            jax.ShapeDtypeStruct((m, Z_WIDTH), F32),
            jax.ShapeDtypeStruct((m, GLA_KWIDTH), F32),
        ],
        scratch_shapes=[pltpu.VMEM((TM, D_MODEL), BF16)],
        compiler_params=pltpu.CompilerParams(
            dimension_semantics=("parallel", "arbitrary"), vmem_limit_bytes=VMEM_LIMIT),
        name="proj",
    )(x, gain, w_in_p, wa_p, ba)


def _attn_kernel(q_ref, kc_ref, kp_ref, vc_ref, vp_ref, sink_ref, o_ref):
    i = pl.program_id(1)
    row = lax.broadcasted_iota(jnp.int32, (BLOCK, 2 * BLOCK), 0)
    col = lax.broadcasted_iota(jnp.int32, (BLOCK, 2 * BLOCK), 1)
    diff = row + BLOCK - col
    key_pos = col + (i - 1) * BLOCK
    mask = (diff >= 0) & (diff <= WINDOW) & (key_pos >= PAD)
    for kv in range(ATT_KV_HEADS):
        ks = slice(kv * ATT_HEAD_DIM, (kv + 1) * ATT_HEAD_DIM)
        kk = jnp.concatenate([kp_ref[:, ks], kc_ref[:, ks]], axis=0).astype(BF16)
        vv = jnp.concatenate([vp_ref[:, ks], vc_ref[:, ks]], axis=0).astype(BF16)
        for g in range(ATT_GROUP):
            h = kv * ATT_GROUP + g
            hs = slice(h * ATT_HEAD_DIM, (h + 1) * ATT_HEAD_DIM)
            sink = sink_ref[:, h:h + 1]
            s = _dot_nt(q_ref[:, hs].astype(BF16), kk)
            s = jnp.where(mask, s, -jnp.inf)
            mx = jnp.maximum(jnp.max(s, axis=-1, keepdims=True), sink)
            p = jnp.exp(s - mx)
            den = jnp.sum(p, axis=-1, keepdims=True) + jnp.exp(sink - mx)
            o_ref[:, hs] = (_dot(p.astype(BF16), vv) / den).astype(o_ref.dtype)


def _attn_prompt(z, sinks, batch, nb):
    def cur(c):
        return lambda b, i: (b * nb + i, c)

    def prev(c):
        return lambda b, i: (b * nb + jnp.maximum(i - 1, 0), c)

    return pl.pallas_call(
        _attn_kernel,
        grid=(batch, nb),
        in_specs=[
            pl.BlockSpec((BLOCK, ATT_WIDTH), cur(COL_QA // ATT_WIDTH)),
            pl.BlockSpec((BLOCK, KV_WIDTH), cur(COL_KA // KV_WIDTH)),
            pl.BlockSpec((BLOCK, KV_WIDTH), prev(COL_KA // KV_WIDTH)),
            pl.BlockSpec((BLOCK, KV_WIDTH), cur(COL_VA // KV_WIDTH)),
            pl.BlockSpec((BLOCK, KV_WIDTH), prev(COL_VA // KV_WIDTH)),
            pl.BlockSpec((1, ATT_HEADS), lambda b, i: (0, 0)),
        ],
        out_specs=pl.BlockSpec((BLOCK, ATT_WIDTH), lambda b, i: (b * nb + i, 0)),
        out_shape=jax.ShapeDtypeStruct((batch * nb * BLOCK, ATT_WIDTH), BF16),
        compiler_params=pltpu.CompilerParams(
            dimension_semantics=("parallel", "arbitrary"), vmem_limit_bytes=VMEM_LIMIT),
        name="attn_prompt",
    )(z, z, z, z, z, sinks)


def _cumsum_rows(la, tri):
    hi = la.astype(BF16)
    r1 = la - hi.astype(F32)
    mid = r1.astype(BF16)
    lo = (r1 - mid.astype(F32)).astype(BF16)
    return _dot(tri, hi) + _dot(tri, mid) + _dot(tri, lo)


def _gla_kernel(q_ref, k_ref, v_ref, la_ref, o_ref, st_ref, state_ref):
    c = pl.program_id(1)
    C = GLA_CHUNK

    @pl.when(c == 0)
    def _():
        state_ref[...] = jnp.zeros_like(state_ref)

    rowc = lax.broadcasted_iota(jnp.int32, (C, C), 0)
    colc = lax.broadcasted_iota(jnp.int32, (C, C), 1)
    tri = (rowc >= colc).astype(BF16)
    b_all = _cumsum_rows(la_ref[...], tri)

    pos = c * C + lax.broadcasted_iota(jnp.int32, (C, 1), 0)
    valid = (pos >= PAD).astype(F32)

    lane = lax.broadcasted_iota(jnp.int32, (GLA_SUB, C), 1)
    rsub = lax.broadcasted_iota(jnp.int32, (GLA_SUB, C), 0)
    scale = GLA_DK ** -0.5

    for h in range(GLA_HEADS):
        ks = slice(h * GLA_DK, (h + 1) * GLA_DK)
        vs = slice(h * GLA_DV, (h + 1) * GLA_DV)
        q = q_ref[:, ks] * scale
        k = k_ref[:, ks] * valid
        v = v_ref[:, vs].astype(BF16)
        bh = b_all[:, ks]
        bl = bh[C - 1:C, :]
        st = state_ref[h]

        o = _dot_nt((q * jnp.exp(bh)).astype(BF16), st.astype(BF16))

        blocks = []
        for i in range(C // GLA_SUB):
            lo_r = i * GLA_SUB
            qb = q[lo_r:lo_r + GLA_SUB]
            bb = bh[lo_r:lo_r + GLA_SUB]
            w = jnp.zeros((GLA_SUB, C), F32)
            for s in range(GLA_SUB):
                r = lo_r + s
                zed = qb * k[r:r + 1] * jnp.exp(jnp.minimum(bb - bh[r:r + 1], 0.0))
                w = jnp.where(lane == r, jnp.sum(zed, axis=-1, keepdims=True), w)
            w = jnp.where(lane <= rsub + lo_r, w, 0.0)
            if i > 0:
                ref_b = bh[lo_r - 1:lo_r]
                qi = (qb * jnp.exp(bb - ref_b)).astype(BF16)
                kj = (k * jnp.exp(jnp.minimum(ref_b - bh, 0.0))).astype(BF16)
                w = jnp.where(lane < lo_r, _dot_nt(qi, kj), w)
            blocks.append(w)
        a = jnp.concatenate(blocks, axis=0).astype(BF16)
        o_ref[:, vs] = o + _dot(a, v)

        k_out = (k * jnp.exp(bl - bh)).astype(BF16)
        state_ref[h] = st * jnp.exp(bl) + _dot_tn(v, k_out)

    @pl.when(c == pl.num_programs(1) - 1)
    def _():
        st_ref[...] = state_ref[...]


def _gla_prompt(z, la, batch, seq):
    nc = seq // GLA_CHUNK

    def rows(cidx):
        return lambda b, c: (b * nc + c, cidx)

    return pl.pallas_call(
        _gla_kernel,
        grid=(batch, nc),
        in_specs=[
            pl.BlockSpec((GLA_CHUNK, GLA_KWIDTH), rows(COL_QG // GLA_KWIDTH)),
            pl.BlockSpec((GLA_CHUNK, GLA_KWIDTH), rows(COL_KG // GLA_KWIDTH)),
            pl.BlockSpec((GLA_CHUNK, GLA_WIDTH), rows(COL_VG // GLA_WIDTH)),
            pl.BlockSpec((GLA_CHUNK, GLA_KWIDTH), rows(0)),
        ],
        out_specs=[
            pl.BlockSpec((GLA_CHUNK, GLA_WIDTH), rows(0)),
            pl.BlockSpec((None, GLA_HEADS, GLA_DV, GLA_DK), lambda b, c: (b, 0, 0, 0)),
        ],
        out_shape=[
            jax.ShapeDtypeStruct((batch * seq, GLA_WIDTH), F32),
            jax.ShapeDtypeStruct((batch, GLA_HEADS, GLA_DV, GLA_DK), F32),
        ],
        scratch_shapes=[pltpu.VMEM((GLA_HEADS, GLA_DV, GLA_DK), F32)],
        compiler_params=pltpu.CompilerParams(
            dimension_semantics=("parallel", "arbitrary"), vmem_limit_bytes=VMEM_LIMIT),
        name="gla_prompt",
    )(z, z, z, la)


def _sample_kernel(qa_ref, zs_ref, la_ref, kbuf_ref, vbuf_ref, sink_ref, s0_ref,
                   att_ref, go_ref, st_ref, *, dec_seq):
    S = dec_seq
    R = S * ATT_GROUP
    t_row = lax.broadcasted_iota(jnp.int32, (R, 1), 0) // ATT_GROUP
    c_idx = lax.broadcasted_iota(jnp.int32, (R, kbuf_ref.shape[0]), 1)
    mask_c = c_idx >= t_row
    for kv in range(ATT_KV_HEADS):
        ks = slice(kv * ATT_HEAD_DIM, (kv + 1) * ATT_HEAD_DIM)
        q = qa_ref[kv]
        sink = sink_ref[kv]
        sc = _dot_nt(q.astype(BF16), kbuf_ref[:, ks].astype(BF16))
        sc = jnp.where(mask_c, sc, -jnp.inf)
        mx = jnp.maximum(jnp.max(sc, axis=-1, keepdims=True), sink)
        kn = zs_ref[:, COL_KA + kv * ATT_HEAD_DIM:COL_KA + (kv + 1) * ATT_HEAD_DIM]
        vn = zs_ref[:, COL_VA + kv * ATT_HEAD_DIM:COL_VA + (kv + 1) * ATT_HEAD_DIM]
        sn = []
        for s in range(S):
            v = jnp.sum(q * kn[s:s + 1], axis=-1, keepdims=True)
            v = jnp.where(t_row >= s, v, -jnp.inf)
            sn.append(v)
            mx = jnp.maximum(mx, v)
        pc = jnp.exp(sc - mx)
        den = jnp.sum(pc, axis=-1, keepdims=True) + jnp.exp(sink - mx)
        o = _dot(pc.astype(BF16), vbuf_ref[:, ks].astype(BF16))
        for s in range(S):
            pn = jnp.exp(sn[s] - mx)
            den = den + pn
            o = o + pn * vn[s:s + 1]
        att_ref[kv] = o / den

    rows8 = zs_ref.shape[0]
    rid = lax.broadcasted_iota(jnp.int32, (rows8, 1), 0)
    la = la_ref[...]
    run = la[0:1]
    b_all = jnp.where(rid == 0, run, 0.0)
    for t in range(1, S):
        run = run + la[t:t + 1]
        b_all = jnp.where(rid == t, run, b_all)
    real = (rid < S).astype(F32)
    scale = GLA_DK ** -0.5
    zpad = jnp.zeros((GLA_DK - 3 * rows8, GLA_DK), F32)
    for h in range(GLA_HEADS):
        ks = slice(h * GLA_DK, (h + 1) * GLA_DK)
        q = zs_ref[:, COL_QG + h * GLA_DK:COL_QG + (h + 1) * GLA_DK] * scale
        k = zs_ref[:, COL_KG + h * GLA_DK:COL_KG + (h + 1) * GLA_DK]
        v = zs_ref[:, COL_VG + h * GLA_DV:COL_VG + (h + 1) * GLA_DV]
        bh = b_all[:, ks] * real
        bl = bh[S - 1:S]
        s0 = s0_ref[h]
        e_l = jnp.broadcast_to(jnp.exp(bl), (rows8, GLA_DK))
        k_out = k * jnp.exp(jnp.minimum(bl - bh, 0.0)) * real
        q_in = q * jnp.exp(bh) * real
        cols = jnp.concatenate([e_l, k_out, q_in, zpad], axis=0).T
        o = jnp.zeros((rows8, GLA_DV), F32)
        for t in range(S):
            o_t = jnp.sum(cols[:, 2 * rows8 + t:2 * rows8 + t + 1] * s0, axis=0, keepdims=True)
            o = jnp.where(rid == t, o_t, o)
        for s in range(S):
            zed = q * k[s:s + 1] * jnp.exp(jnp.minimum(bh - bh[s:s + 1], 0.0))
            w = jnp.sum(zed, axis=-1, keepdims=True)
            w = jnp.where((rid >= s) & (rid < S), w, 0.0)
            o = o + w * v[s:s + 1]
        go_ref[:, h * GLA_DV:(h + 1) * GLA_DV] = o
        st = cols[:, 0:1] * s0
        for t in range(S):
            st = st + cols[:, rows8 + t:rows8 + t + 1] * v[t:t + 1]
        st_ref[h] = st


def _sample_mixer(qa_s, zs8, la8, kbuf, vbuf, sink_rows, s0, dec_seq):
    nb = zs8.shape[0]
    rows8 = zs8.shape[1]
    R = dec_seq * ATT_GROUP
    wb = kbuf.shape[1]
    return pl.pallas_call(
        functools.partial(_sample_kernel, dec_seq=dec_seq),
        grid=(nb,),
        in_specs=[
            pl.BlockSpec((None, ATT_KV_HEADS, R, ATT_HEAD_DIM), lambda b: (b, 0, 0, 0)),
            pl.BlockSpec((None, rows8, Z_WIDTH), lambda b: (b, 0, 0)),
            pl.BlockSpec((None, rows8, GLA_KWIDTH), lambda b: (b, 0, 0)),
            pl.BlockSpec((None, wb, KV_WIDTH), lambda b: (b, 0, 0)),
            pl.BlockSpec((None, wb, KV_WIDTH), lambda b: (b, 0, 0)),
            pl.BlockSpec((ATT_KV_HEADS, R, 1), lambda b: (0, 0, 0)),
            pl.BlockSpec((None, GLA_HEADS, GLA_DK, GLA_DV), lambda b: (b, 0, 0, 0)),
        ],
        out_specs=[
            pl.BlockSpec((None, ATT_KV_HEADS, R, ATT_HEAD_DIM), lambda b: (b, 0, 0, 0)),
            pl.BlockSpec((None, rows8, GLA_WIDTH), lambda b: (b, 0, 0)),
            pl.BlockSpec((None, GLA_HEADS, GLA_DK, GLA_DV), lambda b: (b, 0, 0, 0)),
        ],
        out_shape=[
            jax.ShapeDtypeStruct((nb, ATT_KV_HEADS, R, ATT_HEAD_DIM), F32),
            jax.ShapeDtypeStruct((nb, rows8, GLA_WIDTH), F32),
            jax.ShapeDtypeStruct((nb, GLA_HEADS, GLA_DK, GLA_DV), F32),
        ],
        compiler_params=pltpu.CompilerParams(
            dimension_semantics=("parallel",), vmem_limit_bytes=VMEM_LIMIT),
        name="sample_mixer",
    )(qa_s, zs8, la8, kbuf, vbuf, sink_rows, s0)


def _merge_kernel(x_ref, att_ref, go_ref, rg_ref, ggla_ref, wo_ref, gpost_ref, o_ref):
    parts = []
    for h in range(GLA_HEADS):
        vs = slice(h * GLA_DV, (h + 1) * GLA_DV)
        rg = rg_ref[:, vs]
        parts.append((_rms(go_ref[:, vs], ggla_ref[...]) * (rg * jax.nn.sigmoid(rg))).astype(BF16))
    go = jnp.concatenate(parts, axis=-1)
    y = _dot(att_ref[...], wo_ref[:ATT_WIDTH, :]) + _dot(go, wo_ref[ATT_WIDTH:, :])
    o_ref[...] = x_ref[...] + _rms(y, gpost_ref[...])


def _merge(x, att, go, z, g_gla, w_out, gpost):
    m = x.shape[0]
    return pl.pallas_call(
        _merge_kernel,
        grid=(m // TM,),
        in_specs=[
            pl.BlockSpec((TM, D_MODEL), lambda i: (i, 0)),
            pl.BlockSpec((TM, ATT_WIDTH), lambda i: (i, 0)),
            pl.BlockSpec((TM, GLA_WIDTH), lambda i: (i, 0)),
            pl.BlockSpec((TM, GLA_WIDTH), lambda i: (i, COL_RG // GLA_WIDTH)),
            pl.BlockSpec((1, GLA_DV), lambda i: (0, 0)),
            pl.BlockSpec((ATT_WIDTH + GLA_WIDTH, D_MODEL), lambda i: (0, 0)),
            pl.BlockSpec((1, D_MODEL), lambda i: (0, 0)),
        ],
        out_specs=pl.BlockSpec((TM, D_MODEL), lambda i: (i, 0)),
        out_shape=jax.ShapeDtypeStruct((m, D_MODEL), F32),
        compiler_params=pltpu.CompilerParams(
            dimension_semantics=("parallel",), vmem_limit_bytes=VMEM_LIMIT),
        name="merge",
    )(x, att, go, z, g_gla, w_out, gpost)


def _permute_w_in(w):
    o = 0
    parts = {}
    for name, width in (("qa", ATT_WIDTH), ("ka", KV_WIDTH), ("va", KV_WIDTH), ("qg", GLA_KWIDTH),
                        ("kg", GLA_KWIDTH), ("vg", GLA_WIDTH), ("rg", GLA_WIDTH), ("lr", GLA_RANK)):
        parts[name] = w[..., o:o + width]
        o += width
    pad = jnp.zeros(w.shape[:-1] + (LR_PAD - GLA_RANK,), w.dtype)
    out = jnp.concatenate([parts["qa"] * (ATT_HEAD_DIM ** -0.5), parts["vg"], parts["rg"], parts["qg"],
                           parts["kg"], parts["ka"], parts["va"], parts["lr"], pad], axis=-1)
    return out.astype(BF16)


def kernel(x_prompt, x_sample, cache_k_win, cache_v_win, state_gla, meta_tokens, norm_gains,
           w_ffn_gate, w_ffn_up, w_ffn_down, w_in, w_gate_up, b_gate, attn_sinks, gla_norm, w_out):
    batch, seq, _ = x_prompt.shape
    dec_batch, dec_seq, _ = x_sample.shape
    depth = norm_gains.shape[0]
    lp = seq + BLOCK
    nb = lp // BLOCK
    mp = batch * lp
    ms = dec_batch * dec_seq
    m_pad = -(-(mp + ms) // TM) * TM
    wb = cache_k_win.shape[2]
    rows8 = 8
    assert dec_seq <= rows8 and lp % GLA_CHUNK == 0

    head = jnp.concatenate([jnp.zeros((PAD, D_MODEL), F32), meta_tokens.astype(F32)], axis=0)
    x = jnp.concatenate(
        [jnp.concatenate([jnp.broadcast_to(head[None], (batch, BLOCK, D_MODEL)), x_prompt], axis=1).reshape(mp, D_MODEL),
         x_sample.reshape(ms, D_MODEL),
         jnp.zeros((m_pad - mp - ms, D_MODEL), F32)], axis=0)

    wg = w_ffn_gate.astype(BF16)
    wu = w_ffn_up.astype(BF16)
    wd = w_ffn_down.astype(BF16)
    w_in_p = _permute_w_in(w_in)
    wa_p = jnp.concatenate([w_gate_up, jnp.zeros((depth, LR_PAD - GLA_RANK, GLA_KWIDTH), F32)], axis=1).astype(BF16)
    wo = w_out.astype(BF16)
    gains = norm_gains.reshape(depth, 6, 1, D_MODEL)
    tail = jnp.zeros((m_pad - mp - ms, GLA_WIDTH), F32)

    pk, pv, ps, sk, sv, ss = [], [], [], [], [], []
    for l in range(depth):
        g = gains[l]
        x = _ffn(x, g[0], g[1], wg[l, 0], wu[l, 0], wd[l, 0])

        z, la = _proj(x, g[2], w_in_p[l], wa_p[l], b_gate[l].reshape(1, GLA_KWIDTH))
        att_p = _attn_prompt(z, attn_sinks[l].reshape(1, ATT_HEADS), batch, nb)
        go_p, st_p = _gla_prompt(z, la, batch, lp)

        zs = z[mp:mp + ms].reshape(dec_batch, dec_seq, Z_WIDTH)
        zs8 = jnp.pad(zs, ((0, 0), (0, rows8 - dec_seq), (0, 0)))
        la8 = jnp.pad(la[mp:mp + ms].reshape(dec_batch, dec_seq, GLA_KWIDTH), ((0, 0), (0, rows8 - dec_seq), (0, 0)))
        qa_s = zs[:, :, COL_QA:COL_QA + ATT_WIDTH].reshape(dec_batch, dec_seq, ATT_KV_HEADS, ATT_GROUP, ATT_HEAD_DIM)
        qa_s = qa_s.transpose(0, 2, 1, 3, 4).reshape(dec_batch, ATT_KV_HEADS, dec_seq * ATT_GROUP, ATT_HEAD_DIM)
        sink_rows = jnp.tile(attn_sinks[l].reshape(ATT_KV_HEADS, 1, ATT_GROUP), (1, dec_seq, 1))
        sink_rows = sink_rows.reshape(ATT_KV_HEADS, dec_seq * ATT_GROUP, 1)
        kbuf = cache_k_win[l].reshape(dec_batch, wb, KV_WIDTH)
        vbuf = cache_v_win[l].reshape(dec_batch, wb, KV_WIDTH)
        att_s, go_s, st_s = _sample_mixer(qa_s, zs8, la8, kbuf, vbuf, sink_rows, state_gla[l], dec_seq)
        att_s = att_s.reshape(dec_batch, ATT_KV_HEADS, dec_seq, ATT_GROUP, ATT_HEAD_DIM)
        att_s = att_s.transpose(0, 2, 1, 3, 4).reshape(ms, ATT_WIDTH)

        att = jnp.concatenate([att_p, att_s.astype(BF16), tail.astype(BF16)], axis=0)
        go = jnp.concatenate([go_p, go_s[:, :dec_seq].reshape(ms, GLA_WIDTH), tail], axis=0)
        x = _merge(x, att, go, z, gla_norm[l].reshape(1, GLA_DV), wo[l], g[3])

        zp = z[:mp].reshape(batch, lp, Z_WIDTH)
        pk.append(zp[:, lp - WINDOW:, COL_KA:COL_KA + KV_WIDTH].reshape(batch, WINDOW, ATT_KV_HEADS, ATT_HEAD_DIM))
        pv.append(zp[:, lp - WINDOW:, COL_VA:COL_VA + KV_WIDTH].reshape(batch, WINDOW, ATT_KV_HEADS, ATT_HEAD_DIM))
        ps.append(st_p.transpose(0, 1, 3, 2))
        kn = zs[:, :, COL_KA:COL_KA + KV_WIDTH].reshape(dec_batch, dec_seq, ATT_KV_HEADS, ATT_HEAD_DIM)
        vn = zs[:, :, COL_VA:COL_VA + KV_WIDTH].reshape(dec_batch, dec_seq, ATT_KV_HEADS, ATT_HEAD_DIM)
        sk.append(jnp.concatenate([cache_k_win[l], kn], axis=1)[:, -wb:])
        sv.append(jnp.concatenate([cache_v_win[l], vn], axis=1)[:, -wb:])
        ss.append(st_s)

        x = _ffn(x, g[4], g[5], wg[l, 1], wu[l, 1], wd[l, 1])

    y_prompt = x[:mp].reshape(batch, lp, D_MODEL)[:, BLOCK:]
    y_sample = x[mp:mp + ms].reshape(dec_batch, dec_seq, D_MODEL)
    return (y_prompt, y_sample, jnp.stack(pk), jnp.stack(pv), jnp.stack(ps),
            jnp.stack(sk), jnp.stack(sv), jnp.stack(ss))
```

```python
import functools

import jax
import jax.numpy as jnp
from jax import lax
from jax.experimental import pallas as pl
from jax.experimental.pallas import tpu as pltpu

F32 = jnp.float32
BF16 = jnp.bfloat16

D_MODEL = 2048
D_FF = 5632
N_META = 16
BLOCK = 128
WINDOW = 128
ATT_HEADS = 16
ATT_KV_HEADS = 2
ATT_GROUP = 8
ATT_HEAD_DIM = 64
ATT_WIDTH = 1024
KV_WIDTH = 128
GLA_HEADS = 4
GLA_DK = 128
GLA_DV = 256
GLA_KWIDTH = 512
GLA_WIDTH = 1024
GLA_RANK = 16
GLA_GATE_NORM = 16.0
GLA_CHUNK = 64
GLA_SUB = 16
EPS = 1e-6
PAD = BLOCK - N_META

COL_QA, COL_VG, COL_RG, COL_QG, COL_KG, COL_KA, COL_VA, COL_LR = 0, 1024, 2048, 3072, 3584, 4096, 4224, 4352
Z_WIDTH = 4608
LR_PAD = 128

VMEM_LIMIT = 56 * 1024 * 1024

TM = 1072
TM_MERGE = 536
TF = 512
TN = 768


def _rms(x, gain):
    ms = jnp.mean(x * x, axis=-1, keepdims=True)
    return x * lax.rsqrt(ms + EPS) * gain


def _dot(a, b):
    return jnp.dot(a, b, preferred_element_type=F32)


def _dot_nt(a, b):
    return lax.dot_general(a, b, (((1,), (1,)), ((), ())), preferred_element_type=F32)


def _dot_tn(a, b):
    return lax.dot_general(a, b, (((0,), (0,)), ((), ())), preferred_element_type=F32)


def _ffn_kernel(x_ref, gpre_ref, gpost_ref, wg_ref, wu_ref, wd_ref, o_ref, xn_ref):
    j = pl.program_id(1)

    @pl.when(j == 0)
    def _():
        xn_ref[...] = _rms(x_ref[...], gpre_ref[...]).astype(BF16)
        o_ref[...] = jnp.zeros_like(o_ref)

    xn = xn_ref[...]
    g = _dot(xn, wg_ref[...])
    u = _dot(xn, wu_ref[...])
    h = (g * jax.nn.sigmoid(g) * u).astype(BF16)
    o_ref[...] += _dot(h, wd_ref[...])

    @pl.when(j == pl.num_programs(1) - 1)
    def _():
        o_ref[...] = x_ref[...] + 0.5 * _rms(o_ref[...], gpost_ref[...])


def _ffn(x, gpre, gpost, wg, wu, wd):
    m = x.shape[0]
    return pl.pallas_call(
        _ffn_kernel,
        grid=(m // TM, D_FF // TF),
        in_specs=[
            pl.BlockSpec((TM, D_MODEL), lambda i, j: (i, 0), pipeline_mode=pl.Buffered(1)),
            pl.BlockSpec((1, D_MODEL), lambda i, j: (0, 0)),
            pl.BlockSpec((1, D_MODEL), lambda i, j: (0, 0)),
            pl.BlockSpec((D_MODEL, TF), lambda i, j: (0, j)),
            pl.BlockSpec((D_MODEL, TF), lambda i, j: (0, j)),
            pl.BlockSpec((TF, D_MODEL), lambda i, j: (j, 0)),
        ],
        out_specs=pl.BlockSpec((TM, D_MODEL), lambda i, j: (i, 0)),
        out_shape=jax.ShapeDtypeStruct((m, D_MODEL), F32),
        scratch_shapes=[pltpu.VMEM((TM, D_MODEL), BF16)],
        compiler_params=pltpu.CompilerParams(
            dimension_semantics=("parallel", "arbitrary"), vmem_limit_bytes=VMEM_LIMIT),
        name="ffn",
    )(x, gpre, gpost, wg, wu, wd)


def _proj_kernel(x_ref, g_ref, w_ref, wa_ref, ba_ref, z_ref, la_ref, xn_ref):
    j = pl.program_id(1)

    @pl.when(j == 0)
    def _():
        xn_ref[...] = _rms(x_ref[...], g_ref[...]).astype(BF16)

    z = _dot(xn_ref[...], w_ref[...])
    z_ref[...] = z

    @pl.when(j == pl.num_programs(1) - 1)
    def _():
        lr_off = COL_LR - (Z_WIDTH - TN)
        lr = z[:, lr_off:lr_off + LR_PAD].astype(BF16)
        logit = _dot(lr, wa_ref[...]) + ba_ref[...]
        log_sig = jnp.minimum(logit, 0.0) - jnp.log1p(jnp.exp(-jnp.abs(logit)))
        la_ref[...] = log_sig * (1.0 / GLA_GATE_NORM)


def _proj(x, gain, w_in_p, wa_p, ba):
    m = x.shape[0]
    return pl.pallas_call(
        _proj_kernel,
        grid=(m // TM, Z_WIDTH // TN),
        in_specs=[
            pl.BlockSpec((TM, D_MODEL), lambda i, j: (i, 0)),
            pl.BlockSpec((1, D_MODEL), lambda i, j: (0, 0)),
            pl.BlockSpec((D_MODEL, TN), lambda i, j: (0, j)),
            pl.BlockSpec((LR_PAD, GLA_KWIDTH), lambda i, j: (0, 0)),
            pl.BlockSpec((1, GLA_KWIDTH), lambda i, j: (0, 0)),
        ],
        out_specs=[
            pl.BlockSpec((TM, TN), lambda i, j: (i, j)),
            pl.BlockSpec((TM, GLA_KWIDTH), lambda i, j: (i, 0)),
        ],
        out_shape=[
            jax.ShapeDtypeStruct((m, Z_WIDTH), F32),
            jax.ShapeDtypeStruct((m, GLA_KWIDTH), F32),
        ],
        scratch_shapes=[pltpu.VMEM((TM, D_MODEL), BF16)],
        compiler_params=pltpu.CompilerParams(
            dimension_semantics=("parallel", "arbitrary"), vmem_limit_bytes=VMEM_LIMIT),
        name="proj",
    )(x, gain, w_in_p, wa_p, ba)


def _pair_blockdiag(x128, kv):
    lane = lax.broadcasted_iota(jnp.int32, x128.shape, 1)
    own = jnp.where((lane >= kv * ATT_HEAD_DIM) & (lane < (kv + 1) * ATT_HEAD_DIM), x128, 0.0)
    other = pltpu.roll(own, ATT_HEAD_DIM, axis=1)
    lo, hi = (own, other) if kv == 0 else (other, own)
    return jnp.concatenate([lo, hi], axis=0).astype(BF16)


def _attn_kernel(sink_ref, q_ref, kc_ref, kp_ref, vc_ref, vp_ref, o_ref):
    i = pl.program_id(1)
    kb = 2 * BLOCK
    row = lax.broadcasted_iota(jnp.int32, (BLOCK, kb), 0)
    col = lax.broadcasted_iota(jnp.int32, (BLOCK, kb), 1)
    diff = row + BLOCK - col
    key_pos = col + (i - 1) * BLOCK
    mask = (diff >= 0) & (diff <= WINDOW) & (key_pos >= PAD)
    lane = lax.broadcasted_iota(jnp.int32, (BLOCK, 2 * ATT_HEAD_DIM), 1)
    kk = jnp.concatenate([kp_ref[...], kc_ref[...]], axis=0)
    vv = jnp.concatenate([vp_ref[...], vc_ref[...]], axis=0)
    slabs = ATT_GROUP // 2
    for kv in range(ATT_KV_HEADS):
        k2 = _pair_blockdiag(kk, kv)
        v2 = _pair_blockdiag(vv, kv)
        c0 = kv * slabs * 2 * ATT_HEAD_DIM
        q4 = q_ref[:, c0:c0 + slabs * 2 * ATT_HEAD_DIM]
        q4 = jnp.concatenate([q4[:, p * 128:(p + 1) * 128] for p in range(slabs)], axis=0).astype(BF16)
        s = _dot_nt(q4, k2)
        probs, inv = [], []
        for p in range(slabs):
            halves, rden = [], []
            for e in range(2):
                sink = sink_ref[kv * ATT_GROUP + 2 * p + e]
                sp = jnp.where(mask, s[p * BLOCK:(p + 1) * BLOCK, e * kb:(e + 1) * kb], -jnp.inf)
                mx = jnp.maximum(jnp.max(sp, axis=-1, keepdims=True), sink)
                pe = jnp.exp(sp - mx)
                rden.append(1.0 / (jnp.sum(pe, axis=-1, keepdims=True) + jnp.exp(sink - mx)))
                halves.append(pe.astype(BF16))
            probs.append(jnp.concatenate(halves, axis=1))
            inv.append(jnp.where(lane < ATT_HEAD_DIM, rden[0], rden[1]))
        o = _dot(jnp.concatenate(probs, axis=0), v2)
        for p in range(slabs):
            o_ref[:, c0 + p * 128:c0 + (p + 1) * 128] = (o[p * BLOCK:(p + 1) * BLOCK] * inv[p]).astype(o_ref.dtype)


def _attn_prompt(z, sinks, batch, nb, m_rows):
    def cur(c):
        return lambda b, i, sink: (b * nb + i, c)

    def prev(c):
        return lambda b, i, sink: (b * nb + jnp.maximum(i - 1, 0), c)

    return pl.pallas_call(
        _attn_kernel,
        grid_spec=pltpu.PrefetchScalarGridSpec(
            num_scalar_prefetch=1,
            grid=(batch, nb),
            in_specs=[
                pl.BlockSpec((BLOCK, ATT_WIDTH), cur(COL_QA // ATT_WIDTH)),
                pl.BlockSpec((BLOCK, KV_WIDTH), cur(COL_KA // KV_WIDTH)),
                pl.BlockSpec((BLOCK, KV_WIDTH), prev(COL_KA // KV_WIDTH)),
                pl.BlockSpec((BLOCK, KV_WIDTH), cur(COL_VA // KV_WIDTH)),
                pl.BlockSpec((BLOCK, KV_WIDTH), prev(COL_VA // KV_WIDTH)),
            ],
            out_specs=pl.BlockSpec((BLOCK, ATT_WIDTH), lambda b, i, sink: (b * nb + i, 0)),
        ),
        out_shape=jax.ShapeDtypeStruct((m_rows, ATT_WIDTH), BF16),
        compiler_params=pltpu.CompilerParams(
            dimension_semantics=("parallel", "arbitrary"), vmem_limit_bytes=VMEM_LIMIT),
        name="attn_prompt",
    )(sinks, z, z, z, z, z)


def _cumsum_rows(la, tri):
    hi = la.astype(BF16)
    r1 = la - hi.astype(F32)
    mid = r1.astype(BF16)
    lo = (r1 - mid.astype(F32)).astype(BF16)
    return _dot(tri, hi) + _dot(tri, mid) + _dot(tri, lo)


def _gla_kernel(q_ref, k_ref, v_ref, la_ref, o_ref, st_ref, state_ref):
    c = pl.program_id(1)
    C = GLA_CHUNK

    @pl.when(c == 0)
    def _():
        state_ref[...] = jnp.zeros_like(state_ref)

    rowc = lax.broadcasted_iota(jnp.int32, (C, C), 0)
    colc = lax.broadcasted_iota(jnp.int32, (C, C), 1)
    tri = (rowc >= colc).astype(BF16)
    b_all = _cumsum_rows(la_ref[...], tri)

    pos = c * C + lax.broadcasted_iota(jnp.int32, (C, 1), 0)
    valid = (pos >= PAD).astype(F32)

    lane = lax.broadcasted_iota(jnp.int32, (GLA_SUB, C), 1)
    rsub = lax.broadcasted_iota(jnp.int32, (GLA_SUB, C), 0)
    scale = GLA_DK ** -0.5

    for h in range(GLA_HEADS):
        ks = slice(h * GLA_DK, (h + 1) * GLA_DK)
        vs = slice(h * GLA_DV, (h + 1) * GLA_DV)
        q = q_ref[:, ks] * scale
        k = k_ref[:, ks] * valid
        v = v_ref[:, vs].astype(BF16)
        bh = b_all[:, ks]
        bl = bh[C - 1:C, :]
        st = state_ref[h]

        o = _dot_nt((q * jnp.exp(bh)).astype(BF16), st.astype(BF16))

        blocks = []
        for i in range(C // GLA_SUB):
            lo_r = i * GLA_SUB
            qb = q[lo_r:lo_r + GLA_SUB]
            bb = bh[lo_r:lo_r + GLA_SUB]
            w = jnp.zeros((GLA_SUB, C), F32)
            for s in range(GLA_SUB):
                r = lo_r + s
                zed = qb * k[r:r + 1] * jnp.exp(jnp.minimum(bb - bh[r:r + 1], 0.0))
                w = jnp.where(lane == r, jnp.sum(zed, axis=-1, keepdims=True), w)
            w = jnp.where(lane <= rsub + lo_r, w, 0.0)
            if i > 0:
                ref_b = bh[lo_r - 1:lo_r]
                qi = (qb * jnp.exp(bb - ref_b)).astype(BF16)
                kj = (k * jnp.exp(jnp.minimum(ref_b - bh, 0.0))).astype(BF16)
                w = jnp.where(lane < lo_r, _dot_nt(qi, kj), w)
            blocks.append(w)
        a = jnp.concatenate(blocks, axis=0).astype(BF16)
        o_ref[:, vs] = o + _dot(a, v)

        k_out = (k * jnp.exp(bl - bh)).astype(BF16)
        state_ref[h] = st * jnp.exp(bl) + _dot_tn(v, k_out)

    @pl.when(c == pl.num_programs(1) - 1)
    def _():
        st_ref[...] = state_ref[...]


def _gla_prompt(z, la, batch, seq, m_rows):
    nc = seq // GLA_CHUNK

    def rows(cidx):
        return lambda b, c: (b * nc + c, cidx)

    return pl.pallas_call(
        _gla_kernel,
        grid=(batch, nc),
        in_specs=[
            pl.BlockSpec((GLA_CHUNK, GLA_KWIDTH), rows(COL_QG // GLA_KWIDTH)),
            pl.BlockSpec((GLA_CHUNK, GLA_KWIDTH), rows(COL_KG // GLA_KWIDTH)),
            pl.BlockSpec((GLA_CHUNK, GLA_WIDTH), rows(COL_VG // GLA_WIDTH)),
            pl.BlockSpec((GLA_CHUNK, GLA_KWIDTH), rows(0)),
        ],
        out_specs=[
            pl.BlockSpec((GLA_CHUNK, GLA_WIDTH), rows(0)),
            pl.BlockSpec((None, GLA_HEADS, GLA_DV, GLA_DK), lambda b, c: (b, 0, 0, 0)),
        ],
        out_shape=[
            jax.ShapeDtypeStruct((m_rows, GLA_WIDTH), F32),
            jax.ShapeDtypeStruct((batch, GLA_HEADS, GLA_DV, GLA_DK), F32),
        ],
        scratch_shapes=[pltpu.VMEM((GLA_HEADS, GLA_DV, GLA_DK), F32)],
        compiler_params=pltpu.CompilerParams(
            dimension_semantics=("parallel", "arbitrary"), vmem_limit_bytes=VMEM_LIMIT),
        name="gla_prompt",
    )(z, z, z, la)


def _sample_kernel(qa_ref, zs_ref, la_ref, kbuf_ref, vbuf_ref, sink_ref, s0_ref,
                   att_ref, go_ref, st_ref, *, dec_seq):
    S = dec_seq
    R = S * ATT_GROUP
    t_row = lax.broadcasted_iota(jnp.int32, (R, 1), 0) // ATT_GROUP
    c_idx = lax.broadcasted_iota(jnp.int32, (R, kbuf_ref.shape[0]), 1)
    mask_c = c_idx >= t_row
    for kv in range(ATT_KV_HEADS):
        ks = slice(kv * ATT_HEAD_DIM, (kv + 1) * ATT_HEAD_DIM)
        q = qa_ref[kv]
        sink = sink_ref[kv]
        sc = _dot_nt(q.astype(BF16), kbuf_ref[:, ks].astype(BF16))
        sc = jnp.where(mask_c, sc, -jnp.inf)
        mx = jnp.maximum(jnp.max(sc, axis=-1, keepdims=True), sink)
        kn = zs_ref[:, COL_KA + kv * ATT_HEAD_DIM:COL_KA + (kv + 1) * ATT_HEAD_DIM]
        vn = zs_ref[:, COL_VA + kv * ATT_HEAD_DIM:COL_VA + (kv + 1) * ATT_HEAD_DIM]
        sn = []
        for s in range(S):
            v = jnp.sum(q * kn[s:s + 1], axis=-1, keepdims=True)
            v = jnp.where(t_row >= s, v, -jnp.inf)
            sn.append(v)
            mx = jnp.maximum(mx, v)
        pc = jnp.exp(sc - mx)
        den = jnp.sum(pc, axis=-1, keepdims=True) + jnp.exp(sink - mx)
        o = _dot(pc.astype(BF16), vbuf_ref[:, ks].astype(BF16))
        for s in range(S):
            pn = jnp.exp(sn[s] - mx)
            den = den + pn
            o = o + pn * vn[s:s + 1]
        att_ref[kv] = o / den

    rows8 = zs_ref.shape[0]
    rid = lax.broadcasted_iota(jnp.int32, (rows8, 1), 0)
    la = la_ref[...]
    run = la[0:1]
    b_all = jnp.where(rid == 0, run, 0.0)
    for t in range(1, S):
        run = run + la[t:t + 1]
        b_all = jnp.where(rid == t, run, b_all)
    real = (rid < S).astype(F32)
    scale = GLA_DK ** -0.5
    zpad = jnp.zeros((GLA_DK - 3 * rows8, GLA_DK), F32)
    for h in range(GLA_HEADS):
        ks = slice(h * GLA_DK, (h + 1) * GLA_DK)
        q = zs_ref[:, COL_QG + h * GLA_DK:COL_QG + (h + 1) * GLA_DK] * scale
        k = zs_ref[:, COL_KG + h * GLA_DK:COL_KG + (h + 1) * GLA_DK]
        v = zs_ref[:, COL_VG + h * GLA_DV:COL_VG + (h + 1) * GLA_DV]
        bh = b_all[:, ks] * real
        bl = bh[S - 1:S]
        s0 = s0_ref[h]
        e_l = jnp.broadcast_to(jnp.exp(bl), (rows8, GLA_DK))
        k_out = k * jnp.exp(jnp.minimum(bl - bh, 0.0)) * real
        q_in = q * jnp.exp(bh) * real
        cols = jnp.concatenate([e_l, k_out, q_in, zpad], axis=0).T
        o = jnp.zeros((rows8, GLA_DV), F32)
        for t in range(S):
            o_t = jnp.sum(cols[:, 2 * rows8 + t:2 * rows8 + t + 1] * s0, axis=0, keepdims=True)
            o = jnp.where(rid == t, o_t, o)
        for s in range(S):
            zed = q * k[s:s + 1] * jnp.exp(jnp.minimum(bh - bh[s:s + 1], 0.0))
            w = jnp.sum(zed, axis=-1, keepdims=True)
            w = jnp.where((rid >= s) & (rid < S), w, 0.0)
            o = o + w * v[s:s + 1]
        go_ref[:, h * GLA_DV:(h + 1) * GLA_DV] = o
        st = cols[:, 0:1] * s0
        for t in range(S):
            st = st + cols[:, rows8 + t:rows8 + t + 1] * v[t:t + 1]
        st_ref[h] = st


def _sample_mixer(qa_s, zs8, la8, kbuf, vbuf, sink_rows, s0, dec_seq):
    nb = zs8.shape[0]
    rows8 = zs8.shape[1]
    R = dec_seq * ATT_GROUP
    wb = kbuf.shape[1]
    return pl.pallas_call(
        functools.partial(_sample_kernel, dec_seq=dec_seq),
        grid=(nb,),
        in_specs=[
            pl.BlockSpec((None, ATT_KV_HEADS, R, ATT_HEAD_DIM), lambda b: (b, 0, 0, 0)),
            pl.BlockSpec((None, rows8, Z_WIDTH), lambda b: (b, 0, 0)),
            pl.BlockSpec((None, rows8, GLA_KWIDTH), lambda b: (b, 0, 0)),
            pl.BlockSpec((None, wb, KV_WIDTH), lambda b: (b, 0, 0)),
            pl.BlockSpec((None, wb, KV_WIDTH), lambda b: (b, 0, 0)),
            pl.BlockSpec((ATT_KV_HEADS, R, 1), lambda b: (0, 0, 0)),
            pl.BlockSpec((None, GLA_HEADS, GLA_DK, GLA_DV), lambda b: (b, 0, 0, 0)),
        ],
        out_specs=[
            pl.BlockSpec((None, ATT_KV_HEADS, R, ATT_HEAD_DIM), lambda b: (b, 0, 0, 0)),
            pl.BlockSpec((None, rows8, GLA_WIDTH), lambda b: (b, 0, 0)),
            pl.BlockSpec((None, GLA_HEADS, GLA_DK, GLA_DV), lambda b: (b, 0, 0, 0)),
        ],
        out_shape=[
            jax.ShapeDtypeStruct((nb, ATT_KV_HEADS, R, ATT_HEAD_DIM), F32),
            jax.ShapeDtypeStruct((nb, rows8, GLA_WIDTH), F32),
            jax.ShapeDtypeStruct((nb, GLA_HEADS, GLA_DK, GLA_DV), F32),
        ],
        compiler_params=pltpu.CompilerParams(
            dimension_semantics=("parallel",), vmem_limit_bytes=VMEM_LIMIT),
        name="sample_mixer",
    )(qa_s, zs8, la8, kbuf, vbuf, sink_rows, s0)


def _merge_kernel(x_ref, att_ref, go_ref, rg_ref, ggla_ref, wo_ref, gpost_ref, o_ref):
    parts = []
    for h in range(GLA_HEADS):
        vs = slice(h * GLA_DV, (h + 1) * GLA_DV)
        rg = rg_ref[:, vs]
        parts.append((_rms(go_ref[:, vs], ggla_ref[...]) * (rg * jax.nn.sigmoid(rg))).astype(BF16))
    go = jnp.concatenate(parts, axis=-1)
    y = _dot(att_ref[...], wo_ref[:ATT_WIDTH, :]) + _dot(go, wo_ref[ATT_WIDTH:, :])
    o_ref[...] = x_ref[...] + _rms(y, gpost_ref[...])


def _merge(x, att, go, z, g_gla, w_out, gpost):
    m = x.shape[0]
    return pl.pallas_call(
        _merge_kernel,
        grid=(m // TM_MERGE,),
        in_specs=[
            pl.BlockSpec((TM_MERGE, D_MODEL), lambda i: (i, 0)),
            pl.BlockSpec((TM_MERGE, ATT_WIDTH), lambda i: (i, 0)),
            pl.BlockSpec((TM_MERGE, GLA_WIDTH), lambda i: (i, 0)),
            pl.BlockSpec((TM_MERGE, GLA_WIDTH), lambda i: (i, COL_RG // GLA_WIDTH)),
            pl.BlockSpec((1, GLA_DV), lambda i: (0, 0)),
            pl.BlockSpec((ATT_WIDTH + GLA_WIDTH, D_MODEL), lambda i: (0, 0)),
            pl.BlockSpec((1, D_MODEL), lambda i: (0, 0)),
        ],
        out_specs=pl.BlockSpec((TM_MERGE, D_MODEL), lambda i: (i, 0)),
        out_shape=jax.ShapeDtypeStruct((m, D_MODEL), F32),
        compiler_params=pltpu.CompilerParams(
            dimension_semantics=("parallel",), vmem_limit_bytes=VMEM_LIMIT),
        name="merge",
    )(x, att, go, z, g_gla, w_out, gpost)


def _permute_w_in(w):
    o = 0
    parts = {}
    for name, width in (("qa", ATT_WIDTH), ("ka", KV_WIDTH), ("va", KV_WIDTH), ("qg", GLA_KWIDTH),
                        ("kg", GLA_KWIDTH), ("vg", GLA_WIDTH), ("rg", GLA_WIDTH), ("lr", GLA_RANK)):
        parts[name] = w[..., o:o + width]
        o += width
    pad = jnp.zeros(w.shape[:-1] + (Z_WIDTH - COL_LR - GLA_RANK,), w.dtype)
    out = jnp.concatenate([parts["qa"] * (ATT_HEAD_DIM ** -0.5), parts["vg"], parts["rg"], parts["qg"],
                           parts["kg"], parts["ka"], parts["va"], parts["lr"], pad], axis=-1)
    return out.astype(BF16)


def kernel(x_prompt, x_sample, cache_k_win, cache_v_win, state_gla, meta_tokens, norm_gains,
           w_ffn_gate, w_ffn_up, w_ffn_down, w_in, w_gate_up, b_gate, attn_sinks, gla_norm, w_out):
    batch, seq, _ = x_prompt.shape
    dec_batch, dec_seq, _ = x_sample.shape
    depth = norm_gains.shape[0]
    lp = seq + BLOCK
    nb = lp // BLOCK
    mp = batch * lp
    ms = dec_batch * dec_seq
    m_pad = -(-(mp + ms) // TM) * TM
    wb = cache_k_win.shape[2]
    rows8 = 8
    assert dec_seq <= rows8 and lp % GLA_CHUNK == 0

    head = jnp.concatenate([jnp.zeros((PAD, D_MODEL), F32), meta_tokens.astype(F32)], axis=0)
    x = jnp.concatenate(
        [jnp.concatenate([jnp.broadcast_to(head[None], (batch, BLOCK, D_MODEL)), x_prompt], axis=1).reshape(mp, D_MODEL),
         x_sample.reshape(ms, D_MODEL),
         jnp.zeros((m_pad - mp - ms, D_MODEL), F32)], axis=0)

    wa_p = jnp.concatenate([w_gate_up, jnp.zeros((depth, LR_PAD - GLA_RANK, GLA_KWIDTH), F32)], axis=1).astype(BF16)
    gains = norm_gains.reshape(depth, 6, 1, D_MODEL)
    tail = jnp.zeros((m_pad - mp - ms, GLA_WIDTH), F32)

    def ffn(x, l, f):
        g = gains[l]
        return _ffn(x, g[4 * f], g[4 * f + 1], w_ffn_gate[l, f].astype(BF16), w_ffn_up[l, f].astype(BF16),
                    w_ffn_down[l, f].astype(BF16))

    pk, pv, ps, sk, sv, ss = [], [], [], [], [], []
    for l in range(depth):
        g = gains[l]
        x = ffn(x, l, 0)

        z, la = _proj(x, g[2], _permute_w_in(w_in[l]), wa_p[l], b_gate[l].reshape(1, GLA_KWIDTH))
        att = _attn_prompt(z, attn_sinks[l], batch, nb, m_pad)
        go, st_p = _gla_prompt(z, la, batch, lp, m_pad)

        zs = z[mp:mp + ms].reshape(dec_batch, dec_seq, Z_WIDTH)
        zs8 = jnp.pad(zs, ((0, 0), (0, rows8 - dec_seq), (0, 0)))
        la8 = jnp.pad(la[mp:mp + ms].reshape(dec_batch, dec_seq, GLA_KWIDTH), ((0, 0), (0, rows8 - dec_seq), (0, 0)))
        qa_s = zs[:, :, COL_QA:COL_QA + ATT_WIDTH].reshape(dec_batch, dec_seq, ATT_KV_HEADS, ATT_GROUP, ATT_HEAD_DIM)
        qa_s = qa_s.transpose(0, 2, 1, 3, 4).reshape(dec_batch, ATT_KV_HEADS, dec_seq * ATT_GROUP, ATT_HEAD_DIM)
        sink_rows = jnp.tile(attn_sinks[l].reshape(ATT_KV_HEADS, 1, ATT_GROUP), (1, dec_seq, 1))
        sink_rows = sink_rows.reshape(ATT_KV_HEADS, dec_seq * ATT_GROUP, 1)
        kbuf = cache_k_win[l].reshape(dec_batch, wb, KV_WIDTH)
        vbuf = cache_v_win[l].reshape(dec_batch, wb, KV_WIDTH)
        att_s, go_s, st_s = _sample_mixer(qa_s, zs8, la8, kbuf, vbuf, sink_rows, state_gla[l], dec_seq)
        att_s = att_s.reshape(dec_batch, ATT_KV_HEADS, dec_seq, ATT_GROUP, ATT_HEAD_DIM)
        att_s = att_s.transpose(0, 2, 1, 3, 4).reshape(ms, ATT_WIDTH)

        att = lax.dynamic_update_slice(att, jnp.concatenate([att_s, tail], axis=0).astype(BF16), (mp, 0))
        go = lax.dynamic_update_slice(go, jnp.concatenate([go_s[:, :dec_seq].reshape(ms, GLA_WIDTH), tail], axis=0), (mp, 0))
        x = _merge(x, att, go, z, gla_norm[l].reshape(1, GLA_DV), w_out[l].astype(BF16), g[3])

        win = [z[b * lp + lp - WINDOW:(b + 1) * lp, COL_KA:COL_VA + KV_WIDTH] for b in range(batch)]
        win = jnp.stack(win).reshape(batch, WINDOW, 2, ATT_KV_HEADS, ATT_HEAD_DIM)
        pk.append(win[:, :, 0])
        pv.append(win[:, :, 1])
        ps.append(st_p.transpose(0, 1, 3, 2))
        kn = zs[:, :, COL_KA:COL_KA + KV_WIDTH].reshape(dec_batch, dec_seq, ATT_KV_HEADS, ATT_HEAD_DIM)
        vn = zs[:, :, COL_VA:COL_VA + KV_WIDTH].reshape(dec_batch, dec_seq, ATT_KV_HEADS, ATT_HEAD_DIM)
        sk.append(jnp.concatenate([cache_k_win[l], kn], axis=1)[:, -wb:])
        sv.append(jnp.concatenate([cache_v_win[l], vn], axis=1)[:, -wb:])
        ss.append(st_s)

        x = ffn(x, l, 1)

    y_prompt = x[:mp].reshape(batch, lp, D_MODEL)[:, BLOCK:]
    y_sample = x[mp:mp + ms].reshape(dec_batch, dec_seq, D_MODEL)
    return (y_prompt, y_sample, jnp.stack(pk), jnp.stack(pv), jnp.stack(ps),
            jnp.stack(sk), jnp.stack(sv), jnp.stack(ss))
```

```python
import functools

import jax
import jax.numpy as jnp
from jax import lax
from jax.experimental import pallas as pl
from jax.experimental.pallas import tpu as pltpu

F32 = jnp.float32
BF16 = jnp.bfloat16

D_MODEL = 2048
D_FF = 5632
N_META = 16
BLOCK = 128
WINDOW = 128
ATT_HEADS = 16
ATT_KV_HEADS = 2
ATT_GROUP = 8
ATT_HEAD_DIM = 64
ATT_WIDTH = 1024
KV_WIDTH = 128
GLA_HEADS = 4
GLA_DK = 128
GLA_DV = 256
GLA_KWIDTH = 512
GLA_WIDTH = 1024
GLA_RANK = 16
GLA_GATE_NORM = 16.0
GLA_CHUNK = 64
GLA_SUB = 16
EPS = 1e-6
PAD = BLOCK - N_META

COL_QA, COL_KA, COL_VA, COL_QG, COL_KG, COL_VG, COL_RG, COL_LR = 0, 1024, 1152, 1280, 1792, 2304, 3328, 4352
COL_TILE = 256
Z_WIDTH = 4608
LR_PAD = 128

VMEM_LIMIT = 56 * 1024 * 1024

TM = 1072
TM_MERGE = 536
TF = 512
TN = 768


def _rms(x, gain):
    ms = jnp.mean(x * x, axis=-1, keepdims=True)
    return x * lax.rsqrt(ms + EPS) * gain


def _dot(a, b):
    return jnp.dot(a, b, preferred_element_type=F32)


def _dot_nt(a, b):
    return lax.dot_general(a, b, (((1,), (1,)), ((), ())), preferred_element_type=F32)


def _dot_tn(a, b):
    return lax.dot_general(a, b, (((0,), (0,)), ((), ())), preferred_element_type=F32)


def _ffn_kernel(x_ref, gpre_ref, gpost_ref, wg_ref, wu_ref, wd_ref, o_ref, xn_ref):
    j = pl.program_id(1)

    @pl.when(j == 0)
    def _():
        xn_ref[...] = _rms(x_ref[...], gpre_ref[...]).astype(BF16)
        o_ref[...] = jnp.zeros_like(o_ref)

    xn = xn_ref[...]
    g = _dot(xn, wg_ref[...])
    u = _dot(xn, wu_ref[...])
    h = (g * jax.nn.sigmoid(g) * u).astype(BF16)
    o_ref[...] += _dot(h, wd_ref[...])

    @pl.when(j == pl.num_programs(1) - 1)
    def _():
        o_ref[...] = x_ref[...] + 0.5 * _rms(o_ref[...], gpost_ref[...])


def _gain_spec(l, idx):
    return pl.BlockSpec((None, None, 1, D_MODEL), lambda *_: (l, idx, 0, 0))


def _ffn(x, gains, wg, wu, wd, l, f):
    m = x.shape[0]
    return pl.pallas_call(
        _ffn_kernel,
        grid=(m // TM, D_FF // TF),
        in_specs=[
            pl.BlockSpec((TM, D_MODEL), lambda i, j: (i, 0), pipeline_mode=pl.Buffered(1)),
            _gain_spec(l, 4 * f),
            _gain_spec(l, 4 * f + 1),
            pl.BlockSpec((None, None, D_MODEL, TF), lambda i, j: (l, f, 0, j)),
            pl.BlockSpec((None, None, D_MODEL, TF), lambda i, j: (l, f, 0, j)),
            pl.BlockSpec((None, None, TF, D_MODEL), lambda i, j: (l, f, j, 0)),
        ],
        out_specs=pl.BlockSpec((TM, D_MODEL), lambda i, j: (i, 0)),
        out_shape=jax.ShapeDtypeStruct((m, D_MODEL), F32),
        scratch_shapes=[pltpu.VMEM((TM, D_MODEL), BF16)],
        compiler_params=pltpu.CompilerParams(
            dimension_semantics=("parallel", "arbitrary"), vmem_limit_bytes=VMEM_LIMIT),
        name="ffn",
    )(x, gains, gains, wg, wu, wd)


def _proj_kernel(x_ref, g_ref, w_ref, wa_ref, ba_ref, z_ref, la_ref, xn_ref):
    j = pl.program_id(1)

    @pl.when(j == 0)
    def _():
        xn_ref[...] = _rms(x_ref[...], g_ref[...]).astype(BF16)

    z = _dot(xn_ref[...], w_ref[...])
    z_ref[...] = z

    @pl.when(j == pl.num_programs(1) - 1)
    def _():
        lr_off = COL_LR - (Z_WIDTH - TN)
        lr = z[:, lr_off:lr_off + LR_PAD].astype(BF16)
        logit = _dot(lr, wa_ref[...]) + ba_ref[...]
        log_sig = jnp.minimum(logit, 0.0) - jnp.log1p(jnp.exp(-jnp.abs(logit)))
        la_ref[...] = log_sig * (1.0 / GLA_GATE_NORM)


def _proj(x, gains, w_in_p, wa_p, ba, l):
    m = x.shape[0]
    return pl.pallas_call(
        _proj_kernel,
        grid=(m // TM, Z_WIDTH // TN),
        in_specs=[
            pl.BlockSpec((TM, D_MODEL), lambda i, j: (i, 0)),
            _gain_spec(l, 2),
            pl.BlockSpec((None, D_MODEL, TN), lambda i, j: (l, 0, j)),
            pl.BlockSpec((None, LR_PAD, GLA_KWIDTH), lambda i, j: (l, 0, 0)),
            pl.BlockSpec((None, 1, GLA_KWIDTH), lambda i, j: (l, 0, 0)),
        ],
        out_specs=[
            pl.BlockSpec((TM, TN), lambda i, j: (i, j)),
            pl.BlockSpec((TM, GLA_KWIDTH), lambda i, j: (i, 0)),
        ],
        out_shape=[
            jax.ShapeDtypeStruct((m, Z_WIDTH), F32),
            jax.ShapeDtypeStruct((m, GLA_KWIDTH), F32),
        ],
        scratch_shapes=[pltpu.VMEM((TM, D_MODEL), BF16)],
        compiler_params=pltpu.CompilerParams(
            dimension_semantics=("parallel", "arbitrary"), vmem_limit_bytes=VMEM_LIMIT),
        name="proj",
    )(x, gains, w_in_p, wa_p, ba)


def _pair_blockdiag(x128, kv):
    lane = lax.broadcasted_iota(jnp.int32, x128.shape, 1)
    own = jnp.where((lane >= kv * ATT_HEAD_DIM) & (lane < (kv + 1) * ATT_HEAD_DIM), x128, 0.0)
    other = pltpu.roll(own, ATT_HEAD_DIM, axis=1)
    lo, hi = (own, other) if kv == 0 else (other, own)
    return jnp.concatenate([lo, hi], axis=0).astype(BF16)


def _attn_kernel(sink_ref, q_ref, kc_ref, kp_ref, vc_ref, vp_ref, o_ref):
    i = pl.program_id(1)
    kb = 2 * BLOCK
    row = lax.broadcasted_iota(jnp.int32, (BLOCK, kb), 0)
    col = lax.broadcasted_iota(jnp.int32, (BLOCK, kb), 1)
    diff = row + BLOCK - col
    key_pos = col + (i - 1) * BLOCK
    mask = (diff >= 0) & (diff <= WINDOW) & (key_pos >= PAD)
    lane = lax.broadcasted_iota(jnp.int32, (BLOCK, 2 * ATT_HEAD_DIM), 1)
    kk = jnp.concatenate([kp_ref[...], kc_ref[...]], axis=0)
    vv = jnp.concatenate([vp_ref[...], vc_ref[...]], axis=0)
    slabs = ATT_GROUP // 2
    for kv in range(ATT_KV_HEADS):
        k2 = _pair_blockdiag(kk, kv)
        v2 = _pair_blockdiag(vv, kv)
        c0 = kv * slabs * 2 * ATT_HEAD_DIM
        q4 = q_ref[:, c0:c0 + slabs * 2 * ATT_HEAD_DIM]
        q4 = jnp.concatenate([q4[:, p * 128:(p + 1) * 128] for p in range(slabs)], axis=0).astype(BF16)
        s = _dot_nt(q4, k2)
        probs, inv = [], []
        for p in range(slabs):
            halves, rden = [], []
            for e in range(2):
                sink = sink_ref[kv * ATT_GROUP + 2 * p + e]
                sp = jnp.where(mask, s[p * BLOCK:(p + 1) * BLOCK, e * kb:(e + 1) * kb], -jnp.inf)
                mx = jnp.maximum(jnp.max(sp, axis=-1, keepdims=True), sink)
                pe = jnp.exp(sp - mx)
                rden.append(1.0 / (jnp.sum(pe, axis=-1, keepdims=True) + jnp.exp(sink - mx)))
                halves.append(pe.astype(BF16))
            probs.append(jnp.concatenate(halves, axis=1))
            inv.append(jnp.where(lane < ATT_HEAD_DIM, rden[0], rden[1]))
        o = _dot(jnp.concatenate(probs, axis=0), v2)
        for p in range(slabs):
            o_ref[:, c0 + p * 128:c0 + (p + 1) * 128] = (o[p * BLOCK:(p + 1) * BLOCK] * inv[p]).astype(o_ref.dtype)


def _attn_prompt(z, sinks, batch, nb, m_rows):
    def cur(c):
        return lambda b, i, sink: (b * nb + i, c)

    def prev(c):
        return lambda b, i, sink: (b * nb + jnp.maximum(i - 1, 0), c)

    return pl.pallas_call(
        _attn_kernel,
        grid_spec=pltpu.PrefetchScalarGridSpec(
            num_scalar_prefetch=1,
            grid=(batch, nb),
            in_specs=[
                pl.BlockSpec((BLOCK, ATT_WIDTH), cur(COL_QA // ATT_WIDTH)),
                pl.BlockSpec((BLOCK, KV_WIDTH), cur(COL_KA // KV_WIDTH)),
                pl.BlockSpec((BLOCK, KV_WIDTH), prev(COL_KA // KV_WIDTH)),
                pl.BlockSpec((BLOCK, KV_WIDTH), cur(COL_VA // KV_WIDTH)),
                pl.BlockSpec((BLOCK, KV_WIDTH), prev(COL_VA // KV_WIDTH)),
            ],
            out_specs=pl.BlockSpec((BLOCK, ATT_WIDTH), lambda b, i, sink: (b * nb + i, 0)),
        ),
        out_shape=jax.ShapeDtypeStruct((m_rows, ATT_WIDTH), BF16),
        compiler_params=pltpu.CompilerParams(
            dimension_semantics=("parallel", "arbitrary"), vmem_limit_bytes=VMEM_LIMIT),
        name="attn_prompt",
    )(sinks, z, z, z, z, z)


def _cumsum_rows(la, tri):
    hi = la.astype(BF16)
    r1 = la - hi.astype(F32)
    mid = r1.astype(BF16)
    lo = (r1 - mid.astype(F32)).astype(BF16)
    return _dot(tri, hi) + _dot(tri, mid) + _dot(tri, lo)


def _gla_kernel(*refs):
    nq = GLA_KWIDTH // COL_TILE
    q_refs, k_refs, v_refs = refs[:nq], refs[nq:2 * nq], refs[2 * nq:2 * nq + GLA_HEADS]
    la_ref, o_ref, st_ref, state_ref = refs[2 * nq + GLA_HEADS:]
    per = COL_TILE // GLA_DK
    c = pl.program_id(1)
    C = GLA_CHUNK

    @pl.when(c == 0)
    def _():
        state_ref[...] = jnp.zeros_like(state_ref)

    rowc = lax.broadcasted_iota(jnp.int32, (C, C), 0)
    colc = lax.broadcasted_iota(jnp.int32, (C, C), 1)
    tri = (rowc >= colc).astype(BF16)
    b_all = _cumsum_rows(la_ref[...], tri)

    pos = c * C + lax.broadcasted_iota(jnp.int32, (C, 1), 0)
    valid = (pos >= PAD).astype(F32)

    lane = lax.broadcasted_iota(jnp.int32, (GLA_SUB, C), 1)
    rsub = lax.broadcasted_iota(jnp.int32, (GLA_SUB, C), 0)
    scale = GLA_DK ** -0.5

    for h in range(GLA_HEADS):
        ks = slice(h * GLA_DK, (h + 1) * GLA_DK)
        vs = slice(h * GLA_DV, (h + 1) * GLA_DV)
        sub = slice((h % per) * GLA_DK, (h % per + 1) * GLA_DK)
        q = q_refs[h // per][:, sub] * scale
        k = k_refs[h // per][:, sub] * valid
        v = v_refs[h][...].astype(BF16)
        bh = b_all[:, ks]
        bl = bh[C - 1:C, :]
        st = state_ref[h]

        o = _dot_nt((q * jnp.exp(bh)).astype(BF16), st.astype(BF16))

        blocks = []
        for i in range(C // GLA_SUB):
            lo_r = i * GLA_SUB
            qb = q[lo_r:lo_r + GLA_SUB]
            bb = bh[lo_r:lo_r + GLA_SUB]
            w = jnp.zeros((GLA_SUB, C), F32)
            for s in range(GLA_SUB):
                r = lo_r + s
                zed = qb * k[r:r + 1] * jnp.exp(jnp.minimum(bb - bh[r:r + 1], 0.0))
                w = jnp.where(lane == r, jnp.sum(zed, axis=-1, keepdims=True), w)
            w = jnp.where(lane <= rsub + lo_r, w, 0.0)
            if i > 0:
                ref_b = bh[lo_r - 1:lo_r]
                qi = (qb * jnp.exp(bb - ref_b)).astype(BF16)
                kj = (k * jnp.exp(jnp.minimum(ref_b - bh, 0.0))).astype(BF16)
                w = jnp.where(lane < lo_r, _dot_nt(qi, kj), w)
            blocks.append(w)
        a = jnp.concatenate(blocks, axis=0).astype(BF16)
        o_ref[:, vs] = o + _dot(a, v)

        k_out = (k * jnp.exp(bl - bh)).astype(BF16)
        state_ref[h] = st * jnp.exp(bl) + _dot_tn(v, k_out)

    @pl.when(c == pl.num_programs(1) - 1)
    def _():
        st_ref[...] = state_ref[...]


def _gla_prompt(z, la, batch, seq, m_rows):
    nc = seq // GLA_CHUNK
    nq = GLA_KWIDTH // COL_TILE

    def rows(cidx):
        return lambda b, c: (b * nc + c, cidx)

    return pl.pallas_call(
        _gla_kernel,
        grid=(batch, nc),
        in_specs=(
            [pl.BlockSpec((GLA_CHUNK, COL_TILE), rows(COL_QG // COL_TILE + t)) for t in range(nq)]
            + [pl.BlockSpec((GLA_CHUNK, COL_TILE), rows(COL_KG // COL_TILE + t)) for t in range(nq)]
            + [pl.BlockSpec((GLA_CHUNK, GLA_DV), rows(COL_VG // GLA_DV + t)) for t in range(GLA_HEADS)]
            + [pl.BlockSpec((GLA_CHUNK, GLA_KWIDTH), rows(0))]
        ),
        out_specs=[
            pl.BlockSpec((GLA_CHUNK, GLA_WIDTH), rows(0)),
            pl.BlockSpec((None, GLA_HEADS, GLA_DV, GLA_DK), lambda b, c: (b, 0, 0, 0)),
        ],
        out_shape=[
            jax.ShapeDtypeStruct((m_rows, GLA_WIDTH), F32),
            jax.ShapeDtypeStruct((batch, GLA_HEADS, GLA_DV, GLA_DK), F32),
        ],
        scratch_shapes=[pltpu.VMEM((GLA_HEADS, GLA_DV, GLA_DK), F32)],
        compiler_params=pltpu.CompilerParams(
            dimension_semantics=("parallel", "arbitrary"), vmem_limit_bytes=VMEM_LIMIT),
        name="gla_prompt",
    )(*([z] * (2 * nq + GLA_HEADS)), la)


def _sample_kernel(qa_ref, zs_ref, la_ref, kbuf_ref, vbuf_ref, sink_ref, s0_ref,
                   att_ref, go_ref, st_ref, *, dec_seq):
    S = dec_seq
    R = S * ATT_GROUP
    t_row = lax.broadcasted_iota(jnp.int32, (R, 1), 0) // ATT_GROUP
    c_idx = lax.broadcasted_iota(jnp.int32, (R, kbuf_ref.shape[0]), 1)
    mask_c = c_idx >= t_row
    for kv in range(ATT_KV_HEADS):
        ks = slice(kv * ATT_HEAD_DIM, (kv + 1) * ATT_HEAD_DIM)
        q = qa_ref[kv]
        sink = sink_ref[kv]
        sc = _dot_nt(q.astype(BF16), kbuf_ref[:, ks].astype(BF16))
        sc = jnp.where(mask_c, sc, -jnp.inf)
        mx = jnp.maximum(jnp.max(sc, axis=-1, keepdims=True), sink)
        kn = zs_ref[:, COL_KA + kv * ATT_HEAD_DIM:COL_KA + (kv + 1) * ATT_HEAD_DIM]
        vn = zs_ref[:, COL_VA + kv * ATT_HEAD_DIM:COL_VA + (kv + 1) * ATT_HEAD_DIM]
        sn = []
        for s in range(S):
            v = jnp.sum(q * kn[s:s + 1], axis=-1, keepdims=True)
            v = jnp.where(t_row >= s, v, -jnp.inf)
            sn.append(v)
            mx = jnp.maximum(mx, v)
        pc = jnp.exp(sc - mx)
        den = jnp.sum(pc, axis=-1, keepdims=True) + jnp.exp(sink - mx)
        o = _dot(pc.astype(BF16), vbuf_ref[:, ks].astype(BF16))
        for s in range(S):
            pn = jnp.exp(sn[s] - mx)
            den = den + pn
            o = o + pn * vn[s:s + 1]
        att_ref[kv] = o / den

    rows8 = zs_ref.shape[0]
    rid = lax.broadcasted_iota(jnp.int32, (rows8, 1), 0)
    la = la_ref[...]
    run = la[0:1]
    b_all = jnp.where(rid == 0, run, 0.0)
    for t in range(1, S):
        run = run + la[t:t + 1]
        b_all = jnp.where(rid == t, run, b_all)
    real = (rid < S).astype(F32)
    scale = GLA_DK ** -0.5
    zpad = jnp.zeros((GLA_DK - 3 * rows8, GLA_DK), F32)
    for h in range(GLA_HEADS):
        ks = slice(h * GLA_DK, (h + 1) * GLA_DK)
        q = zs_ref[:, COL_QG + h * GLA_DK:COL_QG + (h + 1) * GLA_DK] * scale
        k = zs_ref[:, COL_KG + h * GLA_DK:COL_KG + (h + 1) * GLA_DK]
        v = zs_ref[:, COL_VG + h * GLA_DV:COL_VG + (h + 1) * GLA_DV]
        bh = b_all[:, ks] * real
        bl = bh[S - 1:S]
        s0 = s0_ref[h]
        e_l = jnp.broadcast_to(jnp.exp(bl), (rows8, GLA_DK))
        k_out = k * jnp.exp(jnp.minimum(bl - bh, 0.0)) * real
        q_in = q * jnp.exp(bh) * real
        cols = jnp.concatenate([e_l, k_out, q_in, zpad], axis=0).T
        o = jnp.zeros((rows8, GLA_DV), F32)
        for t in range(S):
            o_t = jnp.sum(cols[:, 2 * rows8 + t:2 * rows8 + t + 1] * s0, axis=0, keepdims=True)
            o = jnp.where(rid == t, o_t, o)
        for s in range(S):
            zed = q * k[s:s + 1] * jnp.exp(jnp.minimum(bh - bh[s:s + 1], 0.0))
            w = jnp.sum(zed, axis=-1, keepdims=True)
            w = jnp.where((rid >= s) & (rid < S), w, 0.0)
            o = o + w * v[s:s + 1]
        go_ref[:, h * GLA_DV:(h + 1) * GLA_DV] = o
        st = cols[:, 0:1] * s0
        for t in range(S):
            st = st + cols[:, rows8 + t:rows8 + t + 1] * v[t:t + 1]
        st_ref[h] = st


def _sample_mixer(qa_s, zs8, la8, kbuf, vbuf, sink_rows, s0, dec_seq, l):
    nb = zs8.shape[0]
    rows8 = zs8.shape[1]
    R = dec_seq * ATT_GROUP
    wb = kbuf.shape[2]
    return pl.pallas_call(
        functools.partial(_sample_kernel, dec_seq=dec_seq),
        grid=(nb,),
        in_specs=[
            pl.BlockSpec((None, ATT_KV_HEADS, R, ATT_HEAD_DIM), lambda b: (b, 0, 0, 0)),
            pl.BlockSpec((None, rows8, Z_WIDTH), lambda b: (b, 0, 0)),
            pl.BlockSpec((None, rows8, GLA_KWIDTH), lambda b: (b, 0, 0)),
            pl.BlockSpec((None, None, wb, KV_WIDTH), lambda b: (l, b, 0, 0)),
            pl.BlockSpec((None, None, wb, KV_WIDTH), lambda b: (l, b, 0, 0)),
            pl.BlockSpec((ATT_KV_HEADS, R, 1), lambda b: (0, 0, 0)),
            pl.BlockSpec((None, None, GLA_HEADS, GLA_DK, GLA_DV), lambda b: (l, b, 0, 0, 0)),
        ],
        out_specs=[
            pl.BlockSpec((None, ATT_KV_HEADS, R, ATT_HEAD_DIM), lambda b: (b, 0, 0, 0)),
            pl.BlockSpec((None, rows8, GLA_WIDTH), lambda b: (b, 0, 0)),
            pl.BlockSpec((None, GLA_HEADS, GLA_DK, GLA_DV), lambda b: (b, 0, 0, 0)),
        ],
        out_shape=[
            jax.ShapeDtypeStruct((nb, ATT_KV_HEADS, R, ATT_HEAD_DIM), F32),
            jax.ShapeDtypeStruct((nb, rows8, GLA_WIDTH), F32),
            jax.ShapeDtypeStruct((nb, GLA_HEADS, GLA_DK, GLA_DV), F32),
        ],
        compiler_params=pltpu.CompilerParams(
            dimension_semantics=("parallel",), vmem_limit_bytes=VMEM_LIMIT),
        name="sample_mixer",
    )(qa_s, zs8, la8, kbuf, vbuf, sink_rows, s0)


def _merge_kernel(*refs):
    x_ref, att_ref, go_ref = refs[:3]
    rg_refs = refs[3:3 + GLA_HEADS]
    ggla_ref, wo_ref, gpost_ref, o_ref = refs[3 + GLA_HEADS:]
    parts = []
    for h in range(GLA_HEADS):
        vs = slice(h * GLA_DV, (h + 1) * GLA_DV)
        rg = rg_refs[h][...]
        parts.append((_rms(go_ref[:, vs], ggla_ref[...]) * (rg * jax.nn.sigmoid(rg))).astype(BF16))
    go = jnp.concatenate(parts, axis=-1)
    y = _dot(att_ref[...], wo_ref[:ATT_WIDTH, :]) + _dot(go, wo_ref[ATT_WIDTH:, :])
    o_ref[...] = x_ref[...] + _rms(y, gpost_ref[...])


def _merge(x, att, go, z, g_gla, w_out, gains, l):
    m = x.shape[0]
    return pl.pallas_call(
        _merge_kernel,
        grid=(m // TM_MERGE,),
        in_specs=(
            [pl.BlockSpec((TM_MERGE, D_MODEL), lambda i: (i, 0)),
             pl.BlockSpec((TM_MERGE, ATT_WIDTH), lambda i: (i, 0)),
             pl.BlockSpec((TM_MERGE, GLA_WIDTH), lambda i: (i, 0))]
            + [pl.BlockSpec((TM_MERGE, GLA_DV), lambda i, h=h: (i, COL_RG // GLA_DV + h)) for h in range(GLA_HEADS)]
            + [pl.BlockSpec((None, 1, GLA_DV), lambda i: (l, 0, 0)),
               pl.BlockSpec((None, ATT_WIDTH + GLA_WIDTH, D_MODEL), lambda i: (l, 0, 0)),
               _gain_spec(l, 3)]
        ),
        out_specs=pl.BlockSpec((TM_MERGE, D_MODEL), lambda i: (i, 0)),
        out_shape=jax.ShapeDtypeStruct((m, D_MODEL), F32),
        compiler_params=pltpu.CompilerParams(
            dimension_semantics=("parallel",), vmem_limit_bytes=VMEM_LIMIT),
        name="merge",
    )(x, att, go, *([z] * GLA_HEADS), g_gla, w_out, gains)


def _prep_w_in(w):
    col = jnp.arange(Z_WIDTH)
    colscale = jnp.where((col >= COL_QA) & (col < COL_QA + ATT_WIDTH), ATT_HEAD_DIM ** -0.5, 1.0).astype(w.dtype)
    w = jnp.pad(w, ((0, 0), (0, 0), (0, Z_WIDTH - w.shape[-1])))
    return (w * colscale).astype(BF16)


def kernel(x_prompt, x_sample, cache_k_win, cache_v_win, state_gla, meta_tokens, norm_gains,
           w_ffn_gate, w_ffn_up, w_ffn_down, w_in, w_gate_up, b_gate, attn_sinks, gla_norm, w_out):
    batch, seq, _ = x_prompt.shape
    dec_batch, dec_seq, _ = x_sample.shape
    depth = norm_gains.shape[0]
    lp = seq + BLOCK
    nb = lp // BLOCK
    mp = batch * lp
    ms = dec_batch * dec_seq
    m_pad = -(-(mp + ms) // TM) * TM
    wb = cache_k_win.shape[2]
    rows8 = 8
    assert dec_seq <= rows8 and lp % GLA_CHUNK == 0

    head = jnp.concatenate([jnp.zeros((PAD, D_MODEL), F32), meta_tokens.astype(F32)], axis=0)
    pieces = []
    for b in range(batch):
        pieces += [head, x_prompt[b]]
    pieces += [x_sample.reshape(ms, D_MODEL), jnp.zeros((m_pad - mp - ms, D_MODEL), F32)]
    x = jnp.concatenate(pieces, axis=0)

    wg = w_ffn_gate.astype(BF16)
    wu = w_ffn_up.astype(BF16)
    wd = w_ffn_down.astype(BF16)
    wo = w_out.astype(BF16)
    w_in_p = _prep_w_in(w_in)
    wa_p = jnp.pad(w_gate_up, ((0, 0), (0, LR_PAD - GLA_RANK), (0, 0))).astype(BF16)
    ba = b_gate.reshape(depth, 1, GLA_KWIDTH)
    g_gla = gla_norm.reshape(depth, 1, GLA_DV)
    gains = norm_gains.reshape(depth, 6, 1, D_MODEL)
    kbuf = cache_k_win.reshape(depth, dec_batch, wb, KV_WIDTH)
    vbuf = cache_v_win.reshape(depth, dec_batch, wb, KV_WIDTH)
    tail = jnp.zeros((m_pad - mp - ms, GLA_WIDTH), F32)

    pk, pv, ps, sk, sv, ss = [], [], [], [], [], []
    for l in range(depth):
        x = _ffn(x, gains, wg, wu, wd, l, 0)

        z, la = _proj(x, gains, w_in_p, wa_p, ba, l)
        att = _attn_prompt(z, attn_sinks[l], batch, nb, m_pad)
        go, st_p = _gla_prompt(z, la, batch, lp, m_pad)

        zs = z[mp:mp + ms].reshape(dec_batch, dec_seq, Z_WIDTH)
        zs8 = jnp.pad(zs, ((0, 0), (0, rows8 - dec_seq), (0, 0)))
        la8 = jnp.pad(la[mp:mp + ms].reshape(dec_batch, dec_seq, GLA_KWIDTH), ((0, 0), (0, rows8 - dec_seq), (0, 0)))
        qa_s = zs[:, :, COL_QA:COL_QA + ATT_WIDTH].reshape(dec_batch, dec_seq, ATT_KV_HEADS, ATT_GROUP, ATT_HEAD_DIM)
        qa_s = qa_s.transpose(0, 2, 1, 3, 4).reshape(dec_batch, ATT_KV_HEADS, dec_seq * ATT_GROUP, ATT_HEAD_DIM)
        sink_rows = jnp.tile(attn_sinks[l].reshape(ATT_KV_HEADS, 1, ATT_GROUP), (1, dec_seq, 1))
        sink_rows = sink_rows.reshape(ATT_KV_HEADS, dec_seq * ATT_GROUP, 1)
        att_s, go_s, st_s = _sample_mixer(qa_s, zs8, la8, kbuf, vbuf, sink_rows, state_gla, dec_seq, l)
        att_s = att_s.reshape(dec_batch, ATT_KV_HEADS, dec_seq, ATT_GROUP, ATT_HEAD_DIM)
        att_s = att_s.transpose(0, 2, 1, 3, 4).reshape(ms, ATT_WIDTH)

        att = lax.dynamic_update_slice(att, jnp.concatenate([att_s, tail], axis=0).astype(BF16), (mp, 0))
        go = lax.dynamic_update_slice(go, jnp.concatenate([go_s[:, :dec_seq].reshape(ms, GLA_WIDTH), tail], axis=0), (mp, 0))
        x = _merge(x, att, go, z, g_gla, wo, gains, l)

        win = [z[b * lp + lp - WINDOW:(b + 1) * lp, COL_KA:COL_VA + KV_WIDTH] for b in range(batch)]
        win = jnp.stack(win).reshape(batch, WINDOW, 2, ATT_KV_HEADS, ATT_HEAD_DIM)
        pk.append(win[:, :, 0])
        pv.append(win[:, :, 1])
        ps.append(st_p.transpose(0, 1, 3, 2))
        kn = zs[:, :, COL_KA:COL_KA + KV_WIDTH].reshape(dec_batch, dec_seq, ATT_KV_HEADS, ATT_HEAD_DIM)
        vn = zs[:, :, COL_VA:COL_VA + KV_WIDTH].reshape(dec_batch, dec_seq, ATT_KV_HEADS, ATT_HEAD_DIM)
        sk.append(jnp.concatenate([cache_k_win[l], kn], axis=1)[:, -wb:])
        sv.append(jnp.concatenate([cache_v_win[l], vn], axis=1)[:, -wb:])
        ss.append(st_s)

        x = _ffn(x, gains, wg, wu, wd, l, 1)

    y_prompt = jnp.stack([x[b * lp + BLOCK:(b + 1) * lp] for b in range(batch)])
    y_sample = x[mp:mp + ms].reshape(dec_batch, dec_seq, D_MODEL)
    return (y_prompt, y_sample, jnp.stack(pk), jnp.stack(pv), jnp.stack(ps),
            jnp.stack(sk), jnp.stack(sv), jnp.stack(ss))
```

```python
import functools

import jax
import jax.numpy as jnp
from jax import lax
from jax.experimental import pallas as pl
from jax.experimental.pallas import tpu as pltpu

F32 = jnp.float32
BF16 = jnp.bfloat16

D_MODEL = 2048
D_FF = 5632
N_META = 16
BLOCK = 128
WINDOW = 128
ATT_HEADS = 16
ATT_KV_HEADS = 2
ATT_GROUP = 8
ATT_HEAD_DIM = 64
ATT_WIDTH = 1024
KV_WIDTH = 128
GLA_HEADS = 4
GLA_DK = 128
GLA_DV = 256
GLA_KWIDTH = 512
GLA_WIDTH = 1024
GLA_RANK = 16
GLA_GATE_NORM = 16.0
GLA_CHUNK = 64
GLA_SUB = 16
EPS = 1e-6
LOG2_E = 1.4426950408889634
PAD = BLOCK - N_META

COL_QA, COL_KA, COL_VA, COL_QG, COL_KG, COL_VG, COL_RG, COL_LR = 0, 1024, 1152, 1280, 1792, 2304, 3328, 4352
COL_TILE = 256
Z_WIDTH = 4608
LR_PAD = 128

VMEM_LIMIT = 56 * 1024 * 1024
VMEM_LIMIT_FFN = 60 * 1024 * 1024

TM = 1072
TM_MERGE = 536
TF = 256
FF_CHUNK = 256
TN = 768


def _rms(x, gain):
    ms = jnp.mean(x * x, axis=-1, keepdims=True)
    return x * lax.rsqrt(ms + EPS) * gain


def _dot(a, b):
    return jnp.dot(a, b, preferred_element_type=F32)


def _dot_nt(a, b):
    return lax.dot_general(a, b, (((1,), (1,)), ((), ())), preferred_element_type=F32)


def _dot_tn(a, b):
    return lax.dot_general(a, b, (((0,), (0,)), ((), ())), preferred_element_type=F32)


def _ffn_kernel(x_ref, gpre_ref, gpost_ref, wg_ref, wu_ref, wd_ref, o_ref, xn_ref):
    j = pl.program_id(1)

    @pl.when(j == 0)
    def _():
        xn_ref[...] = _rms(x_ref[...], gpre_ref[...]).astype(BF16)
        o_ref[...] = jnp.zeros_like(o_ref)

    xn = xn_ref[...]
    for c in range(TF // FF_CHUNK):
        cs = slice(c * FF_CHUNK, (c + 1) * FF_CHUNK)
        g = _dot(xn, wg_ref[:, cs].astype(BF16))
        u = _dot(xn, wu_ref[:, cs].astype(BF16))
        h = (g * jax.nn.sigmoid(g) * u).astype(BF16)
        o_ref[...] += _dot(h, wd_ref[cs, :].astype(BF16))

    @pl.when(j == pl.num_programs(1) - 1)
    def _():
        o_ref[...] = x_ref[...] + 0.5 * _rms(o_ref[...], gpost_ref[...])


def _gain_spec(l, idx):
    return pl.BlockSpec((None, None, 1, D_MODEL), lambda *_: (l, idx, 0, 0))


def _ffn(x, gains, wg, wu, wd, l, f):
    m = x.shape[0]
    return pl.pallas_call(
        _ffn_kernel,
        grid=(m // TM, D_FF // TF),
        in_specs=[
            pl.BlockSpec((TM, D_MODEL), lambda i, j: (i, 0), pipeline_mode=pl.Buffered(1)),
            _gain_spec(l, 4 * f),
            _gain_spec(l, 4 * f + 1),
            pl.BlockSpec((None, None, D_MODEL, TF), lambda i, j: (l, f, 0, j)),
            pl.BlockSpec((None, None, D_MODEL, TF), lambda i, j: (l, f, 0, j)),
            pl.BlockSpec((None, None, TF, D_MODEL), lambda i, j: (l, f, j, 0)),
        ],
        out_specs=pl.BlockSpec((TM, D_MODEL), lambda i, j: (i, 0)),
        out_shape=jax.ShapeDtypeStruct((m, D_MODEL), F32),
        scratch_shapes=[pltpu.VMEM((TM, D_MODEL), BF16)],
        compiler_params=pltpu.CompilerParams(
            dimension_semantics=("parallel", "arbitrary"), vmem_limit_bytes=VMEM_LIMIT_FFN),
        name="ffn",
    )(x, gains, gains, wg, wu, wd)


def _proj_kernel(x_ref, g_ref, w_ref, wa_ref, ba_ref, z_ref, la_ref, xn_ref):
    j = pl.program_id(1)

    @pl.when(j == 0)
    def _():
        xn_ref[...] = _rms(x_ref[...], g_ref[...]).astype(BF16)

    z = _dot(xn_ref[...], w_ref[...])
    z_ref[...] = z

    @pl.when(j == pl.num_programs(1) - 1)
    def _():
        lr_off = COL_LR - (Z_WIDTH - TN)
        lr = z[:, lr_off:lr_off + LR_PAD].astype(BF16)
        logit = _dot(lr, wa_ref[...]) + ba_ref[...]
        log_sig = jnp.minimum(logit, 0.0) - jnp.log1p(jnp.exp(-jnp.abs(logit)))
        la_ref[...] = log_sig * (1.0 / GLA_GATE_NORM)


def _proj(x, gains, w_in_p, wa_p, ba, l):
    m = x.shape[0]
    return pl.pallas_call(
        _proj_kernel,
        grid=(m // TM, Z_WIDTH // TN),
        in_specs=[
            pl.BlockSpec((TM, D_MODEL), lambda i, j: (i, 0)),
            _gain_spec(l, 2),
            pl.BlockSpec((None, D_MODEL, TN), lambda i, j: (l, 0, j)),
            pl.BlockSpec((None, LR_PAD, GLA_KWIDTH), lambda i, j: (l, 0, 0)),
            pl.BlockSpec((None, 1, GLA_KWIDTH), lambda i, j: (l, 0, 0)),
        ],
        out_specs=[
            pl.BlockSpec((TM, TN), lambda i, j: (i, j)),
            pl.BlockSpec((TM, GLA_KWIDTH), lambda i, j: (i, 0)),
        ],
        out_shape=[
            jax.ShapeDtypeStruct((m, Z_WIDTH), F32),
            jax.ShapeDtypeStruct((m, GLA_KWIDTH), F32),
        ],
        scratch_shapes=[pltpu.VMEM((TM, D_MODEL), BF16)],
        compiler_params=pltpu.CompilerParams(
            dimension_semantics=("parallel", "arbitrary"), vmem_limit_bytes=VMEM_LIMIT),
        name="proj",
    )(x, gains, w_in_p, wa_p, ba)


def _pair_blockdiag(x128, kv):
    lane = lax.broadcasted_iota(jnp.int32, x128.shape, 1)
    own = jnp.where((lane >= kv * ATT_HEAD_DIM) & (lane < (kv + 1) * ATT_HEAD_DIM), x128, 0.0)
    other = pltpu.roll(own, ATT_HEAD_DIM, axis=1)
    lo, hi = (own, other) if kv == 0 else (other, own)
    return jnp.concatenate([lo, hi], axis=0).astype(BF16)


def _attn_block(i, sink_ref, q_ref, kc_ref, kp_ref, vc_ref, vp_ref, o_ref):
    kb = 2 * BLOCK
    row = lax.broadcasted_iota(jnp.int32, (BLOCK, kb), 0)
    col = lax.broadcasted_iota(jnp.int32, (BLOCK, kb), 1)
    diff = row + BLOCK - col
    key_pos = col + (i - 1) * BLOCK
    mask = (diff >= 0) & (diff <= WINDOW) & (key_pos >= PAD)
    lane = lax.broadcasted_iota(jnp.int32, (BLOCK, 2 * ATT_HEAD_DIM), 1)
    kk = jnp.concatenate([kp_ref[...], kc_ref[...]], axis=0)
    vv = jnp.concatenate([vp_ref[...], vc_ref[...]], axis=0)
    slabs = ATT_GROUP // 2
    for kv in range(ATT_KV_HEADS):
        k2 = _pair_blockdiag(kk, kv)
        v2 = _pair_blockdiag(vv, kv)
        c0 = kv * slabs * 2 * ATT_HEAD_DIM
        q4 = q_ref[:, c0:c0 + slabs * 2 * ATT_HEAD_DIM]
        q4 = jnp.concatenate([q4[:, p * 128:(p + 1) * 128] for p in range(slabs)], axis=0).astype(BF16)
        s = _dot_nt(q4, k2)
        probs, inv = [], []
        for p in range(slabs):
            halves, rden = [], []
            for e in range(2):
                sink = sink_ref[kv * ATT_GROUP + 2 * p + e]
                sp = jnp.where(mask, s[p * BLOCK:(p + 1) * BLOCK, e * kb:(e + 1) * kb], -jnp.inf)
                mx = jnp.maximum(jnp.max(sp, axis=-1, keepdims=True), sink)
                pe = jnp.exp(sp - mx)
                rden.append(1.0 / (jnp.sum(pe, axis=-1, keepdims=True) + jnp.exp(sink - mx)))
                halves.append(pe.astype(BF16))
            probs.append(jnp.concatenate(halves, axis=1))
            inv.append(jnp.where(lane < ATT_HEAD_DIM, rden[0], rden[1]))
        o = _dot(jnp.concatenate(probs, axis=0), v2)
        for p in range(slabs):
            o_ref[:, c0 + p * 128:c0 + (p + 1) * 128] = (o[p * BLOCK:(p + 1) * BLOCK] * inv[p]).astype(o_ref.dtype)


def _attn_kernel(sink_ref, q_ref, kc_ref, kp_ref, vc_ref, vp_ref, o_ref, *, nb, n_blocks):
    step = pl.program_id(0)

    @pl.when(step < n_blocks)
    def _():
        _attn_block(lax.rem(step, nb), sink_ref, q_ref, kc_ref, kp_ref, vc_ref, vp_ref, o_ref)

    @pl.when(step >= n_blocks)
    def _():
        o_ref[...] = jnp.zeros_like(o_ref)


def _attn_prompt(z, sinks, batch, nb, m_rows):
    n_blocks = batch * nb

    def cur(c):
        return lambda s, sink: (jnp.minimum(s, n_blocks - 1), c)

    def prev(c):
        return lambda s, sink: (jnp.maximum(jnp.minimum(s, n_blocks - 1) - 1, 0), c)

    return pl.pallas_call(
        functools.partial(_attn_kernel, nb=nb, n_blocks=n_blocks),
        grid_spec=pltpu.PrefetchScalarGridSpec(
            num_scalar_prefetch=1,
            grid=(m_rows // BLOCK,),
            in_specs=[
                pl.BlockSpec((BLOCK, ATT_WIDTH), cur(COL_QA // ATT_WIDTH)),
                pl.BlockSpec((BLOCK, KV_WIDTH), cur(COL_KA // KV_WIDTH)),
                pl.BlockSpec((BLOCK, KV_WIDTH), prev(COL_KA // KV_WIDTH)),
                pl.BlockSpec((BLOCK, KV_WIDTH), cur(COL_VA // KV_WIDTH)),
                pl.BlockSpec((BLOCK, KV_WIDTH), prev(COL_VA // KV_WIDTH)),
            ],
            out_specs=pl.BlockSpec((BLOCK, ATT_WIDTH), lambda s, sink: (s, 0)),
        ),
        out_shape=jax.ShapeDtypeStruct((m_rows, ATT_WIDTH), BF16),
        compiler_params=pltpu.CompilerParams(
            dimension_semantics=("arbitrary",), vmem_limit_bytes=VMEM_LIMIT),
        name="attn_prompt",
    )(sinks, z, z, z, z, z)


def _cumsum_rows(la, tri):
    hi = la.astype(BF16)
    r1 = la - hi.astype(F32)
    mid = r1.astype(BF16)
    lo = (r1 - mid.astype(F32)).astype(BF16)
    return _dot(tri, hi) + _dot(tri, mid) + _dot(tri, lo)


def _gla_chunk(c, nc, refs):
    nq = GLA_KWIDTH // COL_TILE
    q_refs, k_refs, v_refs = refs[:nq], refs[nq:2 * nq], refs[2 * nq:2 * nq + GLA_HEADS]
    la_ref, o_ref, st_ref, state_ref = refs[2 * nq + GLA_HEADS:]
    per = COL_TILE // GLA_DK
    C = GLA_CHUNK

    @pl.when(c == 0)
    def _():
        state_ref[...] = jnp.zeros_like(state_ref)

    rowc = lax.broadcasted_iota(jnp.int32, (C, C), 0)
    colc = lax.broadcasted_iota(jnp.int32, (C, C), 1)
    tri = (rowc >= colc).astype(BF16)
    b_all = _cumsum_rows(la_ref[...], tri) * LOG2_E

    pos = c * C + lax.broadcasted_iota(jnp.int32, (C, 1), 0)
    valid = (pos >= PAD).astype(F32)

    lane = lax.broadcasted_iota(jnp.int32, (GLA_SUB, C), 1)
    rsub = lax.broadcasted_iota(jnp.int32, (GLA_SUB, C), 0)
    scale = GLA_DK ** -0.5

    for h in range(GLA_HEADS):
        ks = slice(h * GLA_DK, (h + 1) * GLA_DK)
        vs = slice(h * GLA_DV, (h + 1) * GLA_DV)
        sub = slice((h % per) * GLA_DK, (h % per + 1) * GLA_DK)
        q = q_refs[h // per][:, sub] * scale
        k = k_refs[h // per][:, sub] * valid
        v = v_refs[h][...].astype(BF16)
        bh = b_all[:, ks]
        bl = bh[C - 1:C, :]
        st = state_ref[h]

        o = _dot_nt((q * jnp.exp2(bh)).astype(BF16), st.astype(BF16))

        blocks = []
        for i in range(C // GLA_SUB):
            lo_r = i * GLA_SUB
            qb = q[lo_r:lo_r + GLA_SUB]
            bb = bh[lo_r:lo_r + GLA_SUB]
            w = jnp.zeros((GLA_SUB, C), F32)
            for s in range(GLA_SUB):
                r = lo_r + s
                zed = qb * k[r:r + 1] * jnp.exp2(bb - bh[r:r + 1])
                w = jnp.where(lane == r, jnp.sum(zed, axis=-1, keepdims=True), w)
            w = jnp.where(lane <= rsub + lo_r, w, 0.0)
            if i > 0:
                ref_b = bh[lo_r - 1:lo_r]
                qi = (qb * jnp.exp2(bb - ref_b)).astype(BF16)
                kj = (k * jnp.exp2(jnp.minimum(ref_b - bh, 0.0))).astype(BF16)
                w = jnp.where(lane < lo_r, _dot_nt(qi, kj), w)
            blocks.append(w)
        a = jnp.concatenate(blocks, axis=0).astype(BF16)
        o_ref[:, vs] = o + _dot(a, v)

        k_out = (k * jnp.exp2(bl - bh)).astype(BF16)
        state_ref[h] = st * jnp.exp2(bl) + _dot_tn(v, k_out)

    @pl.when(c == nc - 1)
    def _():
        st_ref[...] = state_ref[...]


def _gla_kernel(*refs, nc, n_chunks):
    step = pl.program_id(0)

    @pl.when(step < n_chunks)
    def _():
        _gla_chunk(lax.rem(step, nc), nc, refs)

    @pl.when(step >= n_chunks)
    def _():
        o_ref = refs[-3]
        o_ref[...] = jnp.zeros_like(o_ref)


def _gla_prompt(z, la, batch, seq, m_rows):
    nc = seq // GLA_CHUNK
    n_chunks = batch * nc
    nq = GLA_KWIDTH // COL_TILE

    def rows(cidx):
        return lambda s: (s, cidx)

    return pl.pallas_call(
        functools.partial(_gla_kernel, nc=nc, n_chunks=n_chunks),
        grid=(m_rows // GLA_CHUNK,),
        in_specs=(
            [pl.BlockSpec((GLA_CHUNK, COL_TILE), rows(COL_QG // COL_TILE + t)) for t in range(nq)]
            + [pl.BlockSpec((GLA_CHUNK, COL_TILE), rows(COL_KG // COL_TILE + t)) for t in range(nq)]
            + [pl.BlockSpec((GLA_CHUNK, GLA_DV), rows(COL_VG // GLA_DV + t)) for t in range(GLA_HEADS)]
            + [pl.BlockSpec((GLA_CHUNK, GLA_KWIDTH), rows(0))]
        ),
        out_specs=[
            pl.BlockSpec((GLA_CHUNK, GLA_WIDTH), rows(0)),
            pl.BlockSpec((None, GLA_HEADS, GLA_DV, GLA_DK), lambda s: (jnp.minimum(s // nc, batch - 1), 0, 0, 0)),
        ],
        out_shape=[
            jax.ShapeDtypeStruct((m_rows, GLA_WIDTH), F32),
            jax.ShapeDtypeStruct((batch, GLA_HEADS, GLA_DV, GLA_DK), F32),
        ],
        scratch_shapes=[pltpu.VMEM((GLA_HEADS, GLA_DV, GLA_DK), F32)],
        compiler_params=pltpu.CompilerParams(
            dimension_semantics=("arbitrary",), vmem_limit_bytes=VMEM_LIMIT),
        name="gla_prompt",
    )(*([z] * (2 * nq + GLA_HEADS)), la)


def _sample_kernel(qa_ref, zs_ref, la_ref, kbuf_ref, vbuf_ref, sink_ref, s0_ref,
                   att_ref, go_ref, st_ref, *, dec_seq):
    S = dec_seq
    R = S * ATT_GROUP
    t_row = lax.broadcasted_iota(jnp.int32, (R, 1), 0) // ATT_GROUP
    c_idx = lax.broadcasted_iota(jnp.int32, (R, kbuf_ref.shape[0]), 1)
    mask_c = c_idx >= t_row
    for kv in range(ATT_KV_HEADS):
        ks = slice(kv * ATT_HEAD_DIM, (kv + 1) * ATT_HEAD_DIM)
        q = qa_ref[kv]
        sink = sink_ref[kv]
        sc = _dot_nt(q.astype(BF16), kbuf_ref[:, ks].astype(BF16))
        sc = jnp.where(mask_c, sc, -jnp.inf)
        mx = jnp.maximum(jnp.max(sc, axis=-1, keepdims=True), sink)
        kn = zs_ref[:, COL_KA + kv * ATT_HEAD_DIM:COL_KA + (kv + 1) * ATT_HEAD_DIM]
        vn = zs_ref[:, COL_VA + kv * ATT_HEAD_DIM:COL_VA + (kv + 1) * ATT_HEAD_DIM]
        sn = []
        for s in range(S):
            v = jnp.sum(q * kn[s:s + 1], axis=-1, keepdims=True)
            v = jnp.where(t_row >= s, v, -jnp.inf)
            sn.append(v)
            mx = jnp.maximum(mx, v)
        pc = jnp.exp(sc - mx)
        den = jnp.sum(pc, axis=-1, keepdims=True) + jnp.exp(sink - mx)
        o = _dot(pc.astype(BF16), vbuf_ref[:, ks].astype(BF16))
        for s in range(S):
            pn = jnp.exp(sn[s] - mx)
            den = den + pn
            o = o + pn * vn[s:s + 1]
        att_ref[kv] = o / den

    rows8 = zs_ref.shape[0]
    rid = lax.broadcasted_iota(jnp.int32, (rows8, 1), 0)
    la = la_ref[...]
    run = la[0:1]
    b_all = jnp.where(rid == 0, run, 0.0)
    for t in range(1, S):
        run = run + la[t:t + 1]
        b_all = jnp.where(rid == t, run, b_all)
    real = (rid < S).astype(F32)
    scale = GLA_DK ** -0.5
    zpad = jnp.zeros((GLA_DK - 3 * rows8, GLA_DK), F32)
    for h in range(GLA_HEADS):
        ks = slice(h * GLA_DK, (h + 1) * GLA_DK)
        q = zs_ref[:, COL_QG + h * GLA_DK:COL_QG + (h + 1) * GLA_DK] * scale
        k = zs_ref[:, COL_KG + h * GLA_DK:COL_KG + (h + 1) * GLA_DK]
        v = zs_ref[:, COL_VG + h * GLA_DV:COL_VG + (h + 1) * GLA_DV]
        bh = b_all[:, ks] * real
        bl = bh[S - 1:S]
        s0 = s0_ref[h]
        e_l = jnp.broadcast_to(jnp.exp(bl), (rows8, GLA_DK))
        k_out = k * jnp.exp(jnp.minimum(bl - bh, 0.0)) * real
        q_in = q * jnp.exp(bh) * real
        cols = jnp.concatenate([e_l, k_out, q_in, zpad], axis=0).T
        o = jnp.zeros((rows8, GLA_DV), F32)
        for t in range(S):
            o_t = jnp.sum(cols[:, 2 * rows8 + t:2 * rows8 + t + 1] * s0, axis=0, keepdims=True)
            o = jnp.where(rid == t, o_t, o)
        for s in range(S):
            zed = q * k[s:s + 1] * jnp.exp(jnp.minimum(bh - bh[s:s + 1], 0.0))
            w = jnp.sum(zed, axis=-1, keepdims=True)
            w = jnp.where((rid >= s) & (rid < S), w, 0.0)
            o = o + w * v[s:s + 1]
        go_ref[:, h * GLA_DV:(h + 1) * GLA_DV] = o
        st = cols[:, 0:1] * s0
        for t in range(S):
            st = st + cols[:, rows8 + t:rows8 + t + 1] * v[t:t + 1]
        st_ref[h] = st


def _sample_mixer(qa_s, zs8, la8, kbuf, vbuf, sink_rows, s0, dec_seq, l):
    nb = zs8.shape[0]
    rows8 = zs8.shape[1]
    R = dec_seq * ATT_GROUP
    wb = kbuf.shape[2]
    return pl.pallas_call(
        functools.partial(_sample_kernel, dec_seq=dec_seq),
        grid=(nb,),
        in_specs=[
            pl.BlockSpec((None, ATT_KV_HEADS, R, ATT_HEAD_DIM), lambda b: (b, 0, 0, 0)),
            pl.BlockSpec((None, rows8, Z_WIDTH), lambda b: (b, 0, 0)),
            pl.BlockSpec((None, rows8, GLA_KWIDTH), lambda b: (b, 0, 0)),
            pl.BlockSpec((None, None, wb, KV_WIDTH), lambda b: (l, b, 0, 0)),
            pl.BlockSpec((None, None, wb, KV_WIDTH), lambda b: (l, b, 0, 0)),
            pl.BlockSpec((ATT_KV_HEADS, R, 1), lambda b: (0, 0, 0)),
            pl.BlockSpec((None, None, GLA_HEADS, GLA_DK, GLA_DV), lambda b: (l, b, 0, 0, 0)),
        ],
        out_specs=[
            pl.BlockSpec((None, ATT_KV_HEADS, R, ATT_HEAD_DIM), lambda b: (b, 0, 0, 0)),
            pl.BlockSpec((None, rows8, GLA_WIDTH), lambda b: (b, 0, 0)),
            pl.BlockSpec((None, GLA_HEADS, GLA_DK, GLA_DV), lambda b: (b, 0, 0, 0)),
        ],
        out_shape=[
            jax.ShapeDtypeStruct((nb, ATT_KV_HEADS, R, ATT_HEAD_DIM), F32),
            jax.ShapeDtypeStruct((nb, rows8, GLA_WIDTH), F32),
            jax.ShapeDtypeStruct((nb, GLA_HEADS, GLA_DK, GLA_DV), F32),
        ],
        compiler_params=pltpu.CompilerParams(
            dimension_semantics=("parallel",), vmem_limit_bytes=VMEM_LIMIT),
        name="sample_mixer",
    )(qa_s, zs8, la8, kbuf, vbuf, sink_rows, s0)


def _merge_kernel(*refs):
    x_ref, att_ref, go_ref = refs[:3]
    rg_refs = refs[3:3 + GLA_HEADS]
    ggla_ref, wo_ref, gpost_ref, o_ref = refs[3 + GLA_HEADS:]
    parts = []
    for h in range(GLA_HEADS):
        vs = slice(h * GLA_DV, (h + 1) * GLA_DV)
        rg = rg_refs[h][...]
        parts.append((_rms(go_ref[:, vs], ggla_ref[...]) * (rg * jax.nn.sigmoid(rg))).astype(BF16))
    go = jnp.concatenate(parts, axis=-1)
    y = _dot(att_ref[...], wo_ref[:ATT_WIDTH, :]) + _dot(go, wo_ref[ATT_WIDTH:, :])
    o_ref[...] = x_ref[...] + _rms(y, gpost_ref[...])


def _merge(x, att, go, z, g_gla, w_out, gains, l):
    m = x.shape[0]
    return pl.pallas_call(
        _merge_kernel,
        grid=(m // TM_MERGE,),
        in_specs=(
            [pl.BlockSpec((TM_MERGE, D_MODEL), lambda i: (i, 0)),
             pl.BlockSpec((TM_MERGE, ATT_WIDTH), lambda i: (i, 0)),
             pl.BlockSpec((TM_MERGE, GLA_WIDTH), lambda i: (i, 0))]
            + [pl.BlockSpec((TM_MERGE, GLA_DV), lambda i, h=h: (i, COL_RG // GLA_DV + h)) for h in range(GLA_HEADS)]
            + [pl.BlockSpec((None, 1, GLA_DV), lambda i: (l, 0, 0)),
               pl.BlockSpec((None, ATT_WIDTH + GLA_WIDTH, D_MODEL), lambda i: (l, 0, 0)),
               _gain_spec(l, 3)]
        ),
        out_specs=pl.BlockSpec((TM_MERGE, D_MODEL), lambda i: (i, 0)),
        out_shape=jax.ShapeDtypeStruct((m, D_MODEL), F32),
        compiler_params=pltpu.CompilerParams(
            dimension_semantics=("parallel",), vmem_limit_bytes=VMEM_LIMIT),
        name="merge",
    )(x, att, go, *([z] * GLA_HEADS), g_gla, w_out, gains)


def _prep_w_in(w):
    col = jnp.arange(Z_WIDTH)
    colscale = jnp.where((col >= COL_QA) & (col < COL_QA + ATT_WIDTH), ATT_HEAD_DIM ** -0.5, 1.0).astype(w.dtype)
    w = jnp.pad(w, ((0, 0), (0, 0), (0, Z_WIDTH - w.shape[-1])))
    return (w * colscale).astype(BF16)


def kernel(x_prompt, x_sample, cache_k_win, cache_v_win, state_gla, meta_tokens, norm_gains,
           w_ffn_gate, w_ffn_up, w_ffn_down, w_in, w_gate_up, b_gate, attn_sinks, gla_norm, w_out):
    batch, seq, _ = x_prompt.shape
    dec_batch, dec_seq, _ = x_sample.shape
    depth = norm_gains.shape[0]
    lp = seq + BLOCK
    nb = lp // BLOCK
    mp = batch * lp
    ms = dec_batch * dec_seq
    m_pad = -(-(mp + ms) // TM) * TM
    wb = cache_k_win.shape[2]
    rows8 = 8
    assert dec_seq <= rows8 and lp % GLA_CHUNK == 0

    head = jnp.concatenate([jnp.zeros((PAD, D_MODEL), F32), meta_tokens.astype(F32)], axis=0)
    pieces = []
    for b in range(batch):
        pieces += [head, x_prompt[b]]
    pieces += [x_sample.reshape(ms, D_MODEL), jnp.zeros((m_pad - mp - ms, D_MODEL), F32)]
    x = jnp.concatenate(pieces, axis=0)

    wg, wu, wd = w_ffn_gate, w_ffn_up, w_ffn_down
    wo = w_out.astype(BF16)
    w_in_p = _prep_w_in(w_in)
    wa_p = jnp.pad(w_gate_up, ((0, 0), (0, LR_PAD - GLA_RANK), (0, 0))).astype(BF16)
    ba = b_gate.reshape(depth, 1, GLA_KWIDTH)
    g_gla = gla_norm.reshape(depth, 1, GLA_DV)
    gains = norm_gains.reshape(depth, 6, 1, D_MODEL)
    kbuf = cache_k_win.reshape(depth, dec_batch, wb, KV_WIDTH)
    vbuf = cache_v_win.reshape(depth, dec_batch, wb, KV_WIDTH)
    tail = jnp.zeros((m_pad - mp - ms, GLA_WIDTH), F32)

    pk, pv, ps, sk, sv, ss = [], [], [], [], [], []
    for l in range(depth):
        x = _ffn(x, gains, wg, wu, wd, l, 0)

        z, la = _proj(x, gains, w_in_p, wa_p, ba, l)
        att = _attn_prompt(z, attn_sinks[l], batch, nb, m_pad)
        go, st_p = _gla_prompt(z, la, batch, lp, m_pad)

        zs = z[mp:mp + ms].reshape(dec_batch, dec_seq, Z_WIDTH)
        zs8 = jnp.pad(zs, ((0, 0), (0, rows8 - dec_seq), (0, 0)))
        la8 = jnp.pad(la[mp:mp + ms].reshape(dec_batch, dec_seq, GLA_KWIDTH), ((0, 0), (0, rows8 - dec_seq), (0, 0)))
        qa_s = zs[:, :, COL_QA:COL_QA + ATT_WIDTH].reshape(dec_batch, dec_seq, ATT_KV_HEADS, ATT_GROUP, ATT_HEAD_DIM)
        qa_s = qa_s.transpose(0, 2, 1, 3, 4).reshape(dec_batch, ATT_KV_HEADS, dec_seq * ATT_GROUP, ATT_HEAD_DIM)
        sink_rows = jnp.tile(attn_sinks[l].reshape(ATT_KV_HEADS, 1, ATT_GROUP), (1, dec_seq, 1))
        sink_rows = sink_rows.reshape(ATT_KV_HEADS, dec_seq * ATT_GROUP, 1)
        att_s, go_s, st_s = _sample_mixer(qa_s, zs8, la8, kbuf, vbuf, sink_rows, state_gla, dec_seq, l)
        att_s = att_s.reshape(dec_batch, ATT_KV_HEADS, dec_seq, ATT_GROUP, ATT_HEAD_DIM)
        att_s = att_s.transpose(0, 2, 1, 3, 4).reshape(ms, ATT_WIDTH)

        att = lax.dynamic_update_slice(att, jnp.concatenate([att_s, tail], axis=0).astype(BF16), (mp, 0))
        go = lax.dynamic_update_slice(go, jnp.concatenate([go_s[:, :dec_seq].reshape(ms, GLA_WIDTH), tail], axis=0), (mp, 0))
        x = _merge(x, att, go, z, g_gla, wo, gains, l)

        win = [z[b * lp + lp - WINDOW:(b + 1) * lp, COL_KA:COL_VA + KV_WIDTH] for b in range(batch)]
        win = jnp.stack(win).reshape(batch, WINDOW, 2, ATT_KV_HEADS, ATT_HEAD_DIM)
        pk.append(win[:, :, 0])
        pv.append(win[:, :, 1])
        ps.append(st_p.transpose(0, 1, 3, 2))
        kn = zs[:, :, COL_KA:COL_KA + KV_WIDTH].reshape(dec_batch, dec_seq, ATT_KV_HEADS, ATT_HEAD_DIM)
        vn = zs[:, :, COL_VA:COL_VA + KV_WIDTH].reshape(dec_batch, dec_seq, ATT_KV_HEADS, ATT_HEAD_DIM)
        sk.append(jnp.concatenate([cache_k_win[l], kn], axis=1)[:, -wb:])
        sv.append(jnp.concatenate([cache_v_win[l], vn], axis=1)[:, -wb:])
        ss.append(st_s)

        x = _ffn(x, gains, wg, wu, wd, l, 1)

    y_prompt = jnp.stack([x[b * lp + BLOCK:(b + 1) * lp] for b in range(batch)])
    y_sample = x[mp:mp + ms].reshape(dec_batch, dec_seq, D_MODEL)
    return (y_prompt, y_sample, jnp.stack(pk), jnp.stack(pv), jnp.stack(ps),
            jnp.stack(sk), jnp.stack(sv), jnp.stack(ss))
```

```python
import functools

import jax
import jax.numpy as jnp
from jax import lax
from jax.experimental import pallas as pl
from jax.experimental.pallas import tpu as pltpu

F32 = jnp.float32
BF16 = jnp.bfloat16

D_MODEL = 2048
D_FF = 5632
N_META = 16
BLOCK = 128
WINDOW = 128
ATT_HEADS = 16
ATT_KV_HEADS = 2
ATT_GROUP = 8
ATT_HEAD_DIM = 64
ATT_WIDTH = 1024
KV_WIDTH = 128
GLA_HEADS = 4
GLA_DK = 128
GLA_DV = 256
GLA_KWIDTH = 512
GLA_WIDTH = 1024
GLA_RANK = 16
GLA_GATE_NORM = 16.0
GLA_CHUNK = 64
GLA_SUB = 16
EPS = 1e-6
LOG2_E = 1.4426950408889634
PAD = BLOCK - N_META

COL_QA, COL_KA, COL_VA, COL_QG, COL_KG, COL_VG, COL_RG, COL_LR = 0, 1024, 1152, 1280, 1792, 2304, 3328, 4352
COL_TILE = 256
Z_WIDTH = 4608
LR_PAD = 128

VMEM_LIMIT = 56 * 1024 * 1024
VMEM_LIMIT_FFN = 60 * 1024 * 1024

TM = 1072
TM_MERGE = 536
TF = 256
FF_CHUNK = 256
TN = 768


def _rms(x, gain):
    ms = jnp.mean(x * x, axis=-1, keepdims=True)
    return x * lax.rsqrt(ms + EPS) * gain


def _dot(a, b):
    return jnp.dot(a, b, preferred_element_type=F32)


def _dot_nt(a, b):
    return lax.dot_general(a, b, (((1,), (1,)), ((), ())), preferred_element_type=F32)


def _dot_tn(a, b):
    return lax.dot_general(a, b, (((0,), (0,)), ((), ())), preferred_element_type=F32)


def _ffn_kernel(x_ref, gpre_ref, gpost_ref, wg_ref, wu_ref, wd_ref, o_ref, xn_ref):
    j = pl.program_id(1)

    @pl.when(j == 0)
    def _():
        xn_ref[...] = _rms(x_ref[...], gpre_ref[...]).astype(BF16)
        o_ref[...] = jnp.zeros_like(o_ref)

    xn = xn_ref[...]
    for c in range(TF // FF_CHUNK):
        cs = slice(c * FF_CHUNK, (c + 1) * FF_CHUNK)
        g = _dot(xn, wg_ref[:, cs].astype(BF16))
        u = _dot(xn, wu_ref[:, cs].astype(BF16))
        h = (g * jax.nn.sigmoid(g) * u).astype(BF16)
        o_ref[...] += _dot(h, wd_ref[cs, :].astype(BF16))

    @pl.when(j == pl.num_programs(1) - 1)
    def _():
        o_ref[...] = x_ref[...] + 0.5 * _rms(o_ref[...], gpost_ref[...])


def _gain_spec(l, idx):
    return pl.BlockSpec((None, None, 1, D_MODEL), lambda *_: (l, idx, 0, 0))


def _ffn(x, gains, wg, wu, wd, l, f):
    m = x.shape[0]
    return pl.pallas_call(
        _ffn_kernel,
        grid=(m // TM, D_FF // TF),
        in_specs=[
            pl.BlockSpec((TM, D_MODEL), lambda i, j: (i, 0), pipeline_mode=pl.Buffered(1)),
            _gain_spec(l, 4 * f),
            _gain_spec(l, 4 * f + 1),
            pl.BlockSpec((None, None, D_MODEL, TF), lambda i, j: (l, f, 0, j)),
            pl.BlockSpec((None, None, D_MODEL, TF), lambda i, j: (l, f, 0, j)),
            pl.BlockSpec((None, None, TF, D_MODEL), lambda i, j: (l, f, j, 0)),
        ],
        out_specs=pl.BlockSpec((TM, D_MODEL), lambda i, j: (i, 0)),
        out_shape=jax.ShapeDtypeStruct((m, D_MODEL), F32),
        scratch_shapes=[pltpu.VMEM((TM, D_MODEL), BF16)],
        compiler_params=pltpu.CompilerParams(
            dimension_semantics=("parallel", "arbitrary"), vmem_limit_bytes=VMEM_LIMIT_FFN),
        name="ffn",
    )(x, gains, gains, wg, wu, wd)


def _proj_kernel(x_ref, g_ref, wt_ref, cs_ref, wa_ref, ba_ref, z_ref, la_ref, xn_ref, *, n_cols):
    j = pl.program_id(1)
    last = pl.num_programs(1) - 1
    n_valid = n_cols - (Z_WIDTH - TN)

    @pl.when(j == 0)
    def _():
        xn_ref[...] = _rms(x_ref[...], g_ref[...]).astype(BF16)

    @pl.when(j < last)
    def _():
        z_ref[...] = _dot_nt(xn_ref[...], wt_ref[...].astype(BF16)) * cs_ref[...]

    @pl.when(j == last)
    def _():
        z_ref[:, :n_valid] = _dot_nt(xn_ref[...], wt_ref[:n_valid, :].astype(BF16)) * cs_ref[:, :n_valid]
        z_ref[:, n_valid:] = jnp.zeros((z_ref.shape[0], TN - n_valid), F32)
        lr_off = COL_LR - (Z_WIDTH - TN)
        lr = z_ref[:, lr_off:lr_off + LR_PAD].astype(BF16)
        logit = _dot(lr, wa_ref[...]) + ba_ref[...]
        log_sig = jnp.minimum(logit, 0.0) - jnp.log1p(jnp.exp(-jnp.abs(logit)))
        la_ref[...] = log_sig * (1.0 / GLA_GATE_NORM)


def _proj(x, gains, w_in_t, colscale, wa_p, ba, l):
    m = x.shape[0]
    n_cols = w_in_t.shape[1]
    assert Z_WIDTH - TN < COL_LR and COL_LR + GLA_RANK <= n_cols <= Z_WIDTH
    return pl.pallas_call(
        functools.partial(_proj_kernel, n_cols=n_cols),
        grid=(m // TM, Z_WIDTH // TN),
        in_specs=[
            pl.BlockSpec((TM, D_MODEL), lambda i, j: (i, 0)),
            _gain_spec(l, 2),
            pl.BlockSpec((None, TN, D_MODEL), lambda i, j: (l, j, 0)),
            pl.BlockSpec((1, TN), lambda i, j: (0, j)),
            pl.BlockSpec((None, LR_PAD, GLA_KWIDTH), lambda i, j: (l, 0, 0)),
            pl.BlockSpec((None, 1, GLA_KWIDTH), lambda i, j: (l, 0, 0)),
        ],
        out_specs=[
            pl.BlockSpec((TM, TN), lambda i, j: (i, j)),
            pl.BlockSpec((TM, GLA_KWIDTH), lambda i, j: (i, 0)),
        ],
        out_shape=[
            jax.ShapeDtypeStruct((m, Z_WIDTH), F32),
            jax.ShapeDtypeStruct((m, GLA_KWIDTH), F32),
        ],
        scratch_shapes=[pltpu.VMEM((TM, D_MODEL), BF16)],
        compiler_params=pltpu.CompilerParams(
            dimension_semantics=("parallel", "arbitrary"), vmem_limit_bytes=VMEM_LIMIT),
        name="proj",
    )(x, gains, w_in_t, colscale, wa_p, ba)


def _pair_blockdiag(x128, kv):
    lane = lax.broadcasted_iota(jnp.int32, x128.shape, 1)
    own = jnp.where((lane >= kv * ATT_HEAD_DIM) & (lane < (kv + 1) * ATT_HEAD_DIM), x128, 0.0)
    other = pltpu.roll(own, ATT_HEAD_DIM, axis=1)
    lo, hi = (own, other) if kv == 0 else (other, own)
    return jnp.concatenate([lo, hi], axis=0).astype(BF16)


def _attn_block(i, sink_ref, q_ref, kc_ref, kp_ref, vc_ref, vp_ref, o_ref):
    kb = 2 * BLOCK
    row = lax.broadcasted_iota(jnp.int32, (BLOCK, kb), 0)
    col = lax.broadcasted_iota(jnp.int32, (BLOCK, kb), 1)
    diff = row + BLOCK - col
    key_pos = col + (i - 1) * BLOCK
    mask = (diff >= 0) & (diff <= WINDOW) & (key_pos >= PAD)
    lane = lax.broadcasted_iota(jnp.int32, (BLOCK, 2 * ATT_HEAD_DIM), 1)
    kk = jnp.concatenate([kp_ref[...], kc_ref[...]], axis=0)
    vv = jnp.concatenate([vp_ref[...], vc_ref[...]], axis=0)
    slabs = ATT_GROUP // 2
    for kv in range(ATT_KV_HEADS):
        k2 = _pair_blockdiag(kk, kv)
        v2 = _pair_blockdiag(vv, kv)
        c0 = kv * slabs * 2 * ATT_HEAD_DIM
        q4 = q_ref[:, c0:c0 + slabs * 2 * ATT_HEAD_DIM]
        q4 = jnp.concatenate([q4[:, p * 128:(p + 1) * 128] for p in range(slabs)], axis=0).astype(BF16)
        s = _dot_nt(q4, k2)
        probs, inv = [], []
        for p in range(slabs):
            halves, rden = [], []
            for e in range(2):
                sink = sink_ref[kv * ATT_GROUP + 2 * p + e]
                sp = jnp.where(mask, s[p * BLOCK:(p + 1) * BLOCK, e * kb:(e + 1) * kb], -jnp.inf)
                mx = jnp.maximum(jnp.max(sp, axis=-1, keepdims=True), sink)
                pe = jnp.exp(sp - mx)
                rden.append(1.0 / (jnp.sum(pe, axis=-1, keepdims=True) + jnp.exp(sink - mx)))
                halves.append(pe.astype(BF16))
            probs.append(jnp.concatenate(halves, axis=1))
            inv.append(jnp.where(lane < ATT_HEAD_DIM, rden[0], rden[1]))
        o = _dot(jnp.concatenate(probs, axis=0), v2)
        for p in range(slabs):
            o_ref[:, c0 + p * 128:c0 + (p + 1) * 128] = (o[p * BLOCK:(p + 1) * BLOCK] * inv[p]).astype(o_ref.dtype)


def _attn_kernel(sink_ref, q_ref, kc_ref, kp_ref, vc_ref, vp_ref, o_ref, *, nb, n_blocks):
    step = pl.program_id(0)

    @pl.when(step < n_blocks)
    def _():
        _attn_block(lax.rem(step, nb), sink_ref, q_ref, kc_ref, kp_ref, vc_ref, vp_ref, o_ref)

    @pl.when(step >= n_blocks)
    def _():
        o_ref[...] = jnp.zeros_like(o_ref)


def _attn_prompt(z, sinks, batch, nb, m_rows):
    n_blocks = batch * nb

    def cur(c):
        return lambda s, sink: (jnp.minimum(s, n_blocks - 1), c)

    def prev(c):
        return lambda s, sink: (jnp.maximum(jnp.minimum(s, n_blocks - 1) - 1, 0), c)

    return pl.pallas_call(
        functools.partial(_attn_kernel, nb=nb, n_blocks=n_blocks),
        grid_spec=pltpu.PrefetchScalarGridSpec(
            num_scalar_prefetch=1,
            grid=(m_rows // BLOCK,),
            in_specs=[
                pl.BlockSpec((BLOCK, ATT_WIDTH), cur(COL_QA // ATT_WIDTH)),
                pl.BlockSpec((BLOCK, KV_WIDTH), cur(COL_KA // KV_WIDTH)),
                pl.BlockSpec((BLOCK, KV_WIDTH), prev(COL_KA // KV_WIDTH)),
                pl.BlockSpec((BLOCK, KV_WIDTH), cur(COL_VA // KV_WIDTH)),
                pl.BlockSpec((BLOCK, KV_WIDTH), prev(COL_VA // KV_WIDTH)),
            ],
            out_specs=pl.BlockSpec((BLOCK, ATT_WIDTH), lambda s, sink: (s, 0)),
        ),
        out_shape=jax.ShapeDtypeStruct((m_rows, ATT_WIDTH), BF16),
        compiler_params=pltpu.CompilerParams(
            dimension_semantics=("arbitrary",), vmem_limit_bytes=VMEM_LIMIT),
        name="attn_prompt",
    )(sinks, z, z, z, z, z)


def _cumsum_rows(la, tri):
    hi = la.astype(BF16)
    r1 = la - hi.astype(F32)
    mid = r1.astype(BF16)
    lo = (r1 - mid.astype(F32)).astype(BF16)
    return _dot(tri, hi) + _dot(tri, mid) + _dot(tri, lo)


def _gla_chunk(c, nc, refs):
    nq = GLA_KWIDTH // COL_TILE
    q_refs, k_refs, v_refs = refs[:nq], refs[nq:2 * nq], refs[2 * nq:2 * nq + GLA_HEADS]
    la_ref, o_ref, st_ref, state_ref = refs[2 * nq + GLA_HEADS:]
    per = COL_TILE // GLA_DK
    C = GLA_CHUNK

    @pl.when(c == 0)
    def _():
        state_ref[...] = jnp.zeros_like(state_ref)

    rowc = lax.broadcasted_iota(jnp.int32, (C, C), 0)
    colc = lax.broadcasted_iota(jnp.int32, (C, C), 1)
    tri = (rowc >= colc).astype(BF16)
    b_all = _cumsum_rows(la_ref[...], tri) * LOG2_E

    pos = c * C + lax.broadcasted_iota(jnp.int32, (C, 1), 0)
    valid = (pos >= PAD).astype(F32)

    lane = lax.broadcasted_iota(jnp.int32, (GLA_SUB, C), 1)
    rsub = lax.broadcasted_iota(jnp.int32, (GLA_SUB, C), 0)
    scale = GLA_DK ** -0.5

    for h in range(GLA_HEADS):
        ks = slice(h * GLA_DK, (h + 1) * GLA_DK)
        vs = slice(h * GLA_DV, (h + 1) * GLA_DV)
        sub = slice((h % per) * GLA_DK, (h % per + 1) * GLA_DK)
        q = q_refs[h // per][:, sub] * scale
        k = k_refs[h // per][:, sub] * valid
        v = v_refs[h][...].astype(BF16)
        bh = b_all[:, ks]
        bl = bh[C - 1:C, :]
        st = state_ref[h]

        o = _dot_nt((q * jnp.exp2(bh)).astype(BF16), st.astype(BF16))

        blocks = []
        for i in range(C // GLA_SUB):
            lo_r = i * GLA_SUB
            qb = q[lo_r:lo_r + GLA_SUB]
            bb = bh[lo_r:lo_r + GLA_SUB]
            w = jnp.zeros((GLA_SUB, C), F32)
            for s in range(GLA_SUB):
                r = lo_r + s
                zed = qb * k[r:r + 1] * jnp.exp2(bb - bh[r:r + 1])
                w = jnp.where(lane == r, jnp.sum(zed, axis=-1, keepdims=True), w)
            w = jnp.where(lane <= rsub + lo_r, w, 0.0)
            if i > 0:
                ref_b = bh[lo_r - 1:lo_r]
                qi = (qb * jnp.exp2(bb - ref_b)).astype(BF16)
                kj = (k * jnp.exp2(jnp.minimum(ref_b - bh, 0.0))).astype(BF16)
                w = jnp.where(lane < lo_r, _dot_nt(qi, kj), w)
            blocks.append(w)
        a = jnp.concatenate(blocks, axis=0).astype(BF16)
        o_ref[:, vs] = o + _dot(a, v)

        k_out = (k * jnp.exp2(bl - bh)).astype(BF16)
        state_ref[h] = st * jnp.exp2(bl) + _dot_tn(v, k_out)

    @pl.when(c == nc - 1)
    def _():
        st_ref[...] = state_ref[...]


def _gla_kernel(*refs, nc, n_chunks):
    step = pl.program_id(0)

    @pl.when(step < n_chunks)
    def _():
        _gla_chunk(lax.rem(step, nc), nc, refs)

    @pl.when(step >= n_chunks)
    def _():
        o_ref = refs[-3]
        o_ref[...] = jnp.zeros_like(o_ref)


def _gla_prompt(z, la, batch, seq, m_rows):
    nc = seq // GLA_CHUNK
    n_chunks = batch * nc
    nq = GLA_KWIDTH // COL_TILE

    def rows(cidx):
        return lambda s: (s, cidx)

    return pl.pallas_call(
        functools.partial(_gla_kernel, nc=nc, n_chunks=n_chunks),
        grid=(m_rows // GLA_CHUNK,),
        in_specs=(
            [pl.BlockSpec((GLA_CHUNK, COL_TILE), rows(COL_QG // COL_TILE + t)) for t in range(nq)]
            + [pl.BlockSpec((GLA_CHUNK, COL_TILE), rows(COL_KG // COL_TILE + t)) for t in range(nq)]
            + [pl.BlockSpec((GLA_CHUNK, GLA_DV), rows(COL_VG // GLA_DV + t)) for t in range(GLA_HEADS)]
            + [pl.BlockSpec((GLA_CHUNK, GLA_KWIDTH), rows(0))]
        ),
        out_specs=[
            pl.BlockSpec((GLA_CHUNK, GLA_WIDTH), rows(0)),
            pl.BlockSpec((None, GLA_HEADS, GLA_DV, GLA_DK), lambda s: (jnp.minimum(s // nc, batch - 1), 0, 0, 0)),
        ],
        out_shape=[
            jax.ShapeDtypeStruct((m_rows, GLA_WIDTH), F32),
            jax.ShapeDtypeStruct((batch, GLA_HEADS, GLA_DV, GLA_DK), F32),
        ],
        scratch_shapes=[pltpu.VMEM((GLA_HEADS, GLA_DV, GLA_DK), F32)],
        compiler_params=pltpu.CompilerParams(
            dimension_semantics=("arbitrary",), vmem_limit_bytes=VMEM_LIMIT),
        name="gla_prompt",
    )(*([z] * (2 * nq + GLA_HEADS)), la)


def _sample_kernel(qa_ref, zs_ref, la_ref, kbuf_ref, vbuf_ref, sink_ref, s0_ref,
                   att_ref, go_ref, st_ref, *, dec_seq):
    S = dec_seq
    R = S * ATT_GROUP
    t_row = lax.broadcasted_iota(jnp.int32, (R, 1), 0) // ATT_GROUP
    c_idx = lax.broadcasted_iota(jnp.int32, (R, kbuf_ref.shape[0]), 1)
    mask_c = c_idx >= t_row
    for kv in range(ATT_KV_HEADS):
        ks = slice(kv * ATT_HEAD_DIM, (kv + 1) * ATT_HEAD_DIM)
        q = qa_ref[kv]
        sink = sink_ref[kv]
        sc = _dot_nt(q.astype(BF16), kbuf_ref[:, ks].astype(BF16))
        sc = jnp.where(mask_c, sc, -jnp.inf)
        mx = jnp.maximum(jnp.max(sc, axis=-1, keepdims=True), sink)
        kn = zs_ref[:, COL_KA + kv * ATT_HEAD_DIM:COL_KA + (kv + 1) * ATT_HEAD_DIM]
        vn = zs_ref[:, COL_VA + kv * ATT_HEAD_DIM:COL_VA + (kv + 1) * ATT_HEAD_DIM]
        sn = []
        for s in range(S):
            v = jnp.sum(q * kn[s:s + 1], axis=-1, keepdims=True)
            v = jnp.where(t_row >= s, v, -jnp.inf)
            sn.append(v)
            mx = jnp.maximum(mx, v)
        pc = jnp.exp(sc - mx)
        den = jnp.sum(pc, axis=-1, keepdims=True) + jnp.exp(sink - mx)
        o = _dot(pc.astype(BF16), vbuf_ref[:, ks].astype(BF16))
        for s in range(S):
            pn = jnp.exp(sn[s] - mx)
            den = den + pn
            o = o + pn * vn[s:s + 1]
        att_ref[kv] = o / den

    rows8 = zs_ref.shape[0]
    rid = lax.broadcasted_iota(jnp.int32, (rows8, 1), 0)
    la = la_ref[...]
    run = la[0:1]
    b_all = jnp.where(rid == 0, run, 0.0)
    for t in range(1, S):
        run = run + la[t:t + 1]
        b_all = jnp.where(rid == t, run, b_all)
    real = (rid < S).astype(F32)
    scale = GLA_DK ** -0.5
    for h in range(GLA_HEADS):
        ks = slice(h * GLA_DK, (h + 1) * GLA_DK)
        q = zs_ref[:, COL_QG + h * GLA_DK:COL_QG + (h + 1) * GLA_DK] * scale
        k = zs_ref[:, COL_KG + h * GLA_DK:COL_KG + (h + 1) * GLA_DK]
        v = zs_ref[:, COL_VG + h * GLA_DV:COL_VG + (h + 1) * GLA_DV]
        bh = b_all[:, ks] * real
        bl = bh[S - 1:S]
        s0 = s0_ref[h]
        k_out = k * jnp.exp(jnp.minimum(bl - bh, 0.0)) * real
        q_in = q * jnp.exp(bh) * real
        o = _dot(q_in.astype(BF16), s0.astype(BF16))
        for s in range(S):
            zed = q * k[s:s + 1] * jnp.exp(jnp.minimum(bh - bh[s:s + 1], 0.0))
            w = jnp.sum(zed, axis=-1, keepdims=True)
            w = jnp.where((rid >= s) & (rid < S), w, 0.0)
            o = o + w * v[s:s + 1]
        go_ref[:, h * GLA_DV:(h + 1) * GLA_DV] = o
        e_rows = jnp.broadcast_to(jnp.exp(bl), (GLA_DK, GLA_DK)).T
        decay = jnp.concatenate([e_rows] * (GLA_DV // GLA_DK), axis=1)
        st_ref[h] = decay * s0 + _dot_tn(k_out.astype(BF16), v.astype(BF16))


def _sample_mixer(qa_s, zs8, la8, kbuf, vbuf, sink_rows, s0, dec_seq, l):
    nb = zs8.shape[0]
    rows8 = zs8.shape[1]
    R = dec_seq * ATT_GROUP
    wb = kbuf.shape[2]
    return pl.pallas_call(
        functools.partial(_sample_kernel, dec_seq=dec_seq),
        grid=(nb,),
        in_specs=[
            pl.BlockSpec((None, ATT_KV_HEADS, R, ATT_HEAD_DIM), lambda b: (b, 0, 0, 0)),
            pl.BlockSpec((None, rows8, Z_WIDTH), lambda b: (b, 0, 0)),
            pl.BlockSpec((None, rows8, GLA_KWIDTH), lambda b: (b, 0, 0)),
            pl.BlockSpec((None, None, wb, KV_WIDTH), lambda b: (l, b, 0, 0)),
            pl.BlockSpec((None, None, wb, KV_WIDTH), lambda b: (l, b, 0, 0)),
            pl.BlockSpec((ATT_KV_HEADS, R, 1), lambda b: (0, 0, 0)),
            pl.BlockSpec((None, None, GLA_HEADS, GLA_DK, GLA_DV), lambda b: (l, b, 0, 0, 0)),
        ],
        out_specs=[
            pl.BlockSpec((None, ATT_KV_HEADS, R, ATT_HEAD_DIM), lambda b: (b, 0, 0, 0)),
            pl.BlockSpec((None, rows8, GLA_WIDTH), lambda b: (b, 0, 0)),
            pl.BlockSpec((None, GLA_HEADS, GLA_DK, GLA_DV), lambda b: (b, 0, 0, 0)),
        ],
        out_shape=[
            jax.ShapeDtypeStruct((nb, ATT_KV_HEADS, R, ATT_HEAD_DIM), F32),
            jax.ShapeDtypeStruct((nb, rows8, GLA_WIDTH), F32),
            jax.ShapeDtypeStruct((nb, GLA_HEADS, GLA_DK, GLA_DV), F32),
        ],
        compiler_params=pltpu.CompilerParams(
            dimension_semantics=("parallel",), vmem_limit_bytes=VMEM_LIMIT),
        name="sample_mixer",
    )(qa_s, zs8, la8, kbuf, vbuf, sink_rows, s0)


def _merge_kernel(*refs):
    x_ref, att_ref, go_ref = refs[:3]
    rg_refs = refs[3:3 + GLA_HEADS]
    ggla_ref, wo_ref, gpost_ref, o_ref = refs[3 + GLA_HEADS:]
    parts = []
    for h in range(GLA_HEADS):
        vs = slice(h * GLA_DV, (h + 1) * GLA_DV)
        rg = rg_refs[h][...]
        parts.append((_rms(go_ref[:, vs], ggla_ref[...]) * (rg * jax.nn.sigmoid(rg))).astype(BF16))
    go = jnp.concatenate(parts, axis=-1)
    y = _dot(att_ref[...], wo_ref[:ATT_WIDTH, :]) + _dot(go, wo_ref[ATT_WIDTH:, :])
    o_ref[...] = x_ref[...] + _rms(y, gpost_ref[...])


def _merge(x, att, go, z, g_gla, w_out, gains, l):
    m = x.shape[0]
    return pl.pallas_call(
        _merge_kernel,
        grid=(m // TM_MERGE,),
        in_specs=(
            [pl.BlockSpec((TM_MERGE, D_MODEL), lambda i: (i, 0)),
             pl.BlockSpec((TM_MERGE, ATT_WIDTH), lambda i: (i, 0)),
             pl.BlockSpec((TM_MERGE, GLA_WIDTH), lambda i: (i, 0))]
            + [pl.BlockSpec((TM_MERGE, GLA_DV), lambda i, h=h: (i, COL_RG // GLA_DV + h)) for h in range(GLA_HEADS)]
            + [pl.BlockSpec((None, 1, GLA_DV), lambda i: (l, 0, 0)),
               pl.BlockSpec((None, ATT_WIDTH + GLA_WIDTH, D_MODEL), lambda i: (l, 0, 0)),
               _gain_spec(l, 3)]
        ),
        out_specs=pl.BlockSpec((TM_MERGE, D_MODEL), lambda i: (i, 0)),
        out_shape=jax.ShapeDtypeStruct((m, D_MODEL), F32),
        compiler_params=pltpu.CompilerParams(
            dimension_semantics=("parallel",), vmem_limit_bytes=VMEM_LIMIT),
        name="merge",
    )(x, att, go, *([z] * GLA_HEADS), g_gla, w_out, gains)


def kernel(x_prompt, x_sample, cache_k_win, cache_v_win, state_gla, meta_tokens, norm_gains,
           w_ffn_gate, w_ffn_up, w_ffn_down, w_in, w_gate_up, b_gate, attn_sinks, gla_norm, w_out):
    batch, seq, _ = x_prompt.shape
    dec_batch, dec_seq, _ = x_sample.shape
    depth = norm_gains.shape[0]
    lp = seq + BLOCK
    nb = lp // BLOCK
    mp = batch * lp
    ms = dec_batch * dec_seq
    m_pad = -(-(mp + ms) // TM) * TM
    wb = cache_k_win.shape[2]
    rows8 = 8
    assert dec_seq <= rows8 and lp % GLA_CHUNK == 0

    head = jnp.concatenate([jnp.zeros((PAD, D_MODEL), F32), meta_tokens.astype(F32)], axis=0)
    pieces = []
    for b in range(batch):
        pieces += [head, x_prompt[b]]
    pieces += [x_sample.reshape(ms, D_MODEL), jnp.zeros((m_pad - mp - ms, D_MODEL), F32)]
    x = jnp.concatenate(pieces, axis=0)

    wg, wu, wd = w_ffn_gate, w_ffn_up, w_ffn_down
    wo = w_out.astype(BF16)
    w_in_t = jnp.swapaxes(w_in, 1, 2)
    col = jnp.arange(Z_WIDTH)
    colscale = jnp.where((col >= COL_QA) & (col < COL_QA + ATT_WIDTH), ATT_HEAD_DIM ** -0.5, 1.0).astype(F32)[None]
    wa_p = jnp.pad(w_gate_up, ((0, 0), (0, LR_PAD - GLA_RANK), (0, 0))).astype(BF16)
    ba = b_gate.reshape(depth, 1, GLA_KWIDTH)
    g_gla = gla_norm.reshape(depth, 1, GLA_DV)
    gains = norm_gains.reshape(depth, 6, 1, D_MODEL)
    kbuf = cache_k_win.reshape(depth, dec_batch, wb, KV_WIDTH)
    vbuf = cache_v_win.reshape(depth, dec_batch, wb, KV_WIDTH)
    tail = jnp.zeros((m_pad - mp - ms, GLA_WIDTH), F32)

    pk, pv, ps, sk, sv, ss = [], [], [], [], [], []
    for l in range(depth):
        x = _ffn(x, gains, wg, wu, wd, l, 0)

        z, la = _proj(x, gains, w_in_t, colscale, wa_p, ba, l)
        att = _attn_prompt(z, attn_sinks[l], batch, nb, m_pad)
        go, st_p = _gla_prompt(z, la, batch, lp, m_pad)

        zs = z[mp:mp + ms].reshape(dec_batch, dec_seq, Z_WIDTH)
        zs8 = jnp.pad(zs, ((0, 0), (0, rows8 - dec_seq), (0, 0)))
        la8 = jnp.pad(la[mp:mp + ms].reshape(dec_batch, dec_seq, GLA_KWIDTH), ((0, 0), (0, rows8 - dec_seq), (0, 0)))
        qa_s = zs[:, :, COL_QA:COL_QA + ATT_WIDTH].reshape(dec_batch, dec_seq, ATT_KV_HEADS, ATT_GROUP, ATT_HEAD_DIM)
        qa_s = qa_s.transpose(0, 2, 1, 3, 4).reshape(dec_batch, ATT_KV_HEADS, dec_seq * ATT_GROUP, ATT_HEAD_DIM)
        sink_rows = jnp.tile(attn_sinks[l].reshape(ATT_KV_HEADS, 1, ATT_GROUP), (1, dec_seq, 1))
        sink_rows = sink_rows.reshape(ATT_KV_HEADS, dec_seq * ATT_GROUP, 1)
        att_s, go_s, st_s = _sample_mixer(qa_s, zs8, la8, kbuf, vbuf, sink_rows, state_gla, dec_seq, l)
        att_s = att_s.reshape(dec_batch, ATT_KV_HEADS, dec_seq, ATT_GROUP, ATT_HEAD_DIM)
        att_s = att_s.transpose(0, 2, 1, 3, 4).reshape(ms, ATT_WIDTH)

        att = lax.dynamic_update_slice(att, jnp.concatenate([att_s, tail], axis=0).astype(BF16), (mp, 0))
        go = lax.dynamic_update_slice(go, jnp.concatenate([go_s[:, :dec_seq].reshape(ms, GLA_WIDTH), tail], axis=0), (mp, 0))
        x = _merge(x, att, go, z, g_gla, wo, gains, l)

        win = [z[b * lp + lp - WINDOW:(b + 1) * lp, COL_KA:COL_VA + KV_WIDTH] for b in range(batch)]
        win = jnp.stack(win).reshape(batch, WINDOW, 2, ATT_KV_HEADS, ATT_HEAD_DIM)
        pk.append(win[:, :, 0])
        pv.append(win[:, :, 1])
        ps.append(st_p.transpose(0, 1, 3, 2))
        kn = zs[:, :, COL_KA:COL_KA + KV_WIDTH].reshape(dec_batch, dec_seq, ATT_KV_HEADS, ATT_HEAD_DIM)
        vn = zs[:, :, COL_VA:COL_VA + KV_WIDTH].reshape(dec_batch, dec_seq, ATT_KV_HEADS, ATT_HEAD_DIM)
        sk.append(jnp.concatenate([cache_k_win[l], kn], axis=1)[:, -wb:])
        sv.append(jnp.concatenate([cache_v_win[l], vn], axis=1)[:, -wb:])
        ss.append(st_s)

        x = _ffn(x, gains, wg, wu, wd, l, 1)

    y_prompt = jnp.stack([x[b * lp + BLOCK:(b + 1) * lp] for b in range(batch)])
    y_sample = x[mp:mp + ms].reshape(dec_batch, dec_seq, D_MODEL)
    return (y_prompt, y_sample, jnp.stack(pk), jnp.stack(pv), jnp.stack(ps),
            jnp.stack(sk), jnp.stack(sv), jnp.stack(ss))
```

```python
import functools

import jax
import jax.numpy as jnp
from jax import lax
from jax.experimental import pallas as pl
from jax.experimental.pallas import tpu as pltpu

F32 = jnp.float32
BF16 = jnp.bfloat16

D_MODEL = 2048
D_FF = 5632
N_META = 16
BLOCK = 128
WINDOW = 128
ATT_HEADS = 16
ATT_KV_HEADS = 2
ATT_GROUP = 8
ATT_HEAD_DIM = 64
ATT_WIDTH = 1024
KV_WIDTH = 128
GLA_HEADS = 4
GLA_DK = 128
GLA_DV = 256
GLA_KWIDTH = 512
GLA_WIDTH = 1024
GLA_RANK = 16
GLA_GATE_NORM = 16.0
GLA_CHUNK = 64
GLA_SUB = 16
EPS = 1e-6
LOG2_E = 1.4426950408889634
PAD = BLOCK - N_META

COL_QA, COL_KA, COL_VA, COL_QG, COL_KG, COL_VG, COL_RG, COL_LR = 0, 1024, 1152, 1280, 1792, 2304, 3328, 4352
COL_TILE = 256
Z_WIDTH = 4608
LR_PAD = 128

VMEM_LIMIT = 56 * 1024 * 1024
VMEM_LIMIT_FFN = 60 * 1024 * 1024

TM = 1072
TM_MERGE = 536
TF = 256
TN = 768


def _rms(x, gain):
    ms = jnp.mean(x * x, axis=-1, keepdims=True)
    return x * lax.rsqrt(ms + EPS) * gain


def _dot(a, b):
    return jnp.dot(a, b, preferred_element_type=F32)


def _dot_nt(a, b):
    return lax.dot_general(a, b, (((1,), (1,)), ((), ())), preferred_element_type=F32)


def _dot_tn(a, b):
    return lax.dot_general(a, b, (((0,), (0,)), ((), ())), preferred_element_type=F32)


def _ffn_kernel(x_ref, gpre_ref, gpost_ref, wg_hbm, wu_hbm, wd_hbm, o_ref, xn_ref, wg_buf, wu_buf, wd_buf, sem,
                *, l, f):
    i = pl.program_id(0)
    n_chunks = D_FF // TF
    total = pl.num_programs(0) * n_chunks

    def copies(j, slot):
        cols = pl.ds(j * TF, TF)
        return (pltpu.make_async_copy(wg_hbm.at[l, f, :, cols], wg_buf.at[slot], sem.at[0, slot]),
                pltpu.make_async_copy(wu_hbm.at[l, f, :, cols], wu_buf.at[slot], sem.at[1, slot]),
                pltpu.make_async_copy(wd_hbm.at[l, f, cols, :], wd_buf.at[slot], sem.at[2, slot]))

    @pl.when(i == 0)
    def _():
        for c in copies(0, 0):
            c.start()

    xn_ref[...] = _rms(x_ref[...], gpre_ref[...]).astype(BF16)
    o_ref[...] = jnp.zeros_like(o_ref)

    def chunk(j, carry):
        g_idx = i * n_chunks + j
        slot = lax.rem(g_idx, 2)
        for c in copies(j, slot):
            c.wait()

        @pl.when(g_idx + 1 < total)
        def _():
            for c in copies(lax.rem(j + 1, n_chunks), 1 - slot):
                c.start()

        xn = xn_ref[...]
        g = _dot(xn, wg_buf[slot].astype(BF16))
        u = _dot(xn, wu_buf[slot].astype(BF16))
        h = (g * jax.nn.sigmoid(g) * u).astype(BF16)
        o_ref[...] += _dot(h, wd_buf[slot].astype(BF16))
        return carry

    lax.fori_loop(0, n_chunks, chunk, 0)
    o_ref[...] = x_ref[...] + 0.5 * _rms(o_ref[...], gpost_ref[...])


def _gain_spec(l, idx):
    return pl.BlockSpec((None, None, 1, D_MODEL), lambda *_: (l, idx, 0, 0))


def _ffn(x, gains, wg, wu, wd, l, f):
    m = x.shape[0]
    return pl.pallas_call(
        functools.partial(_ffn_kernel, l=l, f=f),
        grid=(m // TM,),
        in_specs=[
            pl.BlockSpec((TM, D_MODEL), lambda i: (i, 0), pipeline_mode=pl.Buffered(1)),
            _gain_spec(l, 4 * f),
            _gain_spec(l, 4 * f + 1),
            pl.BlockSpec(memory_space=pl.ANY),
            pl.BlockSpec(memory_space=pl.ANY),
            pl.BlockSpec(memory_space=pl.ANY),
        ],
        out_specs=pl.BlockSpec((TM, D_MODEL), lambda i: (i, 0)),
        out_shape=jax.ShapeDtypeStruct((m, D_MODEL), F32),
        scratch_shapes=[
            pltpu.VMEM((TM, D_MODEL), BF16),
            pltpu.VMEM((2, D_MODEL, TF), F32),
            pltpu.VMEM((2, D_MODEL, TF), F32),
            pltpu.VMEM((2, TF, D_MODEL), F32),
            pltpu.SemaphoreType.DMA((3, 2)),
        ],
        compiler_params=pltpu.CompilerParams(
            dimension_semantics=("arbitrary",), vmem_limit_bytes=VMEM_LIMIT_FFN),
        name="ffn",
    )(x, gains, gains, wg, wu, wd)


def _proj_kernel(x_ref, g_ref, wt_ref, cs_ref, wa_ref, ba_ref, z_ref, la_ref, xn_ref, *, n_cols):
    j = pl.program_id(1)
    last = pl.num_programs(1) - 1
    n_valid = n_cols - (Z_WIDTH - TN)

    @pl.when(j == 0)
    def _():
        xn_ref[...] = _rms(x_ref[...], g_ref[...]).astype(BF16)

    @pl.when(j < last)
    def _():
        z_ref[...] = _dot_nt(xn_ref[...], wt_ref[...].astype(BF16)) * cs_ref[...]

    @pl.when(j == last)
    def _():
        z_ref[:, :n_valid] = _dot_nt(xn_ref[...], wt_ref[:n_valid, :].astype(BF16)) * cs_ref[:, :n_valid]
        z_ref[:, n_valid:] = jnp.zeros((z_ref.shape[0], TN - n_valid), F32)
        lr_off = COL_LR - (Z_WIDTH - TN)
        lr = z_ref[:, lr_off:lr_off + LR_PAD].astype(BF16)
        logit = _dot(lr, wa_ref[...]) + ba_ref[...]
        log_sig = jnp.minimum(logit, 0.0) - jnp.log1p(jnp.exp(-jnp.abs(logit)))
        la_ref[...] = log_sig * (1.0 / GLA_GATE_NORM)


def _proj(x, gains, w_in_t, colscale, wa_p, ba, l):
    m = x.shape[0]
    n_cols = w_in_t.shape[1]
    assert Z_WIDTH - TN < COL_LR and COL_LR + GLA_RANK <= n_cols <= Z_WIDTH
    return pl.pallas_call(
        functools.partial(_proj_kernel, n_cols=n_cols),
        grid=(m // TM, Z_WIDTH // TN),
        in_specs=[
            pl.BlockSpec((TM, D_MODEL), lambda i, j: (i, 0)),
            _gain_spec(l, 2),
            pl.BlockSpec((None, TN, D_MODEL), lambda i, j: (l, j, 0)),
            pl.BlockSpec((1, TN), lambda i, j: (0, j)),
            pl.BlockSpec((None, LR_PAD, GLA_KWIDTH), lambda i, j: (l, 0, 0)),
            pl.BlockSpec((None, 1, GLA_KWIDTH), lambda i, j: (l, 0, 0)),
        ],
        out_specs=[
            pl.BlockSpec((TM, TN), lambda i, j: (i, j)),
            pl.BlockSpec((TM, GLA_KWIDTH), lambda i, j: (i, 0)),
        ],
        out_shape=[
            jax.ShapeDtypeStruct((m, Z_WIDTH), F32),
            jax.ShapeDtypeStruct((m, GLA_KWIDTH), F32),
        ],
        scratch_shapes=[pltpu.VMEM((TM, D_MODEL), BF16)],
        compiler_params=pltpu.CompilerParams(
            dimension_semantics=("parallel", "arbitrary"), vmem_limit_bytes=VMEM_LIMIT),
        name="proj",
    )(x, gains, w_in_t, colscale, wa_p, ba)


def _pair_blockdiag(x128, kv):
    lane = lax.broadcasted_iota(jnp.int32, x128.shape, 1)
    own = jnp.where((lane >= kv * ATT_HEAD_DIM) & (lane < (kv + 1) * ATT_HEAD_DIM), x128, 0.0)
    other = pltpu.roll(own, ATT_HEAD_DIM, axis=1)
    lo, hi = (own, other) if kv == 0 else (other, own)
    return jnp.concatenate([lo, hi], axis=0).astype(BF16)


def _attn_block(i, sink_ref, q_ref, kc_ref, kp_ref, vc_ref, vp_ref, o_ref):
    kb = 2 * BLOCK
    row = lax.broadcasted_iota(jnp.int32, (BLOCK, kb), 0)
    col = lax.broadcasted_iota(jnp.int32, (BLOCK, kb), 1)
    diff = row + BLOCK - col
    key_pos = col + (i - 1) * BLOCK
    mask = (diff >= 0) & (diff <= WINDOW) & (key_pos >= PAD)
    lane = lax.broadcasted_iota(jnp.int32, (BLOCK, 2 * ATT_HEAD_DIM), 1)
    kk = jnp.concatenate([kp_ref[...], kc_ref[...]], axis=0)
    vv = jnp.concatenate([vp_ref[...], vc_ref[...]], axis=0)
    slabs = ATT_GROUP // 2
    for kv in range(ATT_KV_HEADS):
        k2 = _pair_blockdiag(kk, kv)
        v2 = _pair_blockdiag(vv, kv)
        c0 = kv * slabs * 2 * ATT_HEAD_DIM
        q4 = q_ref[:, c0:c0 + slabs * 2 * ATT_HEAD_DIM]
        q4 = jnp.concatenate([q4[:, p * 128:(p + 1) * 128] for p in range(slabs)], axis=0).astype(BF16)
        s = _dot_nt(q4, k2)
        probs, inv = [], []
        for p in range(slabs):
            halves, rden = [], []
            for e in range(2):
                sink = sink_ref[kv * ATT_GROUP + 2 * p + e]
                sp = jnp.where(mask, s[p * BLOCK:(p + 1) * BLOCK, e * kb:(e + 1) * kb], -jnp.inf)
                mx = jnp.maximum(jnp.max(sp, axis=-1, keepdims=True), sink)
                pe = jnp.exp(sp - mx)
                rden.append(1.0 / (jnp.sum(pe, axis=-1, keepdims=True) + jnp.exp(sink - mx)))
                halves.append(pe.astype(BF16))
            probs.append(jnp.concatenate(halves, axis=1))
            inv.append(jnp.where(lane < ATT_HEAD_DIM, rden[0], rden[1]))
        o = _dot(jnp.concatenate(probs, axis=0), v2)
        for p in range(slabs):
            o_ref[:, c0 + p * 128:c0 + (p + 1) * 128] = (o[p * BLOCK:(p + 1) * BLOCK] * inv[p]).astype(o_ref.dtype)


def _attn_kernel(sink_ref, q_ref, kc_ref, kp_ref, vc_ref, vp_ref, o_ref, *, nb, n_blocks):
    step = pl.program_id(0)

    @pl.when(step < n_blocks)
    def _():
        _attn_block(lax.rem(step, nb), sink_ref, q_ref, kc_ref, kp_ref, vc_ref, vp_ref, o_ref)

    @pl.when(step >= n_blocks)
    def _():
        o_ref[...] = jnp.zeros_like(o_ref)


def _attn_prompt(z, sinks, batch, nb, m_rows):
    n_blocks = batch * nb

    def cur(c):
        return lambda s, sink: (jnp.minimum(s, n_blocks - 1), c)

    def prev(c):
        return lambda s, sink: (jnp.maximum(jnp.minimum(s, n_blocks - 1) - 1, 0), c)

    return pl.pallas_call(
        functools.partial(_attn_kernel, nb=nb, n_blocks=n_blocks),
        grid_spec=pltpu.PrefetchScalarGridSpec(
            num_scalar_prefetch=1,
            grid=(m_rows // BLOCK,),
            in_specs=[
                pl.BlockSpec((BLOCK, ATT_WIDTH), cur(COL_QA // ATT_WIDTH)),
                pl.BlockSpec((BLOCK, KV_WIDTH), cur(COL_KA // KV_WIDTH)),
                pl.BlockSpec((BLOCK, KV_WIDTH), prev(COL_KA // KV_WIDTH)),
                pl.BlockSpec((BLOCK, KV_WIDTH), cur(COL_VA // KV_WIDTH)),
                pl.BlockSpec((BLOCK, KV_WIDTH), prev(COL_VA // KV_WIDTH)),
            ],
            out_specs=pl.BlockSpec((BLOCK, ATT_WIDTH), lambda s, sink: (s, 0)),
        ),
        out_shape=jax.ShapeDtypeStruct((m_rows, ATT_WIDTH), BF16),
        compiler_params=pltpu.CompilerParams(
            dimension_semantics=("arbitrary",), vmem_limit_bytes=VMEM_LIMIT),
        name="attn_prompt",
    )(sinks, z, z, z, z, z)


def _cumsum_rows(la, tri):
    hi = la.astype(BF16)
    r1 = la - hi.astype(F32)
    mid = r1.astype(BF16)
    lo = (r1 - mid.astype(F32)).astype(BF16)
    return _dot(tri, hi) + _dot(tri, mid) + _dot(tri, lo)


def _gla_chunk(c, nc, refs):
    nq = GLA_KWIDTH // COL_TILE
    q_refs, k_refs, v_refs = refs[:nq], refs[nq:2 * nq], refs[2 * nq:2 * nq + GLA_HEADS]
    la_ref, o_ref, st_ref, state_ref = refs[2 * nq + GLA_HEADS:]
    per = COL_TILE // GLA_DK
    C = GLA_CHUNK

    @pl.when(c == 0)
    def _():
        state_ref[...] = jnp.zeros_like(state_ref)

    rowc = lax.broadcasted_iota(jnp.int32, (C, C), 0)
    colc = lax.broadcasted_iota(jnp.int32, (C, C), 1)
    tri = (rowc >= colc).astype(BF16)
    b_all = _cumsum_rows(la_ref[...], tri) * LOG2_E

    pos = c * C + lax.broadcasted_iota(jnp.int32, (C, 1), 0)
    valid = (pos >= PAD).astype(F32)

    lane = lax.broadcasted_iota(jnp.int32, (GLA_SUB, C), 1)
    rsub = lax.broadcasted_iota(jnp.int32, (GLA_SUB, C), 0)
    scale = GLA_DK ** -0.5

    for h in range(GLA_HEADS):
        ks = slice(h * GLA_DK, (h + 1) * GLA_DK)
        vs = slice(h * GLA_DV, (h + 1) * GLA_DV)
        sub = slice((h % per) * GLA_DK, (h % per + 1) * GLA_DK)
        q = q_refs[h // per][:, sub] * scale
        k = k_refs[h // per][:, sub] * valid
        v = v_refs[h][...].astype(BF16)
        bh = b_all[:, ks]
        bl = bh[C - 1:C, :]
        st = state_ref[h]

        o = _dot_nt((q * jnp.exp2(bh)).astype(BF16), st.astype(BF16))

        blocks = []
        for i in range(C // GLA_SUB):
            lo_r = i * GLA_SUB
            qb = q[lo_r:lo_r + GLA_SUB]
            bb = bh[lo_r:lo_r + GLA_SUB]
            w = jnp.zeros((GLA_SUB, C), F32)
            for s in range(GLA_SUB):
                r = lo_r + s
                zed = qb * k[r:r + 1] * jnp.exp2(bb - bh[r:r + 1])
                w = jnp.where(lane == r, jnp.sum(zed, axis=-1, keepdims=True), w)
            w = jnp.where(lane <= rsub + lo_r, w, 0.0)
            if i > 0:
                ref_b = bh[lo_r - 1:lo_r]
                qi = (qb * jnp.exp2(bb - ref_b)).astype(BF16)
                kj = (k * jnp.exp2(jnp.minimum(ref_b - bh, 0.0))).astype(BF16)
                w = jnp.where(lane < lo_r, _dot_nt(qi, kj), w)
            blocks.append(w)
        a = jnp.concatenate(blocks, axis=0).astype(BF16)
        o_ref[:, vs] = o + _dot(a, v)

        k_out = (k * jnp.exp2(bl - bh)).astype(BF16)
        state_ref[h] = st * jnp.exp2(bl) + _dot_tn(v, k_out)

    @pl.when(c == nc - 1)
    def _():
        st_ref[...] = state_ref[...]


def _gla_kernel(*refs, nc, n_chunks):
    step = pl.program_id(0)

    @pl.when(step < n_chunks)
    def _():
        _gla_chunk(lax.rem(step, nc), nc, refs)

    @pl.when(step >= n_chunks)
    def _():
        o_ref = refs[-3]
        o_ref[...] = jnp.zeros_like(o_ref)


def _gla_prompt(z, la, batch, seq, m_rows):
    nc = seq // GLA_CHUNK
    n_chunks = batch * nc
    nq = GLA_KWIDTH // COL_TILE

    def rows(cidx):
        return lambda s: (s, cidx)

    return pl.pallas_call(
        functools.partial(_gla_kernel, nc=nc, n_chunks=n_chunks),
        grid=(m_rows // GLA_CHUNK,),
        in_specs=(
            [pl.BlockSpec((GLA_CHUNK, COL_TILE), rows(COL_QG // COL_TILE + t)) for t in range(nq)]
            + [pl.BlockSpec((GLA_CHUNK, COL_TILE), rows(COL_KG // COL_TILE + t)) for t in range(nq)]
            + [pl.BlockSpec((GLA_CHUNK, GLA_DV), rows(COL_VG // GLA_DV + t)) for t in range(GLA_HEADS)]
            + [pl.BlockSpec((GLA_CHUNK, GLA_KWIDTH), rows(0))]
        ),
        out_specs=[
            pl.BlockSpec((GLA_CHUNK, GLA_WIDTH), rows(0)),
            pl.BlockSpec((None, GLA_HEADS, GLA_DV, GLA_DK), lambda s: (jnp.minimum(s // nc, batch - 1), 0, 0, 0)),
        ],
        out_shape=[
            jax.ShapeDtypeStruct((m_rows, GLA_WIDTH), F32),
            jax.ShapeDtypeStruct((batch, GLA_HEADS, GLA_DV, GLA_DK), F32),
        ],
        scratch_shapes=[pltpu.VMEM((GLA_HEADS, GLA_DV, GLA_DK), F32)],
        compiler_params=pltpu.CompilerParams(
            dimension_semantics=("arbitrary",), vmem_limit_bytes=VMEM_LIMIT),
        name="gla_prompt",
    )(*([z] * (2 * nq + GLA_HEADS)), la)


def _sample_kernel(qa_ref, zs_ref, la_ref, kbuf_ref, vbuf_ref, sink_ref, s0_ref,
                   att_ref, go_ref, st_ref, *, dec_seq):
    S = dec_seq
    R = S * ATT_GROUP
    t_row = lax.broadcasted_iota(jnp.int32, (R, 1), 0) // ATT_GROUP
    c_idx = lax.broadcasted_iota(jnp.int32, (R, kbuf_ref.shape[0]), 1)
    mask_c = c_idx >= t_row
    for kv in range(ATT_KV_HEADS):
        ks = slice(kv * ATT_HEAD_DIM, (kv + 1) * ATT_HEAD_DIM)
        q = qa_ref[kv]
        sink = sink_ref[kv]
        sc = _dot_nt(q.astype(BF16), kbuf_ref[:, ks].astype(BF16))
        sc = jnp.where(mask_c, sc, -jnp.inf)
        mx = jnp.maximum(jnp.max(sc, axis=-1, keepdims=True), sink)
        kn = zs_ref[:, COL_KA + kv * ATT_HEAD_DIM:COL_KA + (kv + 1) * ATT_HEAD_DIM]
        vn = zs_ref[:, COL_VA + kv * ATT_HEAD_DIM:COL_VA + (kv + 1) * ATT_HEAD_DIM]
        sn = []
        for s in range(S):
            v = jnp.sum(q * kn[s:s + 1], axis=-1, keepdims=True)
            v = jnp.where(t_row >= s, v, -jnp.inf)
            sn.append(v)
            mx = jnp.maximum(mx, v)
        pc = jnp.exp(sc - mx)
        den = jnp.sum(pc, axis=-1, keepdims=True) + jnp.exp(sink - mx)
        o = _dot(pc.astype(BF16), vbuf_ref[:, ks].astype(BF16))
        for s in range(S):
            pn = jnp.exp(sn[s] - mx)
            den = den + pn
            o = o + pn * vn[s:s + 1]
        att_ref[kv] = o / den

    rows8 = zs_ref.shape[0]
    rid = lax.broadcasted_iota(jnp.int32, (rows8, 1), 0)
    la = la_ref[...]
    run = la[0:1]
    b_all = jnp.where(rid == 0, run, 0.0)
    for t in range(1, S):
        run = run + la[t:t + 1]
        b_all = jnp.where(rid == t, run, b_all)
    real = (rid < S).astype(F32)
    scale = GLA_DK ** -0.5
    for h in range(GLA_HEADS):
        ks = slice(h * GLA_DK, (h + 1) * GLA_DK)
        q = zs_ref[:, COL_QG + h * GLA_DK:COL_QG + (h + 1) * GLA_DK] * scale
        k = zs_ref[:, COL_KG + h * GLA_DK:COL_KG + (h + 1) * GLA_DK]
        v = zs_ref[:, COL_VG + h * GLA_DV:COL_VG + (h + 1) * GLA_DV]
        bh = b_all[:, ks] * real
        bl = bh[S - 1:S]
        s0 = s0_ref[h]
        k_out = k * jnp.exp(jnp.minimum(bl - bh, 0.0)) * real
        q_in = q * jnp.exp(bh) * real
        o = _dot(q_in.astype(BF16), s0.astype(BF16))
        for s in range(S):
            zed = q * k[s:s + 1] * jnp.exp(jnp.minimum(bh - bh[s:s + 1], 0.0))
            w = jnp.sum(zed, axis=-1, keepdims=True)
            w = jnp.where((rid >= s) & (rid < S), w, 0.0)
            o = o + w * v[s:s + 1]
        go_ref[:, h * GLA_DV:(h + 1) * GLA_DV] = o
        e_rows = jnp.broadcast_to(jnp.exp(bl), (GLA_DK, GLA_DK)).T
        decay = jnp.concatenate([e_rows] * (GLA_DV // GLA_DK), axis=1)
        st_ref[h] = decay * s0 + _dot_tn(k_out.astype(BF16), v.astype(BF16))


def _sample_mixer(qa_s, zs8, la8, kbuf, vbuf, sink_rows, s0, dec_seq, l):
    nb = zs8.shape[0]
    rows8 = zs8.shape[1]
    R = dec_seq * ATT_GROUP
    wb = kbuf.shape[2]
    return pl.pallas_call(
        functools.partial(_sample_kernel, dec_seq=dec_seq),
        grid=(nb,),
        in_specs=[
            pl.BlockSpec((None, ATT_KV_HEADS, R, ATT_HEAD_DIM), lambda b: (b, 0, 0, 0)),
            pl.BlockSpec((None, rows8, Z_WIDTH), lambda b: (b, 0, 0)),
            pl.BlockSpec((None, rows8, GLA_KWIDTH), lambda b: (b, 0, 0)),
            pl.BlockSpec((None, None, wb, KV_WIDTH), lambda b: (l, b, 0, 0)),
            pl.BlockSpec((None, None, wb, KV_WIDTH), lambda b: (l, b, 0, 0)),
            pl.BlockSpec((ATT_KV_HEADS, R, 1), lambda b: (0, 0, 0)),
            pl.BlockSpec((None, None, GLA_HEADS, GLA_DK, GLA_DV), lambda b: (l, b, 0, 0, 0)),
        ],
        out_specs=[
            pl.BlockSpec((None, ATT_KV_HEADS, R, ATT_HEAD_DIM), lambda b: (b, 0, 0, 0)),
            pl.BlockSpec((None, rows8, GLA_WIDTH), lambda b: (b, 0, 0)),
            pl.BlockSpec((None, GLA_HEADS, GLA_DK, GLA_DV), lambda b: (b, 0, 0, 0)),
        ],
        out_shape=[
            jax.ShapeDtypeStruct((nb, ATT_KV_HEADS, R, ATT_HEAD_DIM), F32),
            jax.ShapeDtypeStruct((nb, rows8, GLA_WIDTH), F32),
            jax.ShapeDtypeStruct((nb, GLA_HEADS, GLA_DK, GLA_DV), F32),
        ],
        compiler_params=pltpu.CompilerParams(
            dimension_semantics=("parallel",), vmem_limit_bytes=VMEM_LIMIT),
        name="sample_mixer",
    )(qa_s, zs8, la8, kbuf, vbuf, sink_rows, s0)


def _merge_kernel(*refs):
    x_ref, att_ref, go_ref = refs[:3]
    rg_refs = refs[3:3 + GLA_HEADS]
    ggla_ref, wo_ref, gpost_ref, o_ref = refs[3 + GLA_HEADS:]
    parts = []
    for h in range(GLA_HEADS):
        vs = slice(h * GLA_DV, (h + 1) * GLA_DV)
        rg = rg_refs[h][...]
        parts.append((_rms(go_ref[:, vs], ggla_ref[...]) * (rg * jax.nn.sigmoid(rg))).astype(BF16))
    go = jnp.concatenate(parts, axis=-1)
    y = _dot(att_ref[...], wo_ref[:ATT_WIDTH, :]) + _dot(go, wo_ref[ATT_WIDTH:, :])
    o_ref[...] = x_ref[...] + _rms(y, gpost_ref[...])


def _merge(x, att, go, z, g_gla, w_out, gains, l):
    m = x.shape[0]
    return pl.pallas_call(
        _merge_kernel,
        grid=(m // TM_MERGE,),
        in_specs=(
            [pl.BlockSpec((TM_MERGE, D_MODEL), lambda i: (i, 0)),
             pl.BlockSpec((TM_MERGE, ATT_WIDTH), lambda i: (i, 0)),
             pl.BlockSpec((TM_MERGE, GLA_WIDTH), lambda i: (i, 0))]
            + [pl.BlockSpec((TM_MERGE, GLA_DV), lambda i, h=h: (i, COL_RG // GLA_DV + h)) for h in range(GLA_HEADS)]
            + [pl.BlockSpec((None, 1, GLA_DV), lambda i: (l, 0, 0)),
               pl.BlockSpec((None, ATT_WIDTH + GLA_WIDTH, D_MODEL), lambda i: (l, 0, 0)),
               _gain_spec(l, 3)]
        ),
        out_specs=pl.BlockSpec((TM_MERGE, D_MODEL), lambda i: (i, 0)),
        out_shape=jax.ShapeDtypeStruct((m, D_MODEL), F32),
        compiler_params=pltpu.CompilerParams(
            dimension_semantics=("parallel",), vmem_limit_bytes=VMEM_LIMIT),
        name="merge",
    )(x, att, go, *([z] * GLA_HEADS), g_gla, w_out, gains)


def kernel(x_prompt, x_sample, cache_k_win, cache_v_win, state_gla, meta_tokens, norm_gains,
           w_ffn_gate, w_ffn_up, w_ffn_down, w_in, w_gate_up, b_gate, attn_sinks, gla_norm, w_out):
    batch, seq, _ = x_prompt.shape
    dec_batch, dec_seq, _ = x_sample.shape
    depth = norm_gains.shape[0]
    lp = seq + BLOCK
    nb = lp // BLOCK
    mp = batch * lp
    ms = dec_batch * dec_seq
    m_pad = -(-(mp + ms) // TM) * TM
    wb = cache_k_win.shape[2]
    rows8 = 8
    assert dec_seq <= rows8 and lp % GLA_CHUNK == 0

    head = jnp.concatenate([jnp.zeros((PAD, D_MODEL), F32), meta_tokens.astype(F32)], axis=0)
    pieces = []
    for b in range(batch):
        pieces += [head, x_prompt[b]]
    pieces += [x_sample.reshape(ms, D_MODEL), jnp.zeros((m_pad - mp - ms, D_MODEL), F32)]
    x = jnp.concatenate(pieces, axis=0)

    wg, wu, wd = w_ffn_gate, w_ffn_up, w_ffn_down
    wo = w_out.astype(BF16)
    w_in_t = jnp.swapaxes(w_in, 1, 2)
    col = jnp.arange(Z_WIDTH)
    colscale = jnp.where((col >= COL_QA) & (col < COL_QA + ATT_WIDTH), ATT_HEAD_DIM ** -0.5, 1.0).astype(F32)[None]
    wa_p = jnp.pad(w_gate_up, ((0, 0), (0, LR_PAD - GLA_RANK), (0, 0))).astype(BF16)
    ba = b_gate.reshape(depth, 1, GLA_KWIDTH)
    g_gla = gla_norm.reshape(depth, 1, GLA_DV)
    gains = norm_gains.reshape(depth, 6, 1, D_MODEL)
    kbuf = cache_k_win.reshape(depth, dec_batch, wb, KV_WIDTH)
    vbuf = cache_v_win.reshape(depth, dec_batch, wb, KV_WIDTH)
    tail = jnp.zeros((m_pad - mp - ms, GLA_WIDTH), F32)

    pk, pv, ps, sk, sv, ss = [], [], [], [], [], []
    for l in range(depth):
        x = _ffn(x, gains, wg, wu, wd, l, 0)

        z, la = _proj(x, gains, w_in_t, colscale, wa_p, ba, l)
        att = _attn_prompt(z, attn_sinks[l], batch, nb, m_pad)
        go, st_p = _gla_prompt(z, la, batch, lp, m_pad)

        zs = z[mp:mp + ms].reshape(dec_batch, dec_seq, Z_WIDTH)
        zs8 = jnp.pad(zs, ((0, 0), (0, rows8 - dec_seq), (0, 0)))
        la8 = jnp.pad(la[mp:mp + ms].reshape(dec_batch, dec_seq, GLA_KWIDTH), ((0, 0), (0, rows8 - dec_seq), (0, 0)))
        qa_s = zs[:, :, COL_QA:COL_QA + ATT_WIDTH].reshape(dec_batch, dec_seq, ATT_KV_HEADS, ATT_GROUP, ATT_HEAD_DIM)
        qa_s = qa_s.transpose(0, 2, 1, 3, 4).reshape(dec_batch, ATT_KV_HEADS, dec_seq * ATT_GROUP, ATT_HEAD_DIM)
        sink_rows = jnp.tile(attn_sinks[l].reshape(ATT_KV_HEADS, 1, ATT_GROUP), (1, dec_seq, 1))
        sink_rows = sink_rows.reshape(ATT_KV_HEADS, dec_seq * ATT_GROUP, 1)
        att_s, go_s, st_s = _sample_mixer(qa_s, zs8, la8, kbuf, vbuf, sink_rows, state_gla, dec_seq, l)
        att_s = att_s.reshape(dec_batch, ATT_KV_HEADS, dec_seq, ATT_GROUP, ATT_HEAD_DIM)
        att_s = att_s.transpose(0, 2, 1, 3, 4).reshape(ms, ATT_WIDTH)

        att = lax.dynamic_update_slice(att, jnp.concatenate([att_s, tail], axis=0).astype(BF16), (mp, 0))
        go = lax.dynamic_update_slice(go, jnp.concatenate([go_s[:, :dec_seq].reshape(ms, GLA_WIDTH), tail], axis=0), (mp, 0))
        x = _merge(x, att, go, z, g_gla, wo, gains, l)

        win = [z[b * lp + lp - WINDOW:(b + 1) * lp, COL_KA:COL_VA + KV_WIDTH] for b in range(batch)]
        win = jnp.stack(win).reshape(batch, WINDOW, 2, ATT_KV_HEADS, ATT_HEAD_DIM)
        pk.append(win[:, :, 0])
        pv.append(win[:, :, 1])
        ps.append(st_p.transpose(0, 1, 3, 2))
        kn = zs[:, :, COL_KA:COL_KA + KV_WIDTH].reshape(dec_batch, dec_seq, ATT_KV_HEADS, ATT_HEAD_DIM)
        vn = zs[:, :, COL_VA:COL_VA + KV_WIDTH].reshape(dec_batch, dec_seq, ATT_KV_HEADS, ATT_HEAD_DIM)
        sk.append(jnp.concatenate([cache_k_win[l], kn], axis=1)[:, -wb:])
        sv.append(jnp.concatenate([cache_v_win[l], vn], axis=1)[:, -wb:])
        ss.append(st_s)

        x = _ffn(x, gains, wg, wu, wd, l, 1)

    y_prompt = jnp.stack([x[b * lp + BLOCK:(b + 1) * lp] for b in range(batch)])
    y_sample = x[mp:mp + ms].reshape(dec_batch, dec_seq, D_MODEL)
    return (y_prompt, y_sample, jnp.stack(pk), jnp.stack(pv), jnp.stack(ps),
            jnp.stack(sk), jnp.stack(sv), jnp.stack(ss))
```

```python
import functools

import jax
import jax.numpy as jnp
from jax import lax
from jax.experimental import pallas as pl
from jax.experimental.pallas import tpu as pltpu

F32 = jnp.float32
BF16 = jnp.bfloat16

D_MODEL = 2048
D_FF = 5632
N_META = 16
BLOCK = 128
WINDOW = 128
ATT_HEADS = 16
ATT_KV_HEADS = 2
ATT_GROUP = 8
ATT_HEAD_DIM = 64
ATT_WIDTH = 1024
KV_WIDTH = 128
GLA_HEADS = 4
GLA_DK = 128
GLA_DV = 256
GLA_KWIDTH = 512
GLA_WIDTH = 1024
GLA_RANK = 16
GLA_GATE_NORM = 16.0
GLA_CHUNK = 64
GLA_SUB = 16
EPS = 1e-6
LOG2_E = 1.4426950408889634
PAD = BLOCK - N_META

COL_QA, COL_KA, COL_VA, COL_QG, COL_KG, COL_VG, COL_RG, COL_LR = 0, 1024, 1152, 1280, 1792, 2304, 3328, 4352
COL_TILE = 256
Z_WIDTH = 4608
LR_PAD = 128

VMEM_LIMIT = 56 * 1024 * 1024
VMEM_LIMIT_FFN = 60 * 1024 * 1024

TM = 1072
TM_MERGE = 536
TF = 256
TN = 768


def _rms(x, gain):
    ms = jnp.mean(x * x, axis=-1, keepdims=True)
    return x * lax.rsqrt(ms + EPS) * gain


def _dot(a, b):
    return jnp.dot(a, b, preferred_element_type=F32)


def _dot_nt(a, b):
    return lax.dot_general(a, b, (((1,), (1,)), ((), ())), preferred_element_type=F32)


def _dot_tn(a, b):
    return lax.dot_general(a, b, (((0,), (0,)), ((), ())), preferred_element_type=F32)


def _ffn_kernel(x_ref, gpre_ref, gpost_ref, wg_hbm, wu_hbm, wd_hbm, o_ref, xn_ref, wg_buf, wu_buf, wd_buf, sem,
                *, l, f):
    i = pl.program_id(0)
    n_chunks = D_FF // TF

    def copies(j, slot):
        cols = pl.ds(j * TF, TF)
        return (pltpu.make_async_copy(wg_hbm.at[l, f, :, cols], wg_buf.at[slot], sem.at[0, slot]),
                pltpu.make_async_copy(wu_hbm.at[l, f, :, cols], wu_buf.at[slot], sem.at[1, slot]),
                pltpu.make_async_copy(wd_hbm.at[l, f, cols, :], wd_buf.at[slot], sem.at[2, slot]))

    @pl.when(i == 0)
    def _():
        for c in copies(0, 0):
            c.start()

    xn_ref[...] = _rms(x_ref[...], gpre_ref[...]).astype(BF16)
    o_ref[...] = jnp.zeros_like(o_ref)

    def pair(p, carry):
        for slot in range(2):
            j = 2 * p + slot
            for c in copies(j, slot):
                c.wait()
            for c in copies(lax.rem(j + 1, n_chunks), 1 - slot):
                c.start()
            xn = xn_ref[...]
            g = _dot(xn, wg_buf[slot].astype(BF16))
            u = _dot(xn, wu_buf[slot].astype(BF16))
            h = (g * jax.nn.sigmoid(g) * u).astype(BF16)
            o_ref[...] += _dot(h, wd_buf[slot].astype(BF16))
        return carry

    lax.fori_loop(0, n_chunks // 2, pair, 0)
    o_ref[...] = x_ref[...] + 0.5 * _rms(o_ref[...], gpost_ref[...])

    @pl.when(i == pl.num_programs(0) - 1)
    def _():
        for c in copies(0, 0):
            c.wait()


def _gain_spec(l, idx):
    return pl.BlockSpec((None, None, 1, D_MODEL), lambda *_: (l, idx, 0, 0))


def _ffn(x, gains, wg, wu, wd, l, f):
    m = x.shape[0]
    return pl.pallas_call(
        functools.partial(_ffn_kernel, l=l, f=f),
        grid=(m // TM,),
        in_specs=[
            pl.BlockSpec((TM, D_MODEL), lambda i: (i, 0), pipeline_mode=pl.Buffered(1)),
            _gain_spec(l, 4 * f),
            _gain_spec(l, 4 * f + 1),
            pl.BlockSpec(memory_space=pl.ANY),
            pl.BlockSpec(memory_space=pl.ANY),
            pl.BlockSpec(memory_space=pl.ANY),
        ],
        out_specs=pl.BlockSpec((TM, D_MODEL), lambda i: (i, 0)),
        out_shape=jax.ShapeDtypeStruct((m, D_MODEL), F32),
        scratch_shapes=[
            pltpu.VMEM((TM, D_MODEL), BF16),
            pltpu.VMEM((2, D_MODEL, TF), F32),
            pltpu.VMEM((2, D_MODEL, TF), F32),
            pltpu.VMEM((2, TF, D_MODEL), F32),
            pltpu.SemaphoreType.DMA((3, 2)),
        ],
        compiler_params=pltpu.CompilerParams(
            dimension_semantics=("arbitrary",), vmem_limit_bytes=VMEM_LIMIT_FFN),
        name="ffn",
    )(x, gains, gains, wg, wu, wd)


def _proj_kernel(x_ref, g_ref, wt_ref, cs_ref, wa_ref, ba_ref, z_ref, la_ref, xn_ref, *, n_cols):
    j = pl.program_id(1)
    last = pl.num_programs(1) - 1
    n_valid = n_cols - (Z_WIDTH - TN)

    @pl.when(j == 0)
    def _():
        xn_ref[...] = _rms(x_ref[...], g_ref[...]).astype(BF16)

    @pl.when(j < last)
    def _():
        z_ref[...] = _dot_nt(xn_ref[...], wt_ref[...].astype(BF16)) * cs_ref[...]

    @pl.when(j == last)
    def _():
        z_ref[:, :n_valid] = _dot_nt(xn_ref[...], wt_ref[:n_valid, :].astype(BF16)) * cs_ref[:, :n_valid]
        z_ref[:, n_valid:] = jnp.zeros((z_ref.shape[0], TN - n_valid), F32)
        lr_off = COL_LR - (Z_WIDTH - TN)
        lr = z_ref[:, lr_off:lr_off + LR_PAD].astype(BF16)
        logit = _dot(lr, wa_ref[...]) + ba_ref[...]
        log_sig = jnp.minimum(logit, 0.0) - jnp.log1p(jnp.exp(-jnp.abs(logit)))
        la_ref[...] = log_sig * (1.0 / GLA_GATE_NORM)


def _proj(x, gains, w_in_t, colscale, wa_p, ba, l):
    m = x.shape[0]
    n_cols = w_in_t.shape[1]
    assert Z_WIDTH - TN < COL_LR and COL_LR + GLA_RANK <= n_cols <= Z_WIDTH
    return pl.pallas_call(
        functools.partial(_proj_kernel, n_cols=n_cols),
        grid=(m // TM, Z_WIDTH // TN),
        in_specs=[
            pl.BlockSpec((TM, D_MODEL), lambda i, j: (i, 0)),
            _gain_spec(l, 2),
            pl.BlockSpec((None, TN, D_MODEL), lambda i, j: (l, j, 0)),
            pl.BlockSpec((1, TN), lambda i, j: (0, j)),
            pl.BlockSpec((None, LR_PAD, GLA_KWIDTH), lambda i, j: (l, 0, 0)),
            pl.BlockSpec((None, 1, GLA_KWIDTH), lambda i, j: (l, 0, 0)),
        ],
        out_specs=[
            pl.BlockSpec((TM, TN), lambda i, j: (i, j)),
            pl.BlockSpec((TM, GLA_KWIDTH), lambda i, j: (i, 0)),
        ],
        out_shape=[
            jax.ShapeDtypeStruct((m, Z_WIDTH), F32),
            jax.ShapeDtypeStruct((m, GLA_KWIDTH), F32),
        ],
        scratch_shapes=[pltpu.VMEM((TM, D_MODEL), BF16)],
        compiler_params=pltpu.CompilerParams(
            dimension_semantics=("parallel", "arbitrary"), vmem_limit_bytes=VMEM_LIMIT),
        name="proj",
    )(x, gains, w_in_t, colscale, wa_p, ba)


def _pair_blockdiag(x128, kv):
    lane = lax.broadcasted_iota(jnp.int32, x128.shape, 1)
    own = jnp.where((lane >= kv * ATT_HEAD_DIM) & (lane < (kv + 1) * ATT_HEAD_DIM), x128, 0.0)
    other = pltpu.roll(own, ATT_HEAD_DIM, axis=1)
    lo, hi = (own, other) if kv == 0 else (other, own)
    return jnp.concatenate([lo, hi], axis=0).astype(BF16)


def _attn_block(i, sink_ref, q_ref, kc_ref, kp_ref, vc_ref, vp_ref, o_ref):
    kb = 2 * BLOCK
    row = lax.broadcasted_iota(jnp.int32, (BLOCK, kb), 0)
    col = lax.broadcasted_iota(jnp.int32, (BLOCK, kb), 1)
    diff = row + BLOCK - col
    key_pos = col + (i - 1) * BLOCK
    mask = (diff >= 0) & (diff <= WINDOW) & (key_pos >= PAD)
    lane = lax.broadcasted_iota(jnp.int32, (BLOCK, 2 * ATT_HEAD_DIM), 1)
    kk = jnp.concatenate([kp_ref[...], kc_ref[...]], axis=0)
    vv = jnp.concatenate([vp_ref[...], vc_ref[...]], axis=0)
    slabs = ATT_GROUP // 2
    for kv in range(ATT_KV_HEADS):
        k2 = _pair_blockdiag(kk, kv)
        v2 = _pair_blockdiag(vv, kv)
        c0 = kv * slabs * 2 * ATT_HEAD_DIM
        q4 = q_ref[:, c0:c0 + slabs * 2 * ATT_HEAD_DIM]
        q4 = jnp.concatenate([q4[:, p * 128:(p + 1) * 128] for p in range(slabs)], axis=0).astype(BF16)
        s = _dot_nt(q4, k2)
        probs, inv = [], []
        for p in range(slabs):
            halves, rden = [], []
            for e in range(2):
                sink = sink_ref[kv * ATT_GROUP + 2 * p + e]
                sp = jnp.where(mask, s[p * BLOCK:(p + 1) * BLOCK, e * kb:(e + 1) * kb], -jnp.inf)
                mx = jnp.maximum(jnp.max(sp, axis=-1, keepdims=True), sink)
                pe = jnp.exp(sp - mx)
                rden.append(1.0 / (jnp.sum(pe, axis=-1, keepdims=True) + jnp.exp(sink - mx)))
                halves.append(pe.astype(BF16))
            probs.append(jnp.concatenate(halves, axis=1))
            inv.append(jnp.where(lane < ATT_HEAD_DIM, rden[0], rden[1]))
        o = _dot(jnp.concatenate(probs, axis=0), v2)
        for p in range(slabs):
            o_ref[:, c0 + p * 128:c0 + (p + 1) * 128] = (o[p * BLOCK:(p + 1) * BLOCK] * inv[p]).astype(o_ref.dtype)


def _attn_kernel(sink_ref, q_ref, kc_ref, kp_ref, vc_ref, vp_ref, o_ref, *, nb, n_blocks):
    step = pl.program_id(0)

    @pl.when(step < n_blocks)
    def _():
        _attn_block(lax.rem(step, nb), sink_ref, q_ref, kc_ref, kp_ref, vc_ref, vp_ref, o_ref)

    @pl.when(step >= n_blocks)
    def _():
        o_ref[...] = jnp.zeros_like(o_ref)


def _attn_prompt(z, sinks, batch, nb, m_rows):
    n_blocks = batch * nb

    def cur(c):
        return lambda s, sink: (jnp.minimum(s, n_blocks - 1), c)

    def prev(c):
        return lambda s, sink: (jnp.maximum(jnp.minimum(s, n_blocks - 1) - 1, 0), c)

    return pl.pallas_call(
        functools.partial(_attn_kernel, nb=nb, n_blocks=n_blocks),
        grid_spec=pltpu.PrefetchScalarGridSpec(
            num_scalar_prefetch=1,
            grid=(m_rows // BLOCK,),
            in_specs=[
                pl.BlockSpec((BLOCK, ATT_WIDTH), cur(COL_QA // ATT_WIDTH)),
                pl.BlockSpec((BLOCK, KV_WIDTH), cur(COL_KA // KV_WIDTH)),
                pl.BlockSpec((BLOCK, KV_WIDTH), prev(COL_KA // KV_WIDTH)),
                pl.BlockSpec((BLOCK, KV_WIDTH), cur(COL_VA // KV_WIDTH)),
                pl.BlockSpec((BLOCK, KV_WIDTH), prev(COL_VA // KV_WIDTH)),
            ],
            out_specs=pl.BlockSpec((BLOCK, ATT_WIDTH), lambda s, sink: (s, 0)),
        ),
        out_shape=jax.ShapeDtypeStruct((m_rows, ATT_WIDTH), BF16),
        compiler_params=pltpu.CompilerParams(
            dimension_semantics=("arbitrary",), vmem_limit_bytes=VMEM_LIMIT),
        name="attn_prompt",
    )(sinks, z, z, z, z, z)


def _cumsum_rows(la, tri):
    hi = la.astype(BF16)
    r1 = la - hi.astype(F32)
    mid = r1.astype(BF16)
    lo = (r1 - mid.astype(F32)).astype(BF16)
    return _dot(tri, hi) + _dot(tri, mid) + _dot(tri, lo)


def _gla_chunk(c, nc, refs):
    nq = GLA_KWIDTH // COL_TILE
    q_refs, k_refs, v_refs = refs[:nq], refs[nq:2 * nq], refs[2 * nq:2 * nq + GLA_HEADS]
    la_ref, o_ref, st_ref, state_ref = refs[2 * nq + GLA_HEADS:]
    per = COL_TILE // GLA_DK
    C = GLA_CHUNK

    @pl.when(c == 0)
    def _():
        state_ref[...] = jnp.zeros_like(state_ref)

    rowc = lax.broadcasted_iota(jnp.int32, (C, C), 0)
    colc = lax.broadcasted_iota(jnp.int32, (C, C), 1)
    tri = (rowc >= colc).astype(BF16)
    b_all = _cumsum_rows(la_ref[...], tri) * LOG2_E

    pos = c * C + lax.broadcasted_iota(jnp.int32, (C, 1), 0)
    valid = (pos >= PAD).astype(F32)

    lane = lax.broadcasted_iota(jnp.int32, (GLA_SUB, C), 1)
    rsub = lax.broadcasted_iota(jnp.int32, (GLA_SUB, C), 0)
    scale = GLA_DK ** -0.5

    for h in range(GLA_HEADS):
        ks = slice(h * GLA_DK, (h + 1) * GLA_DK)
        vs = slice(h * GLA_DV, (h + 1) * GLA_DV)
        sub = slice((h % per) * GLA_DK, (h % per + 1) * GLA_DK)
        q = q_refs[h // per][:, sub] * scale
        k = k_refs[h // per][:, sub] * valid
        v = v_refs[h][...].astype(BF16)
        bh = b_all[:, ks]
        bl = bh[C - 1:C, :]
        st = state_ref[h]

        o = _dot_nt((q * jnp.exp2(bh)).astype(BF16), st.astype(BF16))

        blocks = []
        for i in range(C // GLA_SUB):
            lo_r = i * GLA_SUB
            qb = q[lo_r:lo_r + GLA_SUB]
            bb = bh[lo_r:lo_r + GLA_SUB]
            w = jnp.zeros((GLA_SUB, C), F32)
            for s in range(GLA_SUB):
                r = lo_r + s
                zed = qb * k[r:r + 1] * jnp.exp2(bb - bh[r:r + 1])
                w = jnp.where(lane == r, jnp.sum(zed, axis=-1, keepdims=True), w)
            w = jnp.where(lane <= rsub + lo_r, w, 0.0)
            if i > 0:
                ref_b = bh[lo_r - 1:lo_r]
                qi = (qb * jnp.exp2(bb - ref_b)).astype(BF16)
                kj = (k * jnp.exp2(jnp.minimum(ref_b - bh, 0.0))).astype(BF16)
                w = jnp.where(lane < lo_r, _dot_nt(qi, kj), w)
            blocks.append(w)
        a = jnp.concatenate(blocks, axis=0).astype(BF16)
        o_ref[:, vs] = o + _dot(a, v)

        k_out = (k * jnp.exp2(bl - bh)).astype(BF16)
        state_ref[h] = st * jnp.exp2(bl) + _dot_tn(v, k_out)

    @pl.when(c == nc - 1)
    def _():
        st_ref[...] = state_ref[...]


def _gla_kernel(*refs, nc, n_chunks):
    step = pl.program_id(0)

    @pl.when(step < n_chunks)
    def _():
        _gla_chunk(lax.rem(step, nc), nc, refs)

    @pl.when(step >= n_chunks)
    def _():
        o_ref = refs[-3]
        o_ref[...] = jnp.zeros_like(o_ref)


def _gla_prompt(z, la, batch, seq, m_rows):
    nc = seq // GLA_CHUNK
    n_chunks = batch * nc
    nq = GLA_KWIDTH // COL_TILE

    def rows(cidx):
        return lambda s: (s, cidx)

    return pl.pallas_call(
        functools.partial(_gla_kernel, nc=nc, n_chunks=n_chunks),
        grid=(m_rows // GLA_CHUNK,),
        in_specs=(
            [pl.BlockSpec((GLA_CHUNK, COL_TILE), rows(COL_QG // COL_TILE + t)) for t in range(nq)]
            + [pl.BlockSpec((GLA_CHUNK, COL_TILE), rows(COL_KG // COL_TILE + t)) for t in range(nq)]
            + [pl.BlockSpec((GLA_CHUNK, GLA_DV), rows(COL_VG // GLA_DV + t)) for t in range(GLA_HEADS)]
            + [pl.BlockSpec((GLA_CHUNK, GLA_KWIDTH), rows(0))]
        ),
        out_specs=[
            pl.BlockSpec((GLA_CHUNK, GLA_WIDTH), rows(0)),
            pl.BlockSpec((None, GLA_HEADS, GLA_DV, GLA_DK), lambda s: (jnp.minimum(s // nc, batch - 1), 0, 0, 0)),
        ],
        out_shape=[
            jax.ShapeDtypeStruct((m_rows, GLA_WIDTH), F32),
            jax.ShapeDtypeStruct((batch, GLA_HEADS, GLA_DV, GLA_DK), F32),
        ],
        scratch_shapes=[pltpu.VMEM((GLA_HEADS, GLA_DV, GLA_DK), F32)],
        compiler_params=pltpu.CompilerParams(
            dimension_semantics=("arbitrary",), vmem_limit_bytes=VMEM_LIMIT),
        name="gla_prompt",
    )(*([z] * (2 * nq + GLA_HEADS)), la)


def _sample_kernel(qa_ref, zs_ref, la_ref, kbuf_ref, vbuf_ref, sink_ref, s0_ref,
                   att_ref, go_ref, st_ref, *, dec_seq):
    S = dec_seq
    R = S * ATT_GROUP
    t_row = lax.broadcasted_iota(jnp.int32, (R, 1), 0) // ATT_GROUP
    c_idx = lax.broadcasted_iota(jnp.int32, (R, kbuf_ref.shape[0]), 1)
    mask_c = c_idx >= t_row
    for kv in range(ATT_KV_HEADS):
        ks = slice(kv * ATT_HEAD_DIM, (kv + 1) * ATT_HEAD_DIM)
        q = qa_ref[kv]
        sink = sink_ref[kv]
        sc = _dot_nt(q.astype(BF16), kbuf_ref[:, ks].astype(BF16))
        sc = jnp.where(mask_c, sc, -jnp.inf)
        mx = jnp.maximum(jnp.max(sc, axis=-1, keepdims=True), sink)
        kn = zs_ref[:, COL_KA + kv * ATT_HEAD_DIM:COL_KA + (kv + 1) * ATT_HEAD_DIM]
        vn = zs_ref[:, COL_VA + kv * ATT_HEAD_DIM:COL_VA + (kv + 1) * ATT_HEAD_DIM]
        sn = []
        for s in range(S):
            v = jnp.sum(q * kn[s:s + 1], axis=-1, keepdims=True)
            v = jnp.where(t_row >= s, v, -jnp.inf)
            sn.append(v)
            mx = jnp.maximum(mx, v)
        pc = jnp.exp(sc - mx)
        den = jnp.sum(pc, axis=-1, keepdims=True) + jnp.exp(sink - mx)
        o = _dot(pc.astype(BF16), vbuf_ref[:, ks].astype(BF16))
        for s in range(S):
            pn = jnp.exp(sn[s] - mx)
            den = den + pn
            o = o + pn * vn[s:s + 1]
        att_ref[kv] = o / den

    rows8 = zs_ref.shape[0]
    rid = lax.broadcasted_iota(jnp.int32, (rows8, 1), 0)
    la = la_ref[...]
    run = la[0:1]
    b_all = jnp.where(rid == 0, run, 0.0)
    for t in range(1, S):
        run = run + la[t:t + 1]
        b_all = jnp.where(rid == t, run, b_all)
    real = (rid < S).astype(F32)
    scale = GLA_DK ** -0.5
    for h in range(GLA_HEADS):
        ks = slice(h * GLA_DK, (h + 1) * GLA_DK)
        q = zs_ref[:, COL_QG + h * GLA_DK:COL_QG + (h + 1) * GLA_DK] * scale
        k = zs_ref[:, COL_KG + h * GLA_DK:COL_KG + (h + 1) * GLA_DK]
        v = zs_ref[:, COL_VG + h * GLA_DV:COL_VG + (h + 1) * GLA_DV]
        bh = b_all[:, ks] * real
        bl = bh[S - 1:S]
        s0 = s0_ref[h]
        k_out = k * jnp.exp(jnp.minimum(bl - bh, 0.0)) * real
        q_in = q * jnp.exp(bh) * real
        o = _dot(q_in.astype(BF16), s0.astype(BF16))
        for s in range(S):
            zed = q * k[s:s + 1] * jnp.exp(jnp.minimum(bh - bh[s:s + 1], 0.0))
            w = jnp.sum(zed, axis=-1, keepdims=True)
            w = jnp.where((rid >= s) & (rid < S), w, 0.0)
            o = o + w * v[s:s + 1]
        go_ref[:, h * GLA_DV:(h + 1) * GLA_DV] = o
        e_rows = jnp.broadcast_to(jnp.exp(bl), (GLA_DK, GLA_DK)).T
        decay = jnp.concatenate([e_rows] * (GLA_DV // GLA_DK), axis=1)
        st_ref[h] = decay * s0 + _dot_tn(k_out.astype(BF16), v.astype(BF16))


def _sample_mixer(qa_s, zs8, la8, kbuf, vbuf, sink_rows, s0, dec_seq, l):
    nb = zs8.shape[0]
    rows8 = zs8.shape[1]
    R = dec_seq * ATT_GROUP
    wb = kbuf.shape[2]
    return pl.pallas_call(
        functools.partial(_sample_kernel, dec_seq=dec_seq),
        grid=(nb,),
        in_specs=[
            pl.BlockSpec((None, ATT_KV_HEADS, R, ATT_HEAD_DIM), lambda b: (b, 0, 0, 0)),
            pl.BlockSpec((None, rows8, Z_WIDTH), lambda b: (b, 0, 0)),
            pl.BlockSpec((None, rows8, GLA_KWIDTH), lambda b: (b, 0, 0)),
            pl.BlockSpec((None, None, wb, KV_WIDTH), lambda b: (l, b, 0, 0)),
            pl.BlockSpec((None, None, wb, KV_WIDTH), lambda b: (l, b, 0, 0)),
            pl.BlockSpec((ATT_KV_HEADS, R, 1), lambda b: (0, 0, 0)),
            pl.BlockSpec((None, None, GLA_HEADS, GLA_DK, GLA_DV), lambda b: (l, b, 0, 0, 0)),
        ],
        out_specs=[
            pl.BlockSpec((None, ATT_KV_HEADS, R, ATT_HEAD_DIM), lambda b: (b, 0, 0, 0)),
            pl.BlockSpec((None, rows8, GLA_WIDTH), lambda b: (b, 0, 0)),
            pl.BlockSpec((None, GLA_HEADS, GLA_DK, GLA_DV), lambda b: (b, 0, 0, 0)),
        ],
        out_shape=[
            jax.ShapeDtypeStruct((nb, ATT_KV_HEADS, R, ATT_HEAD_DIM), F32),
            jax.ShapeDtypeStruct((nb, rows8, GLA_WIDTH), F32),
            jax.ShapeDtypeStruct((nb, GLA_HEADS, GLA_DK, GLA_DV), F32),
        ],
        compiler_params=pltpu.CompilerParams(
            dimension_semantics=("parallel",), vmem_limit_bytes=VMEM_LIMIT),
        name="sample_mixer",
    )(qa_s, zs8, la8, kbuf, vbuf, sink_rows, s0)


def _merge_kernel(*refs):
    x_ref, att_ref, go_ref = refs[:3]
    rg_refs = refs[3:3 + GLA_HEADS]
    ggla_ref, wo_ref, gpost_ref, o_ref = refs[3 + GLA_HEADS:]
    parts = []
    for h in range(GLA_HEADS):
        vs = slice(h * GLA_DV, (h + 1) * GLA_DV)
        rg = rg_refs[h][...]
        parts.append((_rms(go_ref[:, vs], ggla_ref[...]) * (rg * jax.nn.sigmoid(rg))).astype(BF16))
    go = jnp.concatenate(parts, axis=-1)
    y = _dot(att_ref[...], wo_ref[:ATT_WIDTH, :]) + _dot(go, wo_ref[ATT_WIDTH:, :])
    o_ref[...] = x_ref[...] + _rms(y, gpost_ref[...])


def _merge(x, att, go, z, g_gla, w_out, gains, l):
    m = x.shape[0]
    return pl.pallas_call(
        _merge_kernel,
        grid=(m // TM_MERGE,),
        in_specs=(
            [pl.BlockSpec((TM_MERGE, D_MODEL), lambda i: (i, 0)),
             pl.BlockSpec((TM_MERGE, ATT_WIDTH), lambda i: (i, 0)),
             pl.BlockSpec((TM_MERGE, GLA_WIDTH), lambda i: (i, 0))]
            + [pl.BlockSpec((TM_MERGE, GLA_DV), lambda i, h=h: (i, COL_RG // GLA_DV + h)) for h in range(GLA_HEADS)]
            + [pl.BlockSpec((None, 1, GLA_DV), lambda i: (l, 0, 0)),
               pl.BlockSpec((None, ATT_WIDTH + GLA_WIDTH, D_MODEL), lambda i: (l, 0, 0)),
               _gain_spec(l, 3)]
        ),
        out_specs=pl.BlockSpec((TM_MERGE, D_MODEL), lambda i: (i, 0)),
        out_shape=jax.ShapeDtypeStruct((m, D_MODEL), F32),
        compiler_params=pltpu.CompilerParams(
            dimension_semantics=("parallel",), vmem_limit_bytes=VMEM_LIMIT),
        name="merge",
    )(x, att, go, *([z] * GLA_HEADS), g_gla, w_out, gains)


def kernel(x_prompt, x_sample, cache_k_win, cache_v_win, state_gla, meta_tokens, norm_gains,
           w_ffn_gate, w_ffn_up, w_ffn_down, w_in, w_gate_up, b_gate, attn_sinks, gla_norm, w_out):
    batch, seq, _ = x_prompt.shape
    dec_batch, dec_seq, _ = x_sample.shape
    depth = norm_gains.shape[0]
    lp = seq + BLOCK
    nb = lp // BLOCK
    mp = batch * lp
    ms = dec_batch * dec_seq
    m_pad = -(-(mp + ms) // TM) * TM
    wb = cache_k_win.shape[2]
    rows8 = 8
    assert dec_seq <= rows8 and lp % GLA_CHUNK == 0

    head = jnp.concatenate([jnp.zeros((PAD, D_MODEL), F32), meta_tokens.astype(F32)], axis=0)
    pieces = []
    for b in range(batch):
        pieces += [head, x_prompt[b]]
    pieces += [x_sample.reshape(ms, D_MODEL), jnp.zeros((m_pad - mp - ms, D_MODEL), F32)]
    x = jnp.concatenate(pieces, axis=0)

    wg, wu, wd = w_ffn_gate, w_ffn_up, w_ffn_down
    wo = w_out.astype(BF16)
    w_in_t = jnp.swapaxes(w_in, 1, 2)
    col = jnp.arange(Z_WIDTH)
    colscale = jnp.where((col >= COL_QA) & (col < COL_QA + ATT_WIDTH), ATT_HEAD_DIM ** -0.5, 1.0).astype(F32)[None]
    wa_p = jnp.pad(w_gate_up, ((0, 0), (0, LR_PAD - GLA_RANK), (0, 0))).astype(BF16)
    ba = b_gate.reshape(depth, 1, GLA_KWIDTH)
    g_gla = gla_norm.reshape(depth, 1, GLA_DV)
    gains = norm_gains.reshape(depth, 6, 1, D_MODEL)
    kbuf = cache_k_win.reshape(depth, dec_batch, wb, KV_WIDTH)
    vbuf = cache_v_win.reshape(depth, dec_batch, wb, KV_WIDTH)
    tail = jnp.zeros((m_pad - mp - ms, GLA_WIDTH), F32)

    pk, pv, ps, sk, sv, ss = [], [], [], [], [], []
    for l in range(depth):
        x = _ffn(x, gains, wg, wu, wd, l, 0)

        z, la = _proj(x, gains, w_in_t, colscale, wa_p, ba, l)
        att = _attn_prompt(z, attn_sinks[l], batch, nb, m_pad)
        go, st_p = _gla_prompt(z, la, batch, lp, m_pad)

        zs = z[mp:mp + ms].reshape(dec_batch, dec_seq, Z_WIDTH)
        zs8 = jnp.pad(zs, ((0, 0), (0, rows8 - dec_seq), (0, 0)))
        la8 = jnp.pad(la[mp:mp + ms].reshape(dec_batch, dec_seq, GLA_KWIDTH), ((0, 0), (0, rows8 - dec_seq), (0, 0)))
        qa_s = zs[:, :, COL_QA:COL_QA + ATT_WIDTH].reshape(dec_batch, dec_seq, ATT_KV_HEADS, ATT_GROUP, ATT_HEAD_DIM)
        qa_s = qa_s.transpose(0, 2, 1, 3, 4).reshape(dec_batch, ATT_KV_HEADS, dec_seq * ATT_GROUP, ATT_HEAD_DIM)
        sink_rows = jnp.tile(attn_sinks[l].reshape(ATT_KV_HEADS, 1, ATT_GROUP), (1, dec_seq, 1))
        sink_rows = sink_rows.reshape(ATT_KV_HEADS, dec_seq * ATT_GROUP, 1)
        att_s, go_s, st_s = _sample_mixer(qa_s, zs8, la8, kbuf, vbuf, sink_rows, state_gla, dec_seq, l)
        att_s = att_s.reshape(dec_batch, ATT_KV_HEADS, dec_seq, ATT_GROUP, ATT_HEAD_DIM)
        att_s = att_s.transpose(0, 2, 1, 3, 4).reshape(ms, ATT_WIDTH)

        att = lax.dynamic_update_slice(att, jnp.concatenate([att_s, tail], axis=0).astype(BF16), (mp, 0))
        go = lax.dynamic_update_slice(go, jnp.concatenate([go_s[:, :dec_seq].reshape(ms, GLA_WIDTH), tail], axis=0), (mp, 0))
        x = _merge(x, att, go, z, g_gla, wo, gains, l)

        win = [z[b * lp + lp - WINDOW:(b + 1) * lp, COL_KA:COL_VA + KV_WIDTH] for b in range(batch)]
        win = jnp.stack(win).reshape(batch, WINDOW, 2, ATT_KV_HEADS, ATT_HEAD_DIM)
        pk.append(win[:, :, 0])
        pv.append(win[:, :, 1])
        ps.append(st_p.transpose(0, 1, 3, 2))
        kn = zs[:, :, COL_KA:COL_KA + KV_WIDTH].reshape(dec_batch, dec_seq, ATT_KV_HEADS, ATT_HEAD_DIM)
        vn = zs[:, :, COL_VA:COL_VA + KV_WIDTH].reshape(dec_batch, dec_seq, ATT_KV_HEADS, ATT_HEAD_DIM)
        sk.append(jnp.concatenate([cache_k_win[l], kn], axis=1)[:, -wb:])
        sv.append(jnp.concatenate([cache_v_win[l], vn], axis=1)[:, -wb:])
        ss.append(st_s)

        x = _ffn(x, gains, wg, wu, wd, l, 1)

    y_prompt = jnp.stack([x[b * lp + BLOCK:(b + 1) * lp] for b in range(batch)])
    y_sample = x[mp:mp + ms].reshape(dec_batch, dec_seq, D_MODEL)
    return (y_prompt, y_sample, jnp.stack(pk), jnp.stack(pv), jnp.stack(ps),
            jnp.stack(sk), jnp.stack(sv), jnp.stack(ss))
```

```python
import functools

import jax
import jax.numpy as jnp
from jax import lax
from jax.experimental import pallas as pl
from jax.experimental.pallas import tpu as pltpu

F32 = jnp.float32
BF16 = jnp.bfloat16

D_MODEL = 2048
D_FF = 5632
N_META = 16
BLOCK = 128
WINDOW = 128
ATT_HEADS = 16
ATT_KV_HEADS = 2
ATT_GROUP = 8
ATT_HEAD_DIM = 64
ATT_WIDTH = 1024
KV_WIDTH = 128
GLA_HEADS = 4
GLA_DK = 128
GLA_DV = 256
GLA_KWIDTH = 512
GLA_WIDTH = 1024
GLA_RANK = 16
GLA_GATE_NORM = 16.0
GLA_CHUNK = 64
GLA_SUB = 16
EPS = 1e-6
LOG2_E = 1.4426950408889634
PAD = BLOCK - N_META

COL_QA, COL_KA, COL_VA, COL_QG, COL_KG, COL_VG, COL_RG, COL_LR = 0, 1024, 1152, 1280, 1792, 2304, 3328, 4352
COL_TILE = 256
Z_WIDTH = 4608
LR_PAD = 128

VMEM_LIMIT = 56 * 1024 * 1024
VMEM_LIMIT_FFN = 60 * 1024 * 1024

TM = 1072
TM_MERGE = 536
TF = 256
TN = 768
X_PARTS = 4


def _rms(x, gain):
    ms = jnp.mean(x * x, axis=-1, keepdims=True)
    return x * lax.rsqrt(ms + EPS) * gain


def _dot(a, b):
    return jnp.dot(a, b, preferred_element_type=F32)


def _dot_nt(a, b):
    return lax.dot_general(a, b, (((1,), (1,)), ((), ())), preferred_element_type=F32)


def _dot_tn(a, b):
    return lax.dot_general(a, b, (((0,), (0,)), ((), ())), preferred_element_type=F32)


def _row_parts(n_rows, parts):
    units = n_rows // 8
    out, start = [], 0
    for r in range(parts):
        size = (units // parts + (1 if r < units % parts else 0)) * 8
        out.append((start, size))
        start += size
    assert start == n_rows
    return out


def _ffn_kernel(x_hbm, gpre_ref, gpost_ref, wg_hbm, wu_hbm, wd_hbm, o_ref, xn_ref, x_buf, wg_buf, wu_buf, wd_buf,
                sem, xsem, *, l, f):
    i = pl.program_id(0)
    n_tiles = pl.num_programs(0)
    n_chunks = D_FF // TF
    parts = _row_parts(TM, X_PARTS)

    def w_copies(j, slot):
        cols = pl.ds(j * TF, TF)
        return (pltpu.make_async_copy(wg_hbm.at[l, f, :, cols], wg_buf.at[slot], sem.at[0, slot]),
                pltpu.make_async_copy(wu_hbm.at[l, f, :, cols], wu_buf.at[slot], sem.at[1, slot]),
                pltpu.make_async_copy(wd_hbm.at[l, f, cols, :], wd_buf.at[slot], sem.at[2, slot]))

    def x_copy(tile, r):
        start, size = parts[r]
        return pltpu.make_async_copy(x_hbm.at[pl.ds(tile * TM + start, size), :],
                                     x_buf.at[pl.ds(start, size), :], xsem.at[r])

    @pl.when(i == 0)
    def _():
        for r in range(X_PARTS):
            x_copy(0, r).start()
        for c in w_copies(0, 0):
            c.start()

    for r, (start, size) in enumerate(parts):
        x_copy(i, r).wait()
        rows = pl.ds(start, size)
        xn_ref[rows, :] = _rms(x_buf[rows, :], gpre_ref[...]).astype(BF16)
    o_ref[...] = jnp.zeros_like(o_ref)

    def pair(p, carry):
        for slot in range(2):
            j = 2 * p + slot
            for c in w_copies(j, slot):
                c.wait()
            for c in w_copies(lax.rem(j + 1, n_chunks), 1 - slot):
                c.start()
            xn = xn_ref[...]
            g = _dot(xn, wg_buf[slot].astype(BF16))
            u = _dot(xn, wu_buf[slot].astype(BF16))
            h = (g * jax.nn.sigmoid(g) * u).astype(BF16)
            o_ref[...] += _dot(h, wd_buf[slot].astype(BF16))
        return carry

    lax.fori_loop(0, n_chunks // 2, pair, 0)

    for r, (start, size) in enumerate(parts):
        rows = pl.ds(start, size)
        o_ref[rows, :] = x_buf[rows, :] + 0.5 * _rms(o_ref[rows, :], gpost_ref[...])

        @pl.when(i + 1 < n_tiles)
        def _():
            x_copy(i + 1, r).start()

    @pl.when(i == n_tiles - 1)
    def _():
        for c in w_copies(0, 0):
            c.wait()


def _gain_spec(l, idx):
    return pl.BlockSpec((None, None, 1, D_MODEL), lambda *_: (l, idx, 0, 0))


def _ffn(x, gains, wg, wu, wd, l, f):
    m = x.shape[0]
    return pl.pallas_call(
        functools.partial(_ffn_kernel, l=l, f=f),
        grid=(m // TM,),
        in_specs=[
            pl.BlockSpec(memory_space=pl.ANY),
            _gain_spec(l, 4 * f),
            _gain_spec(l, 4 * f + 1),
            pl.BlockSpec(memory_space=pl.ANY),
            pl.BlockSpec(memory_space=pl.ANY),
            pl.BlockSpec(memory_space=pl.ANY),
        ],
        out_specs=pl.BlockSpec((TM, D_MODEL), lambda i: (i, 0)),
        out_shape=jax.ShapeDtypeStruct((m, D_MODEL), F32),
        scratch_shapes=[
            pltpu.VMEM((TM, D_MODEL), BF16),
            pltpu.VMEM((TM, D_MODEL), F32),
            pltpu.VMEM((2, D_MODEL, TF), F32),
            pltpu.VMEM((2, D_MODEL, TF), F32),
            pltpu.VMEM((2, TF, D_MODEL), F32),
            pltpu.SemaphoreType.DMA((3, 2)),
            pltpu.SemaphoreType.DMA((X_PARTS,)),
        ],
        compiler_params=pltpu.CompilerParams(
            dimension_semantics=("arbitrary",), vmem_limit_bytes=VMEM_LIMIT_FFN),
        name="ffn",
    )(x, gains, gains, wg, wu, wd)


def _proj_kernel(x_ref, g_ref, wt_ref, cs_ref, wa_ref, ba_ref, z_ref, la_ref, xn_ref, *, n_cols):
    j = pl.program_id(1)
    last = pl.num_programs(1) - 1
    n_valid = n_cols - (Z_WIDTH - TN)

    @pl.when(j == 0)
    def _():
        xn_ref[...] = _rms(x_ref[...], g_ref[...]).astype(BF16)

    @pl.when(j < last)
    def _():
        z_ref[...] = _dot_nt(xn_ref[...], wt_ref[...].astype(BF16)) * cs_ref[...]

    @pl.when(j == last)
    def _():
        z_ref[:, :n_valid] = _dot_nt(xn_ref[...], wt_ref[:n_valid, :].astype(BF16)) * cs_ref[:, :n_valid]
        z_ref[:, n_valid:] = jnp.zeros((z_ref.shape[0], TN - n_valid), F32)
        lr_off = COL_LR - (Z_WIDTH - TN)
        lr = z_ref[:, lr_off:lr_off + LR_PAD].astype(BF16)
        logit = _dot(lr, wa_ref[...]) + ba_ref[...]
        log_sig = jnp.minimum(logit, 0.0) - jnp.log1p(jnp.exp(-jnp.abs(logit)))
        la_ref[...] = log_sig * (1.0 / GLA_GATE_NORM)


def _proj(x, gains, w_in_t, colscale, wa_p, ba, l):
    m = x.shape[0]
    n_cols = w_in_t.shape[1]
    assert Z_WIDTH - TN < COL_LR and COL_LR + GLA_RANK <= n_cols <= Z_WIDTH
    return pl.pallas_call(
        functools.partial(_proj_kernel, n_cols=n_cols),
        grid=(m // TM, Z_WIDTH // TN),
        in_specs=[
            pl.BlockSpec((TM, D_MODEL), lambda i, j: (i, 0)),
            _gain_spec(l, 2),
            pl.BlockSpec((None, TN, D_MODEL), lambda i, j: (l, j, 0)),
            pl.BlockSpec((1, TN), lambda i, j: (0, j)),
            pl.BlockSpec((None, LR_PAD, GLA_KWIDTH), lambda i, j: (l, 0, 0)),
            pl.BlockSpec((None, 1, GLA_KWIDTH), lambda i, j: (l, 0, 0)),
        ],
        out_specs=[
            pl.BlockSpec((TM, TN), lambda i, j: (i, j)),
            pl.BlockSpec((TM, GLA_KWIDTH), lambda i, j: (i, 0)),
        ],
        out_shape=[
            jax.ShapeDtypeStruct((m, Z_WIDTH), F32),
            jax.ShapeDtypeStruct((m, GLA_KWIDTH), F32),
        ],
        scratch_shapes=[pltpu.VMEM((TM, D_MODEL), BF16)],
        compiler_params=pltpu.CompilerParams(
            dimension_semantics=("parallel", "arbitrary"), vmem_limit_bytes=VMEM_LIMIT),
        name="proj",
    )(x, gains, w_in_t, colscale, wa_p, ba)


def _pair_blockdiag(x128, kv):
    lane = lax.broadcasted_iota(jnp.int32, x128.shape, 1)
    own = jnp.where((lane >= kv * ATT_HEAD_DIM) & (lane < (kv + 1) * ATT_HEAD_DIM), x128, 0.0)
    other = pltpu.roll(own, ATT_HEAD_DIM, axis=1)
    lo, hi = (own, other) if kv == 0 else (other, own)
    return jnp.concatenate([lo, hi], axis=0).astype(BF16)


def _attn_block(i, sink_ref, q_ref, kc_ref, kp_ref, vc_ref, vp_ref, o_ref):
    kb = 2 * BLOCK
    row = lax.broadcasted_iota(jnp.int32, (BLOCK, kb), 0)
    col = lax.broadcasted_iota(jnp.int32, (BLOCK, kb), 1)
    diff = row + BLOCK - col
    key_pos = col + (i - 1) * BLOCK
    mask = (diff >= 0) & (diff <= WINDOW) & (key_pos >= PAD)
    lane = lax.broadcasted_iota(jnp.int32, (BLOCK, 2 * ATT_HEAD_DIM), 1)
    kk = jnp.concatenate([kp_ref[...], kc_ref[...]], axis=0)
    vv = jnp.concatenate([vp_ref[...], vc_ref[...]], axis=0)
    slabs = ATT_GROUP // 2
    for kv in range(ATT_KV_HEADS):
        k2 = _pair_blockdiag(kk, kv)
        v2 = _pair_blockdiag(vv, kv)
        c0 = kv * slabs * 2 * ATT_HEAD_DIM
        q4 = q_ref[:, c0:c0 + slabs * 2 * ATT_HEAD_DIM]
        q4 = jnp.concatenate([q4[:, p * 128:(p + 1) * 128] for p in range(slabs)], axis=0).astype(BF16)
        s = _dot_nt(q4, k2)
        probs, inv = [], []
        for p in range(slabs):
            halves, rden = [], []
            for e in range(2):
                sink = sink_ref[kv * ATT_GROUP + 2 * p + e]
                sp = jnp.where(mask, s[p * BLOCK:(p + 1) * BLOCK, e * kb:(e + 1) * kb], -jnp.inf)
                mx = jnp.maximum(jnp.max(sp, axis=-1, keepdims=True), sink)
                pe = jnp.exp(sp - mx)
                rden.append(1.0 / (jnp.sum(pe, axis=-1, keepdims=True) + jnp.exp(sink - mx)))
                halves.append(pe.astype(BF16))
            probs.append(jnp.concatenate(halves, axis=1))
            inv.append(jnp.where(lane < ATT_HEAD_DIM, rden[0], rden[1]))
        o = _dot(jnp.concatenate(probs, axis=0), v2)
        for p in range(slabs):
            o_ref[:, c0 + p * 128:c0 + (p + 1) * 128] = (o[p * BLOCK:(p + 1) * BLOCK] * inv[p]).astype(o_ref.dtype)


def _attn_kernel(sink_ref, q_ref, kc_ref, kp_ref, vc_ref, vp_ref, o_ref, *, nb, n_blocks):
    step = pl.program_id(0)

    @pl.when(step < n_blocks)
    def _():
        _attn_block(lax.rem(step, nb), sink_ref, q_ref, kc_ref, kp_ref, vc_ref, vp_ref, o_ref)

    @pl.when(step >= n_blocks)
    def _():
        o_ref[...] = jnp.zeros_like(o_ref)


def _attn_prompt(z, sinks, batch, nb, m_rows):
    n_blocks = batch * nb

    def cur(c):
        return lambda s, sink: (jnp.minimum(s, n_blocks - 1), c)

    def prev(c):
        return lambda s, sink: (jnp.maximum(jnp.minimum(s, n_blocks - 1) - 1, 0), c)

    return pl.pallas_call(
        functools.partial(_attn_kernel, nb=nb, n_blocks=n_blocks),
        grid_spec=pltpu.PrefetchScalarGridSpec(
            num_scalar_prefetch=1,
            grid=(m_rows // BLOCK,),
            in_specs=[
                pl.BlockSpec((BLOCK, ATT_WIDTH), cur(COL_QA // ATT_WIDTH)),
                pl.BlockSpec((BLOCK, KV_WIDTH), cur(COL_KA // KV_WIDTH)),
                pl.BlockSpec((BLOCK, KV_WIDTH), prev(COL_KA // KV_WIDTH)),
                pl.BlockSpec((BLOCK, KV_WIDTH), cur(COL_VA // KV_WIDTH)),
                pl.BlockSpec((BLOCK, KV_WIDTH), prev(COL_VA // KV_WIDTH)),
            ],
            out_specs=pl.BlockSpec((BLOCK, ATT_WIDTH), lambda s, sink: (s, 0)),
        ),
        out_shape=jax.ShapeDtypeStruct((m_rows, ATT_WIDTH), BF16),
        compiler_params=pltpu.CompilerParams(
            dimension_semantics=("arbitrary",), vmem_limit_bytes=VMEM_LIMIT),
        name="attn_prompt",
    )(sinks, z, z, z, z, z)


def _cumsum_rows(la, tri):
    hi = la.astype(BF16)
    r1 = la - hi.astype(F32)
    mid = r1.astype(BF16)
    lo = (r1 - mid.astype(F32)).astype(BF16)
    return _dot(tri, hi) + _dot(tri, mid) + _dot(tri, lo)


def _gla_chunk(c, nc, refs):
    nq = GLA_KWIDTH // COL_TILE
    q_refs, k_refs, v_refs = refs[:nq], refs[nq:2 * nq], refs[2 * nq:2 * nq + GLA_HEADS]
    la_ref, o_ref, st_ref, state_ref = refs[2 * nq + GLA_HEADS:]
    per = COL_TILE // GLA_DK
    C = GLA_CHUNK

    @pl.when(c == 0)
    def _():
        state_ref[...] = jnp.zeros_like(state_ref)

    rowc = lax.broadcasted_iota(jnp.int32, (C, C), 0)
    colc = lax.broadcasted_iota(jnp.int32, (C, C), 1)
    tri = (rowc >= colc).astype(BF16)
    b_all = _cumsum_rows(la_ref[...], tri) * LOG2_E

    pos = c * C + lax.broadcasted_iota(jnp.int32, (C, 1), 0)
    valid = (pos >= PAD).astype(F32)

    lane = lax.broadcasted_iota(jnp.int32, (GLA_SUB, C), 1)
    rsub = lax.broadcasted_iota(jnp.int32, (GLA_SUB, C), 0)
    scale = GLA_DK ** -0.5

    for h in range(GLA_HEADS):
        ks = slice(h * GLA_DK, (h + 1) * GLA_DK)
        vs = slice(h * GLA_DV, (h + 1) * GLA_DV)
        sub = slice((h % per) * GLA_DK, (h % per + 1) * GLA_DK)
        q = q_refs[h // per][:, sub] * scale
        k = k_refs[h // per][:, sub] * valid
        v = v_refs[h][...].astype(BF16)
        bh = b_all[:, ks]
        bl = bh[C - 1:C, :]
        st = state_ref[h]

        o = _dot_nt((q * jnp.exp2(bh)).astype(BF16), st.astype(BF16))

        blocks = []
        for i in range(C // GLA_SUB):
            lo_r = i * GLA_SUB
            qb = q[lo_r:lo_r + GLA_SUB]
            bb = bh[lo_r:lo_r + GLA_SUB]
            w = jnp.zeros((GLA_SUB, C), F32)
            for s in range(GLA_SUB):
                r = lo_r + s
                zed = qb * k[r:r + 1] * jnp.exp2(bb - bh[r:r + 1])
                w = jnp.where(lane == r, jnp.sum(zed, axis=-1, keepdims=True), w)
            w = jnp.where(lane <= rsub + lo_r, w, 0.0)
            if i > 0:
                ref_b = bh[lo_r - 1:lo_r]
                qi = (qb * jnp.exp2(bb - ref_b)).astype(BF16)
                kj = (k * jnp.exp2(jnp.minimum(ref_b - bh, 0.0))).astype(BF16)
                w = jnp.where(lane < lo_r, _dot_nt(qi, kj), w)
            blocks.append(w)
        a = jnp.concatenate(blocks, axis=0).astype(BF16)
        o_ref[:, vs] = o + _dot(a, v)

        k_out = (k * jnp.exp2(bl - bh)).astype(BF16)
        state_ref[h] = st * jnp.exp2(bl) + _dot_tn(v, k_out)

    @pl.when(c == nc - 1)
    def _():
        st_ref[...] = state_ref[...]


def _gla_kernel(*refs, nc, n_chunks):
    step = pl.program_id(0)

    @pl.when(step < n_chunks)
    def _():
        _gla_chunk(lax.rem(step, nc), nc, refs)

    @pl.when(step >= n_chunks)
    def _():
        o_ref = refs[-3]
        o_ref[...] = jnp.zeros_like(o_ref)


def _gla_prompt(z, la, batch, seq, m_rows):
    nc = seq // GLA_CHUNK
    n_chunks = batch * nc
    nq = GLA_KWIDTH // COL_TILE

    def rows(cidx):
        return lambda s: (s, cidx)

    return pl.pallas_call(
        functools.partial(_gla_kernel, nc=nc, n_chunks=n_chunks),
        grid=(m_rows // GLA_CHUNK,),
        in_specs=(
            [pl.BlockSpec((GLA_CHUNK, COL_TILE), rows(COL_QG // COL_TILE + t)) for t in range(nq)]
            + [pl.BlockSpec((GLA_CHUNK, COL_TILE), rows(COL_KG // COL_TILE + t)) for t in range(nq)]
            + [pl.BlockSpec((GLA_CHUNK, GLA_DV), rows(COL_VG // GLA_DV + t)) for t in range(GLA_HEADS)]
            + [pl.BlockSpec((GLA_CHUNK, GLA_KWIDTH), rows(0))]
        ),
        out_specs=[
            pl.BlockSpec((GLA_CHUNK, GLA_WIDTH), rows(0)),
            pl.BlockSpec((None, GLA_HEADS, GLA_DV, GLA_DK), lambda s: (jnp.minimum(s // nc, batch - 1), 0, 0, 0)),
        ],
        out_shape=[
            jax.ShapeDtypeStruct((m_rows, GLA_WIDTH), F32),
            jax.ShapeDtypeStruct((batch, GLA_HEADS, GLA_DV, GLA_DK), F32),
        ],
        scratch_shapes=[pltpu.VMEM((GLA_HEADS, GLA_DV, GLA_DK), F32)],
        compiler_params=pltpu.CompilerParams(
            dimension_semantics=("arbitrary",), vmem_limit_bytes=VMEM_LIMIT),
        name="gla_prompt",
    )(*([z] * (2 * nq + GLA_HEADS)), la)


def _sample_kernel(qa_ref, zs_ref, la_ref, kbuf_ref, vbuf_ref, sink_ref, s0_ref,
                   att_ref, go_ref, st_ref, *, dec_seq):
    S = dec_seq
    R = S * ATT_GROUP
    t_row = lax.broadcasted_iota(jnp.int32, (R, 1), 0) // ATT_GROUP
    c_idx = lax.broadcasted_iota(jnp.int32, (R, kbuf_ref.shape[0]), 1)
    mask_c = c_idx >= t_row
    for kv in range(ATT_KV_HEADS):
        ks = slice(kv * ATT_HEAD_DIM, (kv + 1) * ATT_HEAD_DIM)
        q = qa_ref[kv]
        sink = sink_ref[kv]
        sc = _dot_nt(q.astype(BF16), kbuf_ref[:, ks].astype(BF16))
        sc = jnp.where(mask_c, sc, -jnp.inf)
        mx = jnp.maximum(jnp.max(sc, axis=-1, keepdims=True), sink)
        kn = zs_ref[:, COL_KA + kv * ATT_HEAD_DIM:COL_KA + (kv + 1) * ATT_HEAD_DIM]
        vn = zs_ref[:, COL_VA + kv * ATT_HEAD_DIM:COL_VA + (kv + 1) * ATT_HEAD_DIM]
        sn = []
        for s in range(S):
            v = jnp.sum(q * kn[s:s + 1], axis=-1, keepdims=True)
            v = jnp.where(t_row >= s, v, -jnp.inf)
            sn.append(v)
            mx = jnp.maximum(mx, v)
        pc = jnp.exp(sc - mx)
        den = jnp.sum(pc, axis=-1, keepdims=True) + jnp.exp(sink - mx)
        o = _dot(pc.astype(BF16), vbuf_ref[:, ks].astype(BF16))
        for s in range(S):
            pn = jnp.exp(sn[s] - mx)
            den = den + pn
            o = o + pn * vn[s:s + 1]
        att_ref[kv] = o / den

    rows8 = zs_ref.shape[0]
    rid = lax.broadcasted_iota(jnp.int32, (rows8, 1), 0)
    la = la_ref[...]
    run = la[0:1]
    b_all = jnp.where(rid == 0, run, 0.0)
    for t in range(1, S):
        run = run + la[t:t + 1]
        b_all = jnp.where(rid == t, run, b_all)
    real = (rid < S).astype(F32)
    scale = GLA_DK ** -0.5
    for h in range(GLA_HEADS):
        ks = slice(h * GLA_DK, (h + 1) * GLA_DK)
        q = zs_ref[:, COL_QG + h * GLA_DK:COL_QG + (h + 1) * GLA_DK] * scale
        k = zs_ref[:, COL_KG + h * GLA_DK:COL_KG + (h + 1) * GLA_DK]
        v = zs_ref[:, COL_VG + h * GLA_DV:COL_VG + (h + 1) * GLA_DV]
        bh = b_all[:, ks] * real
        bl = bh[S - 1:S]
        s0 = s0_ref[h]
        k_out = k * jnp.exp(jnp.minimum(bl - bh, 0.0)) * real
        q_in = q * jnp.exp(bh) * real
        o = _dot(q_in.astype(BF16), s0.astype(BF16))
        for s in range(S):
            zed = q * k[s:s + 1] * jnp.exp(jnp.minimum(bh - bh[s:s + 1], 0.0))
            w = jnp.sum(zed, axis=-1, keepdims=True)
            w = jnp.where((rid >= s) & (rid < S), w, 0.0)
            o = o + w * v[s:s + 1]
        go_ref[:, h * GLA_DV:(h + 1) * GLA_DV] = o
        e_rows = jnp.broadcast_to(jnp.exp(bl), (GLA_DK, GLA_DK)).T
        decay = jnp.concatenate([e_rows] * (GLA_DV // GLA_DK), axis=1)
        st_ref[h] = decay * s0 + _dot_tn(k_out.astype(BF16), v.astype(BF16))


def _sample_mixer(qa_s, zs8, la8, kbuf, vbuf, sink_rows, s0, dec_seq, l):
    nb = zs8.shape[0]
    rows8 = zs8.shape[1]
    R = dec_seq * ATT_GROUP
    wb = kbuf.shape[2]
    return pl.pallas_call(
        functools.partial(_sample_kernel, dec_seq=dec_seq),
        grid=(nb,),
        in_specs=[
            pl.BlockSpec((None, ATT_KV_HEADS, R, ATT_HEAD_DIM), lambda b: (b, 0, 0, 0)),
            pl.BlockSpec((None, rows8, Z_WIDTH), lambda b: (b, 0, 0)),
            pl.BlockSpec((None, rows8, GLA_KWIDTH), lambda b: (b, 0, 0)),
            pl.BlockSpec((None, None, wb, KV_WIDTH), lambda b: (l, b, 0, 0)),
            pl.BlockSpec((None, None, wb, KV_WIDTH), lambda b: (l, b, 0, 0)),
            pl.BlockSpec((ATT_KV_HEADS, R, 1), lambda b: (0, 0, 0)),
            pl.BlockSpec((None, None, GLA_HEADS, GLA_DK, GLA_DV), lambda b: (l, b, 0, 0, 0)),
        ],
        out_specs=[
            pl.BlockSpec((None, ATT_KV_HEADS, R, ATT_HEAD_DIM), lambda b: (b, 0, 0, 0)),
            pl.BlockSpec((None, rows8, GLA_WIDTH), lambda b: (b, 0, 0)),
            pl.BlockSpec((None, GLA_HEADS, GLA_DK, GLA_DV), lambda b: (b, 0, 0, 0)),
        ],
        out_shape=[
            jax.ShapeDtypeStruct((nb, ATT_KV_HEADS, R, ATT_HEAD_DIM), F32),
            jax.ShapeDtypeStruct((nb, rows8, GLA_WIDTH), F32),
            jax.ShapeDtypeStruct((nb, GLA_HEADS, GLA_DK, GLA_DV), F32),
        ],
        compiler_params=pltpu.CompilerParams(
            dimension_semantics=("parallel",), vmem_limit_bytes=VMEM_LIMIT),
        name="sample_mixer",
    )(qa_s, zs8, la8, kbuf, vbuf, sink_rows, s0)


def _merge_kernel(*refs):
    x_ref, att_ref, go_ref = refs[:3]
    rg_refs = refs[3:3 + GLA_HEADS]
    ggla_ref, wo_ref, gpost_ref, o_ref = refs[3 + GLA_HEADS:]
    parts = []
    for h in range(GLA_HEADS):
        vs = slice(h * GLA_DV, (h + 1) * GLA_DV)
        rg = rg_refs[h][...]
        parts.append((_rms(go_ref[:, vs], ggla_ref[...]) * (rg * jax.nn.sigmoid(rg))).astype(BF16))
    go = jnp.concatenate(parts, axis=-1)
    y = _dot(att_ref[...], wo_ref[:ATT_WIDTH, :]) + _dot(go, wo_ref[ATT_WIDTH:, :])
    o_ref[...] = x_ref[...] + _rms(y, gpost_ref[...])


def _merge(x, att, go, z, g_gla, w_out, gains, l):
    m = x.shape[0]
    return pl.pallas_call(
        _merge_kernel,
        grid=(m // TM_MERGE,),
        in_specs=(
            [pl.BlockSpec((TM_MERGE, D_MODEL), lambda i: (i, 0)),
             pl.BlockSpec((TM_MERGE, ATT_WIDTH), lambda i: (i, 0)),
             pl.BlockSpec((TM_MERGE, GLA_WIDTH), lambda i: (i, 0))]
            + [pl.BlockSpec((TM_MERGE, GLA_DV), lambda i, h=h: (i, COL_RG // GLA_DV + h)) for h in range(GLA_HEADS)]
            + [pl.BlockSpec((None, 1, GLA_DV), lambda i: (l, 0, 0)),
               pl.BlockSpec((None, ATT_WIDTH + GLA_WIDTH, D_MODEL), lambda i: (l, 0, 0)),
               _gain_spec(l, 3)]
        ),
        out_specs=pl.BlockSpec((TM_MERGE, D_MODEL), lambda i: (i, 0)),
        out_shape=jax.ShapeDtypeStruct((m, D_MODEL), F32),
        compiler_params=pltpu.CompilerParams(
            dimension_semantics=("parallel",), vmem_limit_bytes=VMEM_LIMIT),
        name="merge",
    )(x, att, go, *([z] * GLA_HEADS), g_gla, w_out, gains)


def kernel(x_prompt, x_sample, cache_k_win, cache_v_win, state_gla, meta_tokens, norm_gains,
           w_ffn_gate, w_ffn_up, w_ffn_down, w_in, w_gate_up, b_gate, attn_sinks, gla_norm, w_out):
    batch, seq, _ = x_prompt.shape
    dec_batch, dec_seq, _ = x_sample.shape
    depth = norm_gains.shape[0]
    lp = seq + BLOCK
    nb = lp // BLOCK
    mp = batch * lp
    ms = dec_batch * dec_seq
    m_pad = -(-(mp + ms) // TM) * TM
    wb = cache_k_win.shape[2]
    rows8 = 8
    assert dec_seq <= rows8 and lp % GLA_CHUNK == 0

    head = jnp.concatenate([jnp.zeros((PAD, D_MODEL), F32), meta_tokens.astype(F32)], axis=0)
    pieces = []
    for b in range(batch):
        pieces += [head, x_prompt[b]]
    pieces += [x_sample.reshape(ms, D_MODEL), jnp.zeros((m_pad - mp - ms, D_MODEL), F32)]
    x = jnp.concatenate(pieces, axis=0)

    wg, wu, wd = w_ffn_gate, w_ffn_up, w_ffn_down
    wo = w_out.astype(BF16)
    w_in_t = jnp.swapaxes(w_in, 1, 2)
    col = jnp.arange(Z_WIDTH)
    colscale = jnp.where((col >= COL_QA) & (col < COL_QA + ATT_WIDTH), ATT_HEAD_DIM ** -0.5, 1.0).astype(F32)[None]
    wa_p = jnp.pad(w_gate_up, ((0, 0), (0, LR_PAD - GLA_RANK), (0, 0))).astype(BF16)
    ba = b_gate.reshape(depth, 1, GLA_KWIDTH)
    g_gla = gla_norm.reshape(depth, 1, GLA_DV)
    gains = norm_gains.reshape(depth, 6, 1, D_MODEL)
    kbuf = cache_k_win.reshape(depth, dec_batch, wb, KV_WIDTH)
    vbuf = cache_v_win.reshape(depth, dec_batch, wb, KV_WIDTH)
    tail = jnp.zeros((m_pad - mp - ms, GLA_WIDTH), F32)

    pk, pv, ps, sk, sv, ss = [], [], [], [], [], []
    for l in range(depth):
        x = _ffn(x, gains, wg, wu, wd, l, 0)

        z, la = _proj(x, gains, w_in_t, colscale, wa_p, ba, l)
        att = _attn_prompt(z, attn_sinks[l], batch, nb, m_pad)
        go, st_p = _gla_prompt(z, la, batch, lp, m_pad)

        zs = z[mp:mp + ms].reshape(dec_batch, dec_seq, Z_WIDTH)
        zs8 = jnp.pad(zs, ((0, 0), (0, rows8 - dec_seq), (0, 0)))
        la8 = jnp.pad(la[mp:mp + ms].reshape(dec_batch, dec_seq, GLA_KWIDTH), ((0, 0), (0, rows8 - dec_seq), (0, 0)))
        qa_s = zs[:, :, COL_QA:COL_QA + ATT_WIDTH].reshape(dec_batch, dec_seq, ATT_KV_HEADS, ATT_GROUP, ATT_HEAD_DIM)
        qa_s = qa_s.transpose(0, 2, 1, 3, 4).reshape(dec_batch, ATT_KV_HEADS, dec_seq * ATT_GROUP, ATT_HEAD_DIM)
        sink_rows = jnp.tile(attn_sinks[l].reshape(ATT_KV_HEADS, 1, ATT_GROUP), (1, dec_seq, 1))
        sink_rows = sink_rows.reshape(ATT_KV_HEADS, dec_seq * ATT_GROUP, 1)
        att_s, go_s, st_s = _sample_mixer(qa_s, zs8, la8, kbuf, vbuf, sink_rows, state_gla, dec_seq, l)
        att_s = att_s.reshape(dec_batch, ATT_KV_HEADS, dec_seq, ATT_GROUP, ATT_HEAD_DIM)
        att_s = att_s.transpose(0, 2, 1, 3, 4).reshape(ms, ATT_WIDTH)

        att = lax.dynamic_update_slice(att, jnp.concatenate([att_s, tail], axis=0).astype(BF16), (mp, 0))
        go = lax.dynamic_update_slice(go, jnp.concatenate([go_s[:, :dec_seq].reshape(ms, GLA_WIDTH), tail], axis=0), (mp, 0))
        x = _merge(x, att, go, z, g_gla, wo, gains, l)

        win = [z[b * lp + lp - WINDOW:(b + 1) * lp, COL_KA:COL_VA + KV_WIDTH] for b in range(batch)]
        win = jnp.stack(win).reshape(batch, WINDOW, 2, ATT_KV_HEADS, ATT_HEAD_DIM)
        pk.append(win[:, :, 0])
        pv.append(win[:, :, 1])
        ps.append(st_p.transpose(0, 1, 3, 2))
        kn = zs[:, :, COL_KA:COL_KA + KV_WIDTH].reshape(dec_batch, dec_seq, ATT_KV_HEADS, ATT_HEAD_DIM)
        vn = zs[:, :, COL_VA:COL_VA + KV_WIDTH].reshape(dec_batch, dec_seq, ATT_KV_HEADS, ATT_HEAD_DIM)
        sk.append(jnp.concatenate([cache_k_win[l], kn], axis=1)[:, -wb:])
        sv.append(jnp.concatenate([cache_v_win[l], vn], axis=1)[:, -wb:])
        ss.append(st_s)

        x = _ffn(x, gains, wg, wu, wd, l, 1)

    y_prompt = jnp.stack([x[b * lp + BLOCK:(b + 1) * lp] for b in range(batch)])
    y_sample = x[mp:mp + ms].reshape(dec_batch, dec_seq, D_MODEL)
    return (y_prompt, y_sample, jnp.stack(pk), jnp.stack(pv), jnp.stack(ps),
            jnp.stack(sk), jnp.stack(sv), jnp.stack(ss))
```

```python
import functools

import jax
import jax.numpy as jnp
from jax import lax
from jax.experimental import pallas as pl
from jax.experimental.pallas import tpu as pltpu

F32 = jnp.float32
BF16 = jnp.bfloat16

D_MODEL = 2048
D_FF = 5632
N_META = 16
BLOCK = 128
WINDOW = 128
ATT_HEADS = 16
ATT_KV_HEADS = 2
ATT_GROUP = 8
ATT_HEAD_DIM = 64
ATT_WIDTH = 1024
KV_WIDTH = 128
GLA_HEADS = 4
GLA_DK = 128
GLA_DV = 256
GLA_KWIDTH = 512
GLA_WIDTH = 1024
GLA_RANK = 16
GLA_GATE_NORM = 16.0
GLA_CHUNK = 64
GLA_SUB = 16
EPS = 1e-6
LOG2_E = 1.4426950408889634
PAD = BLOCK - N_META

COL_QA, COL_KA, COL_VA, COL_QG, COL_KG, COL_VG, COL_RG, COL_LR = 0, 1024, 1152, 1280, 1792, 2304, 3328, 4352
COL_TILE = 256
Z_WIDTH = 4608
LR_PAD = 128

VMEM_LIMIT = 56 * 1024 * 1024
VMEM_LIMIT_FFN = 60 * 1024 * 1024

TM = 1072
TM_MERGE = 536
TF = 256
TN = 768
X_PARTS = 4


def _rms(x, gain):
    ms = jnp.mean(x * x, axis=-1, keepdims=True)
    return x * lax.rsqrt(ms + EPS) * gain


def _dot(a, b):
    return jnp.dot(a, b, preferred_element_type=F32)


def _dot_nt(a, b):
    return lax.dot_general(a, b, (((1,), (1,)), ((), ())), preferred_element_type=F32)


def _dot_tn(a, b):
    return lax.dot_general(a, b, (((0,), (0,)), ((), ())), preferred_element_type=F32)


def _row_parts(n_rows, parts):
    units = n_rows // 8
    out, start = [], 0
    for r in range(parts):
        size = (units // parts + (1 if r < units % parts else 0)) * 8
        out.append((start, size))
        start += size
    assert start == n_rows
    return out


def _ffn_kernel(x_hbm, gpre_ref, gpost_ref, wg_hbm, wu_hbm, wd_hbm, o_ref, xn_ref, x_buf, wg_buf, wu_buf, wd_buf,
                sem, xsem, *, l, f):
    i = pl.program_id(0)
    n_tiles = pl.num_programs(0)
    n_chunks = D_FF // TF
    parts = _row_parts(TM, X_PARTS)

    def w_copies(j, slot):
        cols = pl.ds(j * TF, TF)
        return (pltpu.make_async_copy(wg_hbm.at[l, f, :, cols], wg_buf.at[slot], sem.at[0, slot]),
                pltpu.make_async_copy(wu_hbm.at[l, f, :, cols], wu_buf.at[slot], sem.at[1, slot]),
                pltpu.make_async_copy(wd_hbm.at[l, f, cols, :], wd_buf.at[slot], sem.at[2, slot]))

    def x_copy(tile, r):
        start, size = parts[r]
        return pltpu.make_async_copy(x_hbm.at[pl.ds(tile * TM + start, size), :],
                                     x_buf.at[pl.ds(start, size), :], xsem.at[r])

    @pl.when(i == 0)
    def _():
        for r in range(X_PARTS):
            x_copy(0, r).start()
        for c in w_copies(0, 0):
            c.start()

    for r, (start, size) in enumerate(parts):
        x_copy(i, r).wait()
        rows = pl.ds(start, size)
        xn_ref[rows, :] = _rms(x_buf[rows, :], gpre_ref[...]).astype(BF16)
    o_ref[...] = jnp.zeros_like(o_ref)

    def pair(p, carry):
        for slot in range(2):
            j = 2 * p + slot
            for c in w_copies(j, slot):
                c.wait()
            for c in w_copies(lax.rem(j + 1, n_chunks), 1 - slot):
                c.start()
            xn = xn_ref[...]
            g = _dot(xn, wg_buf[slot].astype(BF16))
            u = _dot(xn, wu_buf[slot].astype(BF16))
            h = (g * jax.nn.sigmoid(g) * u).astype(BF16)
            o_ref[...] += _dot(h, wd_buf[slot].astype(BF16))
        return carry

    lax.fori_loop(0, n_chunks // 2, pair, 0)

    for r, (start, size) in enumerate(parts):
        rows = pl.ds(start, size)
        o_ref[rows, :] = x_buf[rows, :] + 0.5 * _rms(o_ref[rows, :], gpost_ref[...])

        @pl.when(i + 1 < n_tiles)
        def _():
            x_copy(i + 1, r).start()

    @pl.when(i == n_tiles - 1)
    def _():
        for c in w_copies(0, 0):
            c.wait()


def _gain_spec(l, idx):
    return pl.BlockSpec((None, None, 1, D_MODEL), lambda *_: (l, idx, 0, 0))


def _ffn(x, gains, wg, wu, wd, l, f):
    m = x.shape[0]
    return pl.pallas_call(
        functools.partial(_ffn_kernel, l=l, f=f),
        grid=(m // TM,),
        in_specs=[
            pl.BlockSpec(memory_space=pl.ANY),
            _gain_spec(l, 4 * f),
            _gain_spec(l, 4 * f + 1),
            pl.BlockSpec(memory_space=pl.ANY),
            pl.BlockSpec(memory_space=pl.ANY),
            pl.BlockSpec(memory_space=pl.ANY),
        ],
        out_specs=pl.BlockSpec((TM, D_MODEL), lambda i: (i, 0)),
        out_shape=jax.ShapeDtypeStruct((m, D_MODEL), F32),
        scratch_shapes=[
            pltpu.VMEM((TM, D_MODEL), BF16),
            pltpu.VMEM((TM, D_MODEL), F32),
            pltpu.VMEM((2, D_MODEL, TF), F32),
            pltpu.VMEM((2, D_MODEL, TF), F32),
            pltpu.VMEM((2, TF, D_MODEL), F32),
            pltpu.SemaphoreType.DMA((3, 2)),
            pltpu.SemaphoreType.DMA((X_PARTS,)),
        ],
        compiler_params=pltpu.CompilerParams(
            dimension_semantics=("arbitrary",), vmem_limit_bytes=VMEM_LIMIT_FFN),
        name="ffn",
    )(x, gains, gains, wg, wu, wd)


def _proj_kernel(x_ref, g_ref, wt_ref, cs_ref, wa_ref, ba_ref, z_ref, la_ref, xn_ref, *, n_cols):
    j = pl.program_id(1)
    last = pl.num_programs(1) - 1
    n_valid = n_cols - (Z_WIDTH - TN)

    @pl.when(j == 0)
    def _():
        xn_ref[...] = _rms(x_ref[...], g_ref[...]).astype(BF16)

    @pl.when(j < last)
    def _():
        z_ref[...] = _dot_nt(xn_ref[...], wt_ref[...].astype(BF16)) * cs_ref[...]

    @pl.when(j == last)
    def _():
        z_ref[:, :n_valid] = _dot_nt(xn_ref[...], wt_ref[:n_valid, :].astype(BF16)) * cs_ref[:, :n_valid]
        z_ref[:, n_valid:] = jnp.zeros((z_ref.shape[0], TN - n_valid), F32)
        lr_off = COL_LR - (Z_WIDTH - TN)
        lr = z_ref[:, lr_off:lr_off + LR_PAD].astype(BF16)
        logit = _dot(lr, wa_ref[...]) + ba_ref[...]
        log_sig = jnp.minimum(logit, 0.0) - jnp.log1p(jnp.exp(-jnp.abs(logit)))
        la_ref[...] = log_sig * (1.0 / GLA_GATE_NORM)


def _proj(x, gains, w_in_t, colscale, wa_p, ba, l):
    m = x.shape[0]
    n_cols = w_in_t.shape[1]
    assert Z_WIDTH - TN < COL_LR and COL_LR + GLA_RANK <= n_cols <= Z_WIDTH
    return pl.pallas_call(
        functools.partial(_proj_kernel, n_cols=n_cols),
        grid=(m // TM, Z_WIDTH // TN),
        in_specs=[
            pl.BlockSpec((TM, D_MODEL), lambda i, j: (i, 0)),
            _gain_spec(l, 2),
            pl.BlockSpec((None, TN, D_MODEL), lambda i, j: (l, j, 0)),
            pl.BlockSpec((1, TN), lambda i, j: (0, j)),
            pl.BlockSpec((None, LR_PAD, GLA_KWIDTH), lambda i, j: (l, 0, 0)),
            pl.BlockSpec((None, 1, GLA_KWIDTH), lambda i, j: (l, 0, 0)),
        ],
        out_specs=[
            pl.BlockSpec((TM, TN), lambda i, j: (i, j)),
            pl.BlockSpec((TM, GLA_KWIDTH), lambda i, j: (i, 0)),
        ],
        out_shape=[
            jax.ShapeDtypeStruct((m, Z_WIDTH), F32),
            jax.ShapeDtypeStruct((m, GLA_KWIDTH), F32),
        ],
        scratch_shapes=[pltpu.VMEM((TM, D_MODEL), BF16)],
        compiler_params=pltpu.CompilerParams(
            dimension_semantics=("parallel", "arbitrary"), vmem_limit_bytes=VMEM_LIMIT),
        name="proj",
    )(x, gains, w_in_t, colscale, wa_p, ba)


def _pair_blockdiag(x128, kv):
    lane = lax.broadcasted_iota(jnp.int32, x128.shape, 1)
    own = jnp.where((lane >= kv * ATT_HEAD_DIM) & (lane < (kv + 1) * ATT_HEAD_DIM), x128, 0.0)
    other = pltpu.roll(own, ATT_HEAD_DIM, axis=1)
    lo, hi = (own, other) if kv == 0 else (other, own)
    return jnp.concatenate([lo, hi], axis=0).astype(BF16)


def _attn_block(i, sink_ref, q_ref, kc_ref, kp_ref, vc_ref, vp_ref, o_ref):
    kb = 2 * BLOCK
    row = lax.broadcasted_iota(jnp.int32, (BLOCK, kb), 0)
    col = lax.broadcasted_iota(jnp.int32, (BLOCK, kb), 1)
    diff = row + BLOCK - col
    key_pos = col + (i - 1) * BLOCK
    mask = (diff >= 0) & (diff <= WINDOW) & (key_pos >= PAD)
    lane = lax.broadcasted_iota(jnp.int32, (BLOCK, 2 * ATT_HEAD_DIM), 1)
    kk = jnp.concatenate([kp_ref[...], kc_ref[...]], axis=0)
    vv = jnp.concatenate([vp_ref[...], vc_ref[...]], axis=0)
    slabs = ATT_GROUP // 2
    for kv in range(ATT_KV_HEADS):
        k2 = _pair_blockdiag(kk, kv)
        v2 = _pair_blockdiag(vv, kv)
        c0 = kv * slabs * 2 * ATT_HEAD_DIM
        q4 = q_ref[:, c0:c0 + slabs * 2 * ATT_HEAD_DIM]
        q4 = jnp.concatenate([q4[:, p * 128:(p + 1) * 128] for p in range(slabs)], axis=0).astype(BF16)
        s = _dot_nt(q4, k2)
        probs, inv = [], []
        for p in range(slabs):
            halves, rden = [], []
            for e in range(2):
                sink = sink_ref[kv * ATT_GROUP + 2 * p + e]
                sp = jnp.where(mask, s[p * BLOCK:(p + 1) * BLOCK, e * kb:(e + 1) * kb], -jnp.inf)
                mx = jnp.maximum(jnp.max(sp, axis=-1, keepdims=True), sink)
                pe = jnp.exp(sp - mx)
                rden.append(1.0 / (jnp.sum(pe, axis=-1, keepdims=True) + jnp.exp(sink - mx)))
                halves.append(pe.astype(BF16))
            probs.append(jnp.concatenate(halves, axis=1))
            inv.append(jnp.where(lane < ATT_HEAD_DIM, rden[0], rden[1]))
        o = _dot(jnp.concatenate(probs, axis=0), v2)
        for p in range(slabs):
            o_ref[:, c0 + p * 128:c0 + (p + 1) * 128] = (o[p * BLOCK:(p + 1) * BLOCK] * inv[p]).astype(o_ref.dtype)


def _cumsum_rows(la, tri):
    hi = la.astype(BF16)
    r1 = la - hi.astype(F32)
    mid = r1.astype(BF16)
    lo = (r1 - mid.astype(F32)).astype(BF16)
    return _dot(tri, hi) + _dot(tri, mid) + _dot(tri, lo)


def _gla_chunk(c, rows, q_refs, k_refs, v_refs, la_ref, o_ref, state_ref, maybe_first):
    per = COL_TILE // GLA_DK
    C = GLA_CHUNK

    rowc = lax.broadcasted_iota(jnp.int32, (C, C), 0)
    colc = lax.broadcasted_iota(jnp.int32, (C, C), 1)
    tri = (rowc >= colc).astype(BF16)
    b_all = _cumsum_rows(la_ref[rows, :], tri) * LOG2_E

    pos = c * C + lax.broadcasted_iota(jnp.int32, (C, 1), 0)
    valid = (pos >= PAD).astype(F32)

    lane = lax.broadcasted_iota(jnp.int32, (GLA_SUB, C), 1)
    rsub = lax.broadcasted_iota(jnp.int32, (GLA_SUB, C), 0)
    scale = GLA_DK ** -0.5

    for h in range(GLA_HEADS):
        ks = slice(h * GLA_DK, (h + 1) * GLA_DK)
        vs = slice(h * GLA_DV, (h + 1) * GLA_DV)
        sub = slice((h % per) * GLA_DK, (h % per + 1) * GLA_DK)
        q = q_refs[h // per][rows, sub] * scale
        k = k_refs[h // per][rows, sub] * valid
        v = v_refs[h][rows, :].astype(BF16)
        bh = b_all[:, ks]
        bl = bh[C - 1:C, :]
        st = state_ref[h]
        if maybe_first:
            st = jnp.where(c == 0, 0.0, st)

        o = _dot_nt((q * jnp.exp2(bh)).astype(BF16), st.astype(BF16))

        blocks = []
        for i in range(C // GLA_SUB):
            lo_r = i * GLA_SUB
            qb = q[lo_r:lo_r + GLA_SUB]
            bb = bh[lo_r:lo_r + GLA_SUB]
            w = jnp.zeros((GLA_SUB, C), F32)
            for s in range(GLA_SUB):
                r = lo_r + s
                zed = qb * k[r:r + 1] * jnp.exp2(bb - bh[r:r + 1])
                w = jnp.where(lane == r, jnp.sum(zed, axis=-1, keepdims=True), w)
            w = jnp.where(lane <= rsub + lo_r, w, 0.0)
            if i > 0:
                ref_b = bh[lo_r - 1:lo_r]
                qi = (qb * jnp.exp2(bb - ref_b)).astype(BF16)
                kj = (k * jnp.exp2(jnp.minimum(ref_b - bh, 0.0))).astype(BF16)
                w = jnp.where(lane < lo_r, _dot_nt(qi, kj), w)
            blocks.append(w)
        a = jnp.concatenate(blocks, axis=0).astype(BF16)
        o_ref[rows, vs] = o + _dot(a, v)

        k_out = (k * jnp.exp2(bl - bh)).astype(BF16)
        state_ref[h] = st * jnp.exp2(bl) + _dot_tn(v, k_out)


def _mixer_kernel(sink_ref, *refs, nb, n_blocks):
    nq = GLA_KWIDTH // COL_TILE
    attn_in, rest = refs[:5], refs[5:]
    q_refs, k_refs, v_refs = rest[:nq], rest[nq:2 * nq], rest[2 * nq:2 * nq + GLA_HEADS]
    la_ref, att_ref, go_ref, st_ref, state_ref = rest[2 * nq + GLA_HEADS:]
    step = pl.program_id(0)

    @pl.when(step < n_blocks)
    def _():
        i = lax.rem(step, nb)
        _attn_block(i, sink_ref, *attn_in, att_ref)
        per_block = BLOCK // GLA_CHUNK
        for half in range(per_block):
            _gla_chunk(i * per_block + half, pl.ds(half * GLA_CHUNK, GLA_CHUNK), q_refs, k_refs, v_refs, la_ref,
                       go_ref, state_ref, maybe_first=(half == 0))
        st_ref[...] = state_ref[...]

    @pl.when(step >= n_blocks)
    def _():
        att_ref[...] = jnp.zeros_like(att_ref)
        go_ref[...] = jnp.zeros_like(go_ref)


def _mixer_prompt(z, la, sinks, batch, nb, m_rows):
    n_blocks = batch * nb
    nq = GLA_KWIDTH // COL_TILE

    def cur(c):
        return lambda s, sink: (jnp.minimum(s, n_blocks - 1), c)

    def prev(c):
        return lambda s, sink: (jnp.maximum(jnp.minimum(s, n_blocks - 1) - 1, 0), c)

    def rows(c):
        return lambda s, sink: (s, c)

    return pl.pallas_call(
        functools.partial(_mixer_kernel, nb=nb, n_blocks=n_blocks),
        grid_spec=pltpu.PrefetchScalarGridSpec(
            num_scalar_prefetch=1,
            grid=(m_rows // BLOCK,),
            in_specs=(
                [pl.BlockSpec((BLOCK, ATT_WIDTH), cur(COL_QA // ATT_WIDTH)),
                 pl.BlockSpec((BLOCK, KV_WIDTH), cur(COL_KA // KV_WIDTH)),
                 pl.BlockSpec((BLOCK, KV_WIDTH), prev(COL_KA // KV_WIDTH)),
                 pl.BlockSpec((BLOCK, KV_WIDTH), cur(COL_VA // KV_WIDTH)),
                 pl.BlockSpec((BLOCK, KV_WIDTH), prev(COL_VA // KV_WIDTH))]
                + [pl.BlockSpec((BLOCK, COL_TILE), rows(COL_QG // COL_TILE + t)) for t in range(nq)]
                + [pl.BlockSpec((BLOCK, COL_TILE), rows(COL_KG // COL_TILE + t)) for t in range(nq)]
                + [pl.BlockSpec((BLOCK, GLA_DV), rows(COL_VG // GLA_DV + t)) for t in range(GLA_HEADS)]
                + [pl.BlockSpec((BLOCK, GLA_KWIDTH), rows(0))]
            ),
            out_specs=[
                pl.BlockSpec((BLOCK, ATT_WIDTH), rows(0)),
                pl.BlockSpec((BLOCK, GLA_WIDTH), rows(0)),
                pl.BlockSpec((None, GLA_HEADS, GLA_DV, GLA_DK),
                             lambda s, sink: (jnp.minimum(s // nb, batch - 1), 0, 0, 0)),
            ],
            scratch_shapes=[pltpu.VMEM((GLA_HEADS, GLA_DV, GLA_DK), F32)],
        ),
        out_shape=[
            jax.ShapeDtypeStruct((m_rows, ATT_WIDTH), BF16),
            jax.ShapeDtypeStruct((m_rows, GLA_WIDTH), F32),
            jax.ShapeDtypeStruct((batch, GLA_HEADS, GLA_DV, GLA_DK), F32),
        ],
        compiler_params=pltpu.CompilerParams(
            dimension_semantics=("arbitrary",), vmem_limit_bytes=VMEM_LIMIT),
        name="mixer_prompt",
    )(sinks, *([z] * (5 + 2 * nq + GLA_HEADS)), la)


def _sample_kernel(qa_ref, zs_ref, la_ref, kbuf_ref, vbuf_ref, sink_ref, s0_ref,
                   att_ref, go_ref, st_ref, *, dec_seq):
    S = dec_seq
    R = S * ATT_GROUP
    t_row = lax.broadcasted_iota(jnp.int32, (R, 1), 0) // ATT_GROUP
    c_idx = lax.broadcasted_iota(jnp.int32, (R, kbuf_ref.shape[0]), 1)
    mask_c = c_idx >= t_row
    for kv in range(ATT_KV_HEADS):
        ks = slice(kv * ATT_HEAD_DIM, (kv + 1) * ATT_HEAD_DIM)
        q = qa_ref[kv]
        sink = sink_ref[kv]
        sc = _dot_nt(q.astype(BF16), kbuf_ref[:, ks].astype(BF16))
        sc = jnp.where(mask_c, sc, -jnp.inf)
        mx = jnp.maximum(jnp.max(sc, axis=-1, keepdims=True), sink)
        kn = zs_ref[:, COL_KA + kv * ATT_HEAD_DIM:COL_KA + (kv + 1) * ATT_HEAD_DIM]
        vn = zs_ref[:, COL_VA + kv * ATT_HEAD_DIM:COL_VA + (kv + 1) * ATT_HEAD_DIM]
        sn = []
        for s in range(S):
            v = jnp.sum(q * kn[s:s + 1], axis=-1, keepdims=True)
            v = jnp.where(t_row >= s, v, -jnp.inf)
            sn.append(v)
            mx = jnp.maximum(mx, v)
        pc = jnp.exp(sc - mx)
        den = jnp.sum(pc, axis=-1, keepdims=True) + jnp.exp(sink - mx)
        o = _dot(pc.astype(BF16), vbuf_ref[:, ks].astype(BF16))
        for s in range(S):
            pn = jnp.exp(sn[s] - mx)
            den = den + pn
            o = o + pn * vn[s:s + 1]
        att_ref[kv] = o / den

    rows8 = zs_ref.shape[0]
    rid = lax.broadcasted_iota(jnp.int32, (rows8, 1), 0)
    la = la_ref[...]
    run = la[0:1]
    b_all = jnp.where(rid == 0, run, 0.0)
    for t in range(1, S):
        run = run + la[t:t + 1]
        b_all = jnp.where(rid == t, run, b_all)
    real = (rid < S).astype(F32)
    scale = GLA_DK ** -0.5
    for h in range(GLA_HEADS):
        ks = slice(h * GLA_DK, (h + 1) * GLA_DK)
        q = zs_ref[:, COL_QG + h * GLA_DK:COL_QG + (h + 1) * GLA_DK] * scale
        k = zs_ref[:, COL_KG + h * GLA_DK:COL_KG + (h + 1) * GLA_DK]
        v = zs_ref[:, COL_VG + h * GLA_DV:COL_VG + (h + 1) * GLA_DV]
        bh = b_all[:, ks] * real
        bl = bh[S - 1:S]
        s0 = s0_ref[h]
        k_out = k * jnp.exp(jnp.minimum(bl - bh, 0.0)) * real
        q_in = q * jnp.exp(bh) * real
        o = _dot(q_in.astype(BF16), s0.astype(BF16))
        for s in range(S):
            zed = q * k[s:s + 1] * jnp.exp(jnp.minimum(bh - bh[s:s + 1], 0.0))
            w = jnp.sum(zed, axis=-1, keepdims=True)
            w = jnp.where((rid >= s) & (rid < S), w, 0.0)
            o = o + w * v[s:s + 1]
        go_ref[:, h * GLA_DV:(h + 1) * GLA_DV] = o
        e_rows = jnp.broadcast_to(jnp.exp(bl), (GLA_DK, GLA_DK)).T
        decay = jnp.concatenate([e_rows] * (GLA_DV // GLA_DK), axis=1)
        st_ref[h] = decay * s0 + _dot_tn(k_out.astype(BF16), v.astype(BF16))


def _sample_mixer(qa_s, zs8, la8, kbuf, vbuf, sink_rows, s0, dec_seq, l):
    nb = zs8.shape[0]
    rows8 = zs8.shape[1]
    R = dec_seq * ATT_GROUP
    wb = kbuf.shape[2]
    return pl.pallas_call(
        functools.partial(_sample_kernel, dec_seq=dec_seq),
        grid=(nb,),
        in_specs=[
            pl.BlockSpec((None, ATT_KV_HEADS, R, ATT_HEAD_DIM), lambda b: (b, 0, 0, 0)),
            pl.BlockSpec((None, rows8, Z_WIDTH), lambda b: (b, 0, 0)),
            pl.BlockSpec((None, rows8, GLA_KWIDTH), lambda b: (b, 0, 0)),
            pl.BlockSpec((None, None, wb, KV_WIDTH), lambda b: (l, b, 0, 0)),
            pl.BlockSpec((None, None, wb, KV_WIDTH), lambda b: (l, b, 0, 0)),
            pl.BlockSpec((ATT_KV_HEADS, R, 1), lambda b: (0, 0, 0)),
            pl.BlockSpec((None, None, GLA_HEADS, GLA_DK, GLA_DV), lambda b: (l, b, 0, 0, 0)),
        ],
        out_specs=[
            pl.BlockSpec((None, ATT_KV_HEADS, R, ATT_HEAD_DIM), lambda b: (b, 0, 0, 0)),
            pl.BlockSpec((None, rows8, GLA_WIDTH), lambda b: (b, 0, 0)),
            pl.BlockSpec((None, GLA_HEADS, GLA_DK, GLA_DV), lambda b: (b, 0, 0, 0)),
        ],
        out_shape=[
            jax.ShapeDtypeStruct((nb, ATT_KV_HEADS, R, ATT_HEAD_DIM), F32),
            jax.ShapeDtypeStruct((nb, rows8, GLA_WIDTH), F32),
            jax.ShapeDtypeStruct((nb, GLA_HEADS, GLA_DK, GLA_DV), F32),
        ],
        compiler_params=pltpu.CompilerParams(
            dimension_semantics=("parallel",), vmem_limit_bytes=VMEM_LIMIT),
        name="sample_mixer",
    )(qa_s, zs8, la8, kbuf, vbuf, sink_rows, s0)


def _merge_kernel(*refs):
    x_ref, att_ref, go_ref = refs[:3]
    rg_refs = refs[3:3 + GLA_HEADS]
    ggla_ref, wo_ref, gpost_ref, o_ref = refs[3 + GLA_HEADS:]
    parts = []
    for h in range(GLA_HEADS):
        vs = slice(h * GLA_DV, (h + 1) * GLA_DV)
        rg = rg_refs[h][...]
        parts.append((_rms(go_ref[:, vs], ggla_ref[...]) * (rg * jax.nn.sigmoid(rg))).astype(BF16))
    go = jnp.concatenate(parts, axis=-1)
    y = _dot(att_ref[...], wo_ref[:ATT_WIDTH, :]) + _dot(go, wo_ref[ATT_WIDTH:, :])
    o_ref[...] = x_ref[...] + _rms(y, gpost_ref[...])


def _merge(x, att, go, z, g_gla, w_out, gains, l):
    m = x.shape[0]
    return pl.pallas_call(
        _merge_kernel,
        grid=(m // TM_MERGE,),
        in_specs=(
            [pl.BlockSpec((TM_MERGE, D_MODEL), lambda i: (i, 0)),
             pl.BlockSpec((TM_MERGE, ATT_WIDTH), lambda i: (i, 0)),
             pl.BlockSpec((TM_MERGE, GLA_WIDTH), lambda i: (i, 0))]
            + [pl.BlockSpec((TM_MERGE, GLA_DV), lambda i, h=h: (i, COL_RG // GLA_DV + h)) for h in range(GLA_HEADS)]
            + [pl.BlockSpec((None, 1, GLA_DV), lambda i: (l, 0, 0)),
               pl.BlockSpec((None, ATT_WIDTH + GLA_WIDTH, D_MODEL), lambda i: (l, 0, 0)),
               _gain_spec(l, 3)]
        ),
        out_specs=pl.BlockSpec((TM_MERGE, D_MODEL), lambda i: (i, 0)),
        out_shape=jax.ShapeDtypeStruct((m, D_MODEL), F32),
        compiler_params=pltpu.CompilerParams(
            dimension_semantics=("parallel",), vmem_limit_bytes=VMEM_LIMIT),
        name="merge",
    )(x, att, go, *([z] * GLA_HEADS), g_gla, w_out, gains)


def kernel(x_prompt, x_sample, cache_k_win, cache_v_win, state_gla, meta_tokens, norm_gains,
           w_ffn_gate, w_ffn_up, w_ffn_down, w_in, w_gate_up, b_gate, attn_sinks, gla_norm, w_out):
    batch, seq, _ = x_prompt.shape
    dec_batch, dec_seq, _ = x_sample.shape
    depth = norm_gains.shape[0]
    lp = seq + BLOCK
    nb = lp // BLOCK
    mp = batch * lp
    ms = dec_batch * dec_seq
    m_pad = -(-(mp + ms) // TM) * TM
    wb = cache_k_win.shape[2]
    rows8 = 8
    assert dec_seq <= rows8 and lp % GLA_CHUNK == 0

    head = jnp.concatenate([jnp.zeros((PAD, D_MODEL), F32), meta_tokens.astype(F32)], axis=0)
    pieces = []
    for b in range(batch):
        pieces += [head, x_prompt[b]]
    pieces += [x_sample.reshape(ms, D_MODEL), jnp.zeros((m_pad - mp - ms, D_MODEL), F32)]
    x = jnp.concatenate(pieces, axis=0)

    wg, wu, wd = w_ffn_gate, w_ffn_up, w_ffn_down
    wo = w_out.astype(BF16)
    w_in_t = jnp.swapaxes(w_in, 1, 2)
    col = jnp.arange(Z_WIDTH)
    colscale = jnp.where((col >= COL_QA) & (col < COL_QA + ATT_WIDTH), ATT_HEAD_DIM ** -0.5, 1.0).astype(F32)[None]
    wa_p = jnp.pad(w_gate_up, ((0, 0), (0, LR_PAD - GLA_RANK), (0, 0))).astype(BF16)
    ba = b_gate.reshape(depth, 1, GLA_KWIDTH)
    g_gla = gla_norm.reshape(depth, 1, GLA_DV)
    gains = norm_gains.reshape(depth, 6, 1, D_MODEL)
    kbuf = cache_k_win.reshape(depth, dec_batch, wb, KV_WIDTH)
    vbuf = cache_v_win.reshape(depth, dec_batch, wb, KV_WIDTH)
    tail = jnp.zeros((m_pad - mp - ms, GLA_WIDTH), F32)

    pk, pv, ps, sk, sv, ss = [], [], [], [], [], []
    for l in range(depth):
        x = _ffn(x, gains, wg, wu, wd, l, 0)

        z, la = _proj(x, gains, w_in_t, colscale, wa_p, ba, l)
        att, go, st_p = _mixer_prompt(z, la, attn_sinks[l], batch, nb, m_pad)

        zs = z[mp:mp + ms].reshape(dec_batch, dec_seq, Z_WIDTH)
        zs8 = jnp.pad(zs, ((0, 0), (0, rows8 - dec_seq), (0, 0)))
        la8 = jnp.pad(la[mp:mp + ms].reshape(dec_batch, dec_seq, GLA_KWIDTH), ((0, 0), (0, rows8 - dec_seq), (0, 0)))
        qa_s = zs[:, :, COL_QA:COL_QA + ATT_WIDTH].reshape(dec_batch, dec_seq, ATT_KV_HEADS, ATT_GROUP, ATT_HEAD_DIM)
        qa_s = qa_s.transpose(0, 2, 1, 3, 4).reshape(dec_batch, ATT_KV_HEADS, dec_seq * ATT_GROUP, ATT_HEAD_DIM)
        sink_rows = jnp.tile(attn_sinks[l].reshape(ATT_KV_HEADS, 1, ATT_GROUP), (1, dec_seq, 1))
        sink_rows = sink_rows.reshape(ATT_KV_HEADS, dec_seq * ATT_GROUP, 1)
        att_s, go_s, st_s = _sample_mixer(qa_s, zs8, la8, kbuf, vbuf, sink_rows, state_gla, dec_seq, l)
        att_s = att_s.reshape(dec_batch, ATT_KV_HEADS, dec_seq, ATT_GROUP, ATT_HEAD_DIM)
        att_s = att_s.transpose(0, 2, 1, 3, 4).reshape(ms, ATT_WIDTH)

        att = lax.dynamic_update_slice(att, jnp.concatenate([att_s, tail], axis=0).astype(BF16), (mp, 0))
        go = lax.dynamic_update_slice(go, jnp.concatenate([go_s[:, :dec_seq].reshape(ms, GLA_WIDTH), tail], axis=0), (mp, 0))
        x = _merge(x, att, go, z, g_gla, wo, gains, l)

        win = [z[b * lp + lp - WINDOW:(b + 1) * lp, COL_KA:COL_VA + KV_WIDTH] for b in range(batch)]
        win = jnp.stack(win).reshape(batch, WINDOW, 2, ATT_KV_HEADS, ATT_HEAD_DIM)
        pk.append(win[:, :, 0])
        pv.append(win[:, :, 1])
        ps.append(st_p.transpose(0, 1, 3, 2))
        kn = zs[:, :, COL_KA:COL_KA + KV_WIDTH].reshape(dec_batch, dec_seq, ATT_KV_HEADS, ATT_HEAD_DIM)
        vn = zs[:, :, COL_VA:COL_VA + KV_WIDTH].reshape(dec_batch, dec_seq, ATT_KV_HEADS, ATT_HEAD_DIM)
        sk.append(jnp.concatenate([cache_k_win[l], kn], axis=1)[:, -wb:])
        sv.append(jnp.concatenate([cache_v_win[l], vn], axis=1)[:, -wb:])
        ss.append(st_s)

        x = _ffn(x, gains, wg, wu, wd, l, 1)

    y_prompt = jnp.stack([x[b * lp + BLOCK:(b + 1) * lp] for b in range(batch)])
    y_sample = x[mp:mp + ms].reshape(dec_batch, dec_seq, D_MODEL)
    return (y_prompt, y_sample, jnp.stack(pk), jnp.stack(pv), jnp.stack(ps),
            jnp.stack(sk), jnp.stack(sv), jnp.stack(ss))
```

```python
import functools

import jax
import jax.numpy as jnp
from jax import lax
from jax.experimental import pallas as pl
from jax.experimental.pallas import tpu as pltpu

F32 = jnp.float32
BF16 = jnp.bfloat16

D_MODEL = 2048
D_FF = 5632
N_META = 16
BLOCK = 128
WINDOW = 128
ATT_HEADS = 16
ATT_KV_HEADS = 2
ATT_GROUP = 8
ATT_HEAD_DIM = 64
ATT_WIDTH = 1024
KV_WIDTH = 128
GLA_HEADS = 4
GLA_DK = 128
GLA_DV = 256
GLA_KWIDTH = 512
GLA_WIDTH = 1024
GLA_RANK = 16
GLA_GATE_NORM = 16.0
GLA_CHUNK = 64
GLA_SUB = 8
EPS = 1e-6
LOG2_E = 1.4426950408889634
PAD = BLOCK - N_META

COL_QA, COL_KA, COL_VA, COL_QG, COL_KG, COL_VG, COL_RG, COL_LR = 0, 1024, 1152, 1280, 1792, 2304, 3328, 4352
COL_TILE = 256
Z_WIDTH = 4608
LR_PAD = 128

VMEM_LIMIT = 56 * 1024 * 1024
VMEM_LIMIT_FFN = 60 * 1024 * 1024

TM = 1072
TM_MERGE = 536
TF = 256
TN = 768
X_PARTS = 4
SAMPLE_TILE = 4


def _rms(x, gain):
    ms = jnp.mean(x * x, axis=-1, keepdims=True)
    return x * lax.rsqrt(ms + EPS) * gain


def _dot(a, b):
    return jnp.dot(a, b, preferred_element_type=F32)


def _dot_nt(a, b):
    return lax.dot_general(a, b, (((1,), (1,)), ((), ())), preferred_element_type=F32)


def _dot_tn(a, b):
    return lax.dot_general(a, b, (((0,), (0,)), ((), ())), preferred_element_type=F32)


def _row_parts(n_rows, parts):
    units = n_rows // 8
    out, start = [], 0
    for r in range(parts):
        size = (units // parts + (1 if r < units % parts else 0)) * 8
        out.append((start, size))
        start += size
    assert start == n_rows
    return out


def _ffn_kernel(x_hbm, gpre_ref, gpost_ref, wg_hbm, wu_hbm, wd_hbm, o_ref, xn_ref, x_buf, wg_buf, wu_buf, wd_buf,
                sem, xsem, *, l, f):
    i = pl.program_id(0)
    n_tiles = pl.num_programs(0)
    n_chunks = D_FF // TF
    parts = _row_parts(TM, X_PARTS)

    def w_copies(j, slot):
        cols = pl.ds(j * TF, TF)
        return (pltpu.make_async_copy(wg_hbm.at[l, f, :, cols], wg_buf.at[slot], sem.at[0, slot]),
                pltpu.make_async_copy(wu_hbm.at[l, f, :, cols], wu_buf.at[slot], sem.at[1, slot]),
                pltpu.make_async_copy(wd_hbm.at[l, f, cols, :], wd_buf.at[slot], sem.at[2, slot]))

    def x_copy(tile, r):
        start, size = parts[r]
        return pltpu.make_async_copy(x_hbm.at[pl.ds(tile * TM + start, size), :],
                                     x_buf.at[pl.ds(start, size), :], xsem.at[r])

    @pl.when(i == 0)
    def _():
        for r in range(X_PARTS):
            x_copy(0, r).start()
        for c in w_copies(0, 0):
            c.start()

    for r, (start, size) in enumerate(parts):
        x_copy(i, r).wait()
        rows = pl.ds(start, size)
        xn_ref[rows, :] = _rms(x_buf[rows, :], gpre_ref[...]).astype(BF16)
    o_ref[...] = jnp.zeros_like(o_ref)

    def pair(p, carry):
        for slot in range(2):
            j = 2 * p + slot
            for c in w_copies(j, slot):
                c.wait()
            for c in w_copies(lax.rem(j + 1, n_chunks), 1 - slot):
                c.start()
            xn = xn_ref[...]
            g = _dot(xn, wg_buf[slot].astype(BF16))
            u = _dot(xn, wu_buf[slot].astype(BF16))
            h = (g * jax.nn.sigmoid(g) * u).astype(BF16)
            o_ref[...] += _dot(h, wd_buf[slot].astype(BF16))
        return carry

    lax.fori_loop(0, n_chunks // 2, pair, 0)

    for r, (start, size) in enumerate(parts):
        rows = pl.ds(start, size)
        o_ref[rows, :] = x_buf[rows, :] + 0.5 * _rms(o_ref[rows, :], gpost_ref[...])

        @pl.when(i + 1 < n_tiles)
        def _():
            x_copy(i + 1, r).start()

    @pl.when(i == n_tiles - 1)
    def _():
        for c in w_copies(0, 0):
            c.wait()


def _gain_spec(l, idx):
    return pl.BlockSpec((None, None, 1, D_MODEL), lambda *_: (l, idx, 0, 0))


def _ffn(x, gains, wg, wu, wd, l, f):
    m = x.shape[0]
    return pl.pallas_call(
        functools.partial(_ffn_kernel, l=l, f=f),
        grid=(m // TM,),
        in_specs=[
            pl.BlockSpec(memory_space=pl.ANY),
            _gain_spec(l, 4 * f),
            _gain_spec(l, 4 * f + 1),
            pl.BlockSpec(memory_space=pl.ANY),
            pl.BlockSpec(memory_space=pl.ANY),
            pl.BlockSpec(memory_space=pl.ANY),
        ],
        out_specs=pl.BlockSpec((TM, D_MODEL), lambda i: (i, 0)),
        out_shape=jax.ShapeDtypeStruct((m, D_MODEL), F32),
        scratch_shapes=[
            pltpu.VMEM((TM, D_MODEL), BF16),
            pltpu.VMEM((TM, D_MODEL), F32),
            pltpu.VMEM((2, D_MODEL, TF), F32),
            pltpu.VMEM((2, D_MODEL, TF), F32),
            pltpu.VMEM((2, TF, D_MODEL), F32),
            pltpu.SemaphoreType.DMA((3, 2)),
            pltpu.SemaphoreType.DMA((X_PARTS,)),
        ],
        compiler_params=pltpu.CompilerParams(
            dimension_semantics=("arbitrary",), vmem_limit_bytes=VMEM_LIMIT_FFN),
        name="ffn",
    )(x, gains, gains, wg, wu, wd)


def _proj_kernel(x_ref, g_ref, wt_ref, cs_ref, wa_ref, ba_ref, z_ref, la_ref, xn_ref, *, n_cols):
    j = pl.program_id(1)
    last = pl.num_programs(1) - 1
    n_valid = n_cols - (Z_WIDTH - TN)

    @pl.when(j == 0)
    def _():
        xn_ref[...] = _rms(x_ref[...], g_ref[...]).astype(BF16)

    @pl.when(j < last)
    def _():
        z_ref[...] = _dot_nt(xn_ref[...], wt_ref[...].astype(BF16)) * cs_ref[...]

    @pl.when(j == last)
    def _():
        z_ref[:, :n_valid] = _dot_nt(xn_ref[...], wt_ref[:n_valid, :].astype(BF16)) * cs_ref[:, :n_valid]
        z_ref[:, n_valid:] = jnp.zeros((z_ref.shape[0], TN - n_valid), F32)
        lr_off = COL_LR - (Z_WIDTH - TN)
        lr = z_ref[:, lr_off:lr_off + LR_PAD].astype(BF16)
        logit = _dot(lr, wa_ref[...]) + ba_ref[...]
        log_sig = jnp.minimum(logit, 0.0) - jnp.log1p(jnp.exp(-jnp.abs(logit)))
        la_ref[...] = log_sig * (1.0 / GLA_GATE_NORM)


def _proj(x, gains, w_in_t, colscale, wa_p, ba, l):
    m = x.shape[0]
    n_cols = w_in_t.shape[1]
    assert Z_WIDTH - TN < COL_LR and COL_LR + GLA_RANK <= n_cols <= Z_WIDTH
    return pl.pallas_call(
        functools.partial(_proj_kernel, n_cols=n_cols),
        grid=(m // TM, Z_WIDTH // TN),
        in_specs=[
            pl.BlockSpec((TM, D_MODEL), lambda i, j: (i, 0)),
            _gain_spec(l, 2),
            pl.BlockSpec((None, TN, D_MODEL), lambda i, j: (l, j, 0)),
            pl.BlockSpec((1, TN), lambda i, j: (0, j)),
            pl.BlockSpec((None, LR_PAD, GLA_KWIDTH), lambda i, j: (l, 0, 0)),
            pl.BlockSpec((None, 1, GLA_KWIDTH), lambda i, j: (l, 0, 0)),
        ],
        out_specs=[
            pl.BlockSpec((TM, TN), lambda i, j: (i, j)),
            pl.BlockSpec((TM, GLA_KWIDTH), lambda i, j: (i, 0)),
        ],
        out_shape=[
            jax.ShapeDtypeStruct((m, Z_WIDTH), F32),
            jax.ShapeDtypeStruct((m, GLA_KWIDTH), F32),
        ],
        scratch_shapes=[pltpu.VMEM((TM, D_MODEL), BF16)],
        compiler_params=pltpu.CompilerParams(
            dimension_semantics=("parallel", "arbitrary"), vmem_limit_bytes=VMEM_LIMIT),
        name="proj",
    )(x, gains, w_in_t, colscale, wa_p, ba)


def _pair_blockdiag(x128, kv):
    lane = lax.broadcasted_iota(jnp.int32, x128.shape, 1)
    own = jnp.where((lane >= kv * ATT_HEAD_DIM) & (lane < (kv + 1) * ATT_HEAD_DIM), x128, 0.0)
    other = pltpu.roll(own, ATT_HEAD_DIM, axis=1)
    lo, hi = (own, other) if kv == 0 else (other, own)
    return jnp.concatenate([lo, hi], axis=0).astype(BF16)


def _attn_block(i, sink_ref, q_ref, kc_ref, kp_ref, vc_ref, vp_ref, o_ref):
    kb = 2 * BLOCK
    row = lax.broadcasted_iota(jnp.int32, (BLOCK, kb), 0)
    col = lax.broadcasted_iota(jnp.int32, (BLOCK, kb), 1)
    diff = row + BLOCK - col
    key_pos = col + (i - 1) * BLOCK
    mask = (diff >= 0) & (diff <= WINDOW) & (key_pos >= PAD)
    lane = lax.broadcasted_iota(jnp.int32, (BLOCK, 2 * ATT_HEAD_DIM), 1)
    kk = jnp.concatenate([kp_ref[...], kc_ref[...]], axis=0)
    vv = jnp.concatenate([vp_ref[...], vc_ref[...]], axis=0)
    slabs = ATT_GROUP // 2
    for kv in range(ATT_KV_HEADS):
        k2 = _pair_blockdiag(kk, kv)
        v2 = _pair_blockdiag(vv, kv)
        c0 = kv * slabs * 2 * ATT_HEAD_DIM
        q4 = q_ref[:, c0:c0 + slabs * 2 * ATT_HEAD_DIM]
        q4 = jnp.concatenate([q4[:, p * 128:(p + 1) * 128] for p in range(slabs)], axis=0).astype(BF16)
        s = _dot_nt(q4, k2)
        probs, inv = [], []
        for p in range(slabs):
            halves, rden = [], []
            for e in range(2):
                sink = sink_ref[kv * ATT_GROUP + 2 * p + e]
                sp = jnp.where(mask, s[p * BLOCK:(p + 1) * BLOCK, e * kb:(e + 1) * kb], -jnp.inf)
                mx = jnp.maximum(jnp.max(sp, axis=-1, keepdims=True), sink)
                pe = jnp.exp(sp - mx)
                rden.append(1.0 / (jnp.sum(pe, axis=-1, keepdims=True) + jnp.exp(sink - mx)))
                halves.append(pe.astype(BF16))
            probs.append(jnp.concatenate(halves, axis=1))
            inv.append(jnp.where(lane < ATT_HEAD_DIM, rden[0], rden[1]))
        o = _dot(jnp.concatenate(probs, axis=0), v2)
        for p in range(slabs):
            o_ref[:, c0 + p * 128:c0 + (p + 1) * 128] = (o[p * BLOCK:(p + 1) * BLOCK] * inv[p]).astype(o_ref.dtype)


def _cumsum_rows(la, tri):
    hi = la.astype(BF16)
    r1 = la - hi.astype(F32)
    mid = r1.astype(BF16)
    lo = (r1 - mid.astype(F32)).astype(BF16)
    return _dot(tri, hi) + _dot(tri, mid) + _dot(tri, lo)


def _gla_chunk(c, rows, q_refs, k_refs, v_refs, la_ref, o_ref, state_ref, maybe_first):
    per = COL_TILE // GLA_DK
    C = GLA_CHUNK

    rowc = lax.broadcasted_iota(jnp.int32, (C, C), 0)
    colc = lax.broadcasted_iota(jnp.int32, (C, C), 1)
    tri = (rowc >= colc).astype(BF16)
    b_all = _cumsum_rows(la_ref[rows, :], tri) * LOG2_E

    pos = c * C + lax.broadcasted_iota(jnp.int32, (C, 1), 0)
    valid = (pos >= PAD).astype(F32)

    lane = lax.broadcasted_iota(jnp.int32, (GLA_SUB, C), 1)
    rsub = lax.broadcasted_iota(jnp.int32, (GLA_SUB, C), 0)
    scale = GLA_DK ** -0.5

    heads = range(GLA_HEADS)
    q, k, bh = [], [], []
    for h in heads:
        sub = slice((h % per) * GLA_DK, (h % per + 1) * GLA_DK)
        q.append(q_refs[h // per][rows, sub] * scale)
        k.append(k_refs[h // per][rows, sub] * valid)
        bh.append(b_all[:, h * GLA_DK:(h + 1) * GLA_DK])

    blocks = [[] for _ in heads]
    for i in range(C // GLA_SUB):
        lo_r = i * GLA_SUB
        sl = slice(lo_r, lo_r + GLA_SUB)
        w = [jnp.zeros((GLA_SUB, C), F32) for _ in heads]
        for s in range(GLA_SUB):
            r = lo_r + s
            at_r = lane == r
            for h in heads:
                zed = q[h][sl] * k[h][r:r + 1] * jnp.exp2(bh[h][sl] - bh[h][r:r + 1])
                w[h] = jnp.where(at_r, jnp.sum(zed, axis=-1, keepdims=True), w[h])
        causal = lane <= rsub + lo_r
        for h in heads:
            wh = jnp.where(causal, w[h], 0.0)
            if i > 0:
                ref_b = bh[h][lo_r - 1:lo_r]
                qi = (q[h][sl] * jnp.exp2(bh[h][sl] - ref_b)).astype(BF16)
                kj = k[h][:lo_r] * jnp.exp2(ref_b - bh[h][:lo_r])
                kj = jnp.concatenate([kj, jnp.zeros((C - lo_r, GLA_DK), F32)], axis=0).astype(BF16)
                wh = jnp.where(lane < lo_r, _dot_nt(qi, kj), wh)
            blocks[h].append(wh)

    for h in heads:
        v = v_refs[h][rows, :].astype(BF16)
        bl = bh[h][C - 1:C, :]
        st = state_ref[h]
        if maybe_first:
            st = jnp.where(c == 0, 0.0, st)
        o = _dot_nt((q[h] * jnp.exp2(bh[h])).astype(BF16), st.astype(BF16))
        a = jnp.concatenate(blocks[h], axis=0).astype(BF16)
        o_ref[rows, h * GLA_DV:(h + 1) * GLA_DV] = o + _dot(a, v)
        k_out = (k[h] * jnp.exp2(bl - bh[h])).astype(BF16)
        state_ref[h] = st * jnp.exp2(bl) + _dot_tn(v, k_out)


def _mixer_kernel(sink_ref, *refs, nb, n_blocks):
    nq = GLA_KWIDTH // COL_TILE
    attn_in, rest = refs[:5], refs[5:]
    q_refs, k_refs, v_refs = rest[:nq], rest[nq:2 * nq], rest[2 * nq:2 * nq + GLA_HEADS]
    la_ref, att_ref, go_ref, st_ref, state_ref = rest[2 * nq + GLA_HEADS:]
    step = pl.program_id(0)

    @pl.when(step < n_blocks)
    def _():
        i = lax.rem(step, nb)
        _attn_block(i, sink_ref, *attn_in, att_ref)
        per_block = BLOCK // GLA_CHUNK
        for half in range(per_block):
            _gla_chunk(i * per_block + half, pl.ds(half * GLA_CHUNK, GLA_CHUNK), q_refs, k_refs, v_refs, la_ref,
                       go_ref, state_ref, maybe_first=(half == 0))
        st_ref[...] = state_ref[...]

    @pl.when(step >= n_blocks)
    def _():
        att_ref[...] = jnp.zeros_like(att_ref)
        go_ref[...] = jnp.zeros_like(go_ref)


def _mixer_prompt(z, la, sinks, batch, nb, m_rows):
    n_blocks = batch * nb
    nq = GLA_KWIDTH // COL_TILE

    def cur(c):
        return lambda s, sink: (jnp.minimum(s, n_blocks - 1), c)

    def prev(c):
        return lambda s, sink: (jnp.maximum(jnp.minimum(s, n_blocks - 1) - 1, 0), c)

    def rows(c):
        return lambda s, sink: (s, c)

    return pl.pallas_call(
        functools.partial(_mixer_kernel, nb=nb, n_blocks=n_blocks),
        grid_spec=pltpu.PrefetchScalarGridSpec(
            num_scalar_prefetch=1,
            grid=(m_rows // BLOCK,),
            in_specs=(
                [pl.BlockSpec((BLOCK, ATT_WIDTH), cur(COL_QA // ATT_WIDTH)),
                 pl.BlockSpec((BLOCK, KV_WIDTH), cur(COL_KA // KV_WIDTH)),
                 pl.BlockSpec((BLOCK, KV_WIDTH), prev(COL_KA // KV_WIDTH)),
                 pl.BlockSpec((BLOCK, KV_WIDTH), cur(COL_VA // KV_WIDTH)),
                 pl.BlockSpec((BLOCK, KV_WIDTH), prev(COL_VA // KV_WIDTH))]
                + [pl.BlockSpec((BLOCK, COL_TILE), rows(COL_QG // COL_TILE + t)) for t in range(nq)]
                + [pl.BlockSpec((BLOCK, COL_TILE), rows(COL_KG // COL_TILE + t)) for t in range(nq)]
                + [pl.BlockSpec((BLOCK, GLA_DV), rows(COL_VG // GLA_DV + t)) for t in range(GLA_HEADS)]
                + [pl.BlockSpec((BLOCK, GLA_KWIDTH), rows(0))]
            ),
            out_specs=[
                pl.BlockSpec((BLOCK, ATT_WIDTH), rows(0)),
                pl.BlockSpec((BLOCK, GLA_WIDTH), rows(0)),
                pl.BlockSpec((None, GLA_HEADS, GLA_DV, GLA_DK),
                             lambda s, sink: (jnp.minimum(s // nb, batch - 1), 0, 0, 0)),
            ],
            scratch_shapes=[pltpu.VMEM((GLA_HEADS, GLA_DV, GLA_DK), F32)],
        ),
        out_shape=[
            jax.ShapeDtypeStruct((m_rows, ATT_WIDTH), BF16),
            jax.ShapeDtypeStruct((m_rows, GLA_WIDTH), F32),
            jax.ShapeDtypeStruct((batch, GLA_HEADS, GLA_DV, GLA_DK), F32),
        ],
        compiler_params=pltpu.CompilerParams(
            dimension_semantics=("arbitrary",), vmem_limit_bytes=VMEM_LIMIT),
        name="mixer_prompt",
    )(sinks, *([z] * (5 + 2 * nq + GLA_HEADS)), la)


def _sample_kernel(qa_ref, zs_ref, la_ref, kbuf_ref, vbuf_ref, sink_ref, s0_ref,
                   att_ref, go_ref, st_ref, *, dec_seq):
    for b in range(qa_ref.shape[0]):
        _sample_one(qa_ref.at[b], zs_ref.at[b], la_ref.at[b], kbuf_ref.at[b], vbuf_ref.at[b], sink_ref,
                    s0_ref.at[b], att_ref.at[b], go_ref.at[b], st_ref.at[b], dec_seq=dec_seq)


def _sample_one(qa_ref, zs_ref, la_ref, kbuf_ref, vbuf_ref, sink_ref, s0_ref,
                att_ref, go_ref, st_ref, *, dec_seq):
    S = dec_seq
    R = S * ATT_GROUP
    t_row = lax.broadcasted_iota(jnp.int32, (R, 1), 0) // ATT_GROUP
    c_idx = lax.broadcasted_iota(jnp.int32, (R, kbuf_ref.shape[0]), 1)
    mask_c = c_idx >= t_row
    for kv in range(ATT_KV_HEADS):
        ks = slice(kv * ATT_HEAD_DIM, (kv + 1) * ATT_HEAD_DIM)
        q = qa_ref[kv]
        sink = sink_ref[kv]
        sc = _dot_nt(q.astype(BF16), kbuf_ref[:, ks].astype(BF16))
        sc = jnp.where(mask_c, sc, -jnp.inf)
        mx = jnp.maximum(jnp.max(sc, axis=-1, keepdims=True), sink)
        kn = zs_ref[:, COL_KA + kv * ATT_HEAD_DIM:COL_KA + (kv + 1) * ATT_HEAD_DIM]
        vn = zs_ref[:, COL_VA + kv * ATT_HEAD_DIM:COL_VA + (kv + 1) * ATT_HEAD_DIM]
        sn = []
        for s in range(S):
            v = jnp.sum(q * kn[s:s + 1], axis=-1, keepdims=True)
            v = jnp.where(t_row >= s, v, -jnp.inf)
            sn.append(v)
            mx = jnp.maximum(mx, v)
        pc = jnp.exp(sc - mx)
        den = jnp.sum(pc, axis=-1, keepdims=True) + jnp.exp(sink - mx)
        o = _dot(pc.astype(BF16), vbuf_ref[:, ks].astype(BF16))
        for s in range(S):
            pn = jnp.exp(sn[s] - mx)
            den = den + pn
            o = o + pn * vn[s:s + 1]
        att_ref[kv] = o / den

    rows8 = zs_ref.shape[0]
    rid = lax.broadcasted_iota(jnp.int32, (rows8, 1), 0)
    la = la_ref[...]
    run = la[0:1]
    b_all = jnp.where(rid == 0, run, 0.0)
    for t in range(1, S):
        run = run + la[t:t + 1]
        b_all = jnp.where(rid == t, run, b_all)
    real = (rid < S).astype(F32)
    scale = GLA_DK ** -0.5
    for h in range(GLA_HEADS):
        ks = slice(h * GLA_DK, (h + 1) * GLA_DK)
        q = zs_ref[:, COL_QG + h * GLA_DK:COL_QG + (h + 1) * GLA_DK] * scale
        k = zs_ref[:, COL_KG + h * GLA_DK:COL_KG + (h + 1) * GLA_DK]
        v = zs_ref[:, COL_VG + h * GLA_DV:COL_VG + (h + 1) * GLA_DV]
        bh = b_all[:, ks] * real
        bl = bh[S - 1:S]
        s0 = s0_ref[h]
        k_out = k * jnp.exp(jnp.minimum(bl - bh, 0.0)) * real
        q_in = q * jnp.exp(bh) * real
        o = _dot(q_in.astype(BF16), s0.astype(BF16))
        for s in range(S):
            zed = q * k[s:s + 1] * jnp.exp(jnp.minimum(bh - bh[s:s + 1], 0.0))
            w = jnp.sum(zed, axis=-1, keepdims=True)
            w = jnp.where((rid >= s) & (rid < S), w, 0.0)
            o = o + w * v[s:s + 1]
        go_ref[:, h * GLA_DV:(h + 1) * GLA_DV] = o
        e_rows = jnp.broadcast_to(jnp.exp(bl), (GLA_DK, GLA_DK)).T
        decay = jnp.concatenate([e_rows] * (GLA_DV // GLA_DK), axis=1)
        st_ref[h] = decay * s0 + _dot_tn(k_out.astype(BF16), v.astype(BF16))


def _sample_mixer(qa_s, zs8, la8, kbuf, vbuf, sink_rows, s0, dec_seq, l):
    nb = zs8.shape[0]
    rows8 = zs8.shape[1]
    R = dec_seq * ATT_GROUP
    wb = kbuf.shape[2]
    return pl.pallas_call(
        functools.partial(_sample_kernel, dec_seq=dec_seq),
        grid=(nb // SAMPLE_TILE,),
        in_specs=[
            pl.BlockSpec((SAMPLE_TILE, ATT_KV_HEADS, R, ATT_HEAD_DIM), lambda b: (b, 0, 0, 0)),
            pl.BlockSpec((SAMPLE_TILE, rows8, Z_WIDTH), lambda b: (b, 0, 0)),
            pl.BlockSpec((SAMPLE_TILE, rows8, GLA_KWIDTH), lambda b: (b, 0, 0)),
            pl.BlockSpec((None, SAMPLE_TILE, wb, KV_WIDTH), lambda b: (l, b, 0, 0)),
            pl.BlockSpec((None, SAMPLE_TILE, wb, KV_WIDTH), lambda b: (l, b, 0, 0)),
            pl.BlockSpec((ATT_KV_HEADS, R, 1), lambda b: (0, 0, 0)),
            pl.BlockSpec((None, SAMPLE_TILE, GLA_HEADS, GLA_DK, GLA_DV), lambda b: (l, b, 0, 0, 0)),
        ],
        out_specs=[
            pl.BlockSpec((SAMPLE_TILE, ATT_KV_HEADS, R, ATT_HEAD_DIM), lambda b: (b, 0, 0, 0)),
            pl.BlockSpec((SAMPLE_TILE, rows8, GLA_WIDTH), lambda b: (b, 0, 0)),
            pl.BlockSpec((SAMPLE_TILE, GLA_HEADS, GLA_DK, GLA_DV), lambda b: (b, 0, 0, 0)),
        ],
        out_shape=[
            jax.ShapeDtypeStruct((nb, ATT_KV_HEADS, R, ATT_HEAD_DIM), F32),
            jax.ShapeDtypeStruct((nb, rows8, GLA_WIDTH), F32),
            jax.ShapeDtypeStruct((nb, GLA_HEADS, GLA_DK, GLA_DV), F32),
        ],
        compiler_params=pltpu.CompilerParams(
            dimension_semantics=("parallel",), vmem_limit_bytes=VMEM_LIMIT),
        name="sample_mixer",
    )(qa_s, zs8, la8, kbuf, vbuf, sink_rows, s0)


def _merge_kernel(*refs):
    x_ref, att_ref, go_ref = refs[:3]
    rg_refs = refs[3:3 + GLA_HEADS]
    ggla_ref, wo_ref, gpost_ref, o_ref = refs[3 + GLA_HEADS:]
    parts = []
    for h in range(GLA_HEADS):
        vs = slice(h * GLA_DV, (h + 1) * GLA_DV)
        rg = rg_refs[h][...]
        parts.append((_rms(go_ref[:, vs], ggla_ref[...]) * (rg * jax.nn.sigmoid(rg))).astype(BF16))
    go = jnp.concatenate(parts, axis=-1)
    y = _dot(att_ref[...], wo_ref[:ATT_WIDTH, :]) + _dot(go, wo_ref[ATT_WIDTH:, :])
    o_ref[...] = x_ref[...] + _rms(y, gpost_ref[...])


def _merge(x, att, go, z, g_gla, w_out, gains, l):
    m = x.shape[0]
    return pl.pallas_call(
        _merge_kernel,
        grid=(m // TM_MERGE,),
        in_specs=(
            [pl.BlockSpec((TM_MERGE, D_MODEL), lambda i: (i, 0)),
             pl.BlockSpec((TM_MERGE, ATT_WIDTH), lambda i: (i, 0)),
             pl.BlockSpec((TM_MERGE, GLA_WIDTH), lambda i: (i, 0))]
            + [pl.BlockSpec((TM_MERGE, GLA_DV), lambda i, h=h: (i, COL_RG // GLA_DV + h)) for h in range(GLA_HEADS)]
            + [pl.BlockSpec((None, 1, GLA_DV), lambda i: (l, 0, 0)),
               pl.BlockSpec((None, ATT_WIDTH + GLA_WIDTH, D_MODEL), lambda i: (l, 0, 0)),
               _gain_spec(l, 3)]
        ),
        out_specs=pl.BlockSpec((TM_MERGE, D_MODEL), lambda i: (i, 0)),
        out_shape=jax.ShapeDtypeStruct((m, D_MODEL), F32),
        compiler_params=pltpu.CompilerParams(
            dimension_semantics=("parallel",), vmem_limit_bytes=VMEM_LIMIT),
        name="merge",
    )(x, att, go, *([z] * GLA_HEADS), g_gla, w_out, gains)


def kernel(x_prompt, x_sample, cache_k_win, cache_v_win, state_gla, meta_tokens, norm_gains,
           w_ffn_gate, w_ffn_up, w_ffn_down, w_in, w_gate_up, b_gate, attn_sinks, gla_norm, w_out):
    batch, seq, _ = x_prompt.shape
    dec_batch, dec_seq, _ = x_sample.shape
    depth = norm_gains.shape[0]
    lp = seq + BLOCK
    nb = lp // BLOCK
    mp = batch * lp
    ms = dec_batch * dec_seq
    m_pad = -(-(mp + ms) // TM) * TM
    wb = cache_k_win.shape[2]
    rows8 = 8
    assert dec_seq <= rows8 and lp % BLOCK == 0 and dec_batch % SAMPLE_TILE == 0

    head = jnp.concatenate([jnp.zeros((PAD, D_MODEL), F32), meta_tokens.astype(F32)], axis=0)
    pieces = []
    for b in range(batch):
        pieces += [head, x_prompt[b]]
    pieces += [x_sample.reshape(ms, D_MODEL), jnp.zeros((m_pad - mp - ms, D_MODEL), F32)]
    x = jnp.concatenate(pieces, axis=0)

    wg, wu, wd = w_ffn_gate, w_ffn_up, w_ffn_down
    wo = w_out.astype(BF16)
    w_in_t = jnp.swapaxes(w_in, 1, 2)
    col = jnp.arange(Z_WIDTH)
    colscale = jnp.where((col >= COL_QA) & (col < COL_QA + ATT_WIDTH), ATT_HEAD_DIM ** -0.5, 1.0).astype(F32)[None]
    wa_p = jnp.pad(w_gate_up, ((0, 0), (0, LR_PAD - GLA_RANK), (0, 0))).astype(BF16)
    ba = b_gate.reshape(depth, 1, GLA_KWIDTH)
    g_gla = gla_norm.reshape(depth, 1, GLA_DV)
    gains = norm_gains.reshape(depth, 6, 1, D_MODEL)
    kbuf = cache_k_win.reshape(depth, dec_batch, wb, KV_WIDTH)
    vbuf = cache_v_win.reshape(depth, dec_batch, wb, KV_WIDTH)
    tail = jnp.zeros((m_pad - mp - ms, GLA_WIDTH), F32)

    pk, pv, ps, sk, sv, ss = [], [], [], [], [], []
    for l in range(depth):
        x = _ffn(x, gains, wg, wu, wd, l, 0)

        z, la = _proj(x, gains, w_in_t, colscale, wa_p, ba, l)
        att, go, st_p = _mixer_prompt(z, la, attn_sinks[l], batch, nb, m_pad)

        zs = z[mp:mp + ms].reshape(dec_batch, dec_seq, Z_WIDTH)
        zs8 = jnp.pad(zs, ((0, 0), (0, rows8 - dec_seq), (0, 0)))
        la8 = jnp.pad(la[mp:mp + ms].reshape(dec_batch, dec_seq, GLA_KWIDTH), ((0, 0), (0, rows8 - dec_seq), (0, 0)))
        qa_s = zs[:, :, COL_QA:COL_QA + ATT_WIDTH].reshape(dec_batch, dec_seq, ATT_KV_HEADS, ATT_GROUP, ATT_HEAD_DIM)
        qa_s = qa_s.transpose(0, 2, 1, 3, 4).reshape(dec_batch, ATT_KV_HEADS, dec_seq * ATT_GROUP, ATT_HEAD_DIM)
        sink_rows = jnp.tile(attn_sinks[l].reshape(ATT_KV_HEADS, 1, ATT_GROUP), (1, dec_seq, 1))
        sink_rows = sink_rows.reshape(ATT_KV_HEADS, dec_seq * ATT_GROUP, 1)
        att_s, go_s, st_s = _sample_mixer(qa_s, zs8, la8, kbuf, vbuf, sink_rows, state_gla, dec_seq, l)
        att_s = att_s.reshape(dec_batch, ATT_KV_HEADS, dec_seq, ATT_GROUP, ATT_HEAD_DIM)
        att_s = att_s.transpose(0, 2, 1, 3, 4).reshape(ms, ATT_WIDTH)

        att = lax.dynamic_update_slice(att, jnp.concatenate([att_s, tail], axis=0).astype(BF16), (mp, 0))
        go = lax.dynamic_update_slice(go, jnp.concatenate([go_s[:, :dec_seq].reshape(ms, GLA_WIDTH), tail], axis=0), (mp, 0))
        x = _merge(x, att, go, z, g_gla, wo, gains, l)

        win = [z[b * lp + lp - WINDOW:(b + 1) * lp, COL_KA:COL_VA + KV_WIDTH] for b in range(batch)]
        win = jnp.stack(win).reshape(batch, WINDOW, 2, ATT_KV_HEADS, ATT_HEAD_DIM)
        pk.append(win[:, :, 0])
        pv.append(win[:, :, 1])
        ps.append(st_p.transpose(0, 1, 3, 2))
        kn = zs[:, :, COL_KA:COL_KA + KV_WIDTH].reshape(dec_batch, dec_seq, ATT_KV_HEADS, ATT_HEAD_DIM)
        vn = zs[:, :, COL_VA:COL_VA + KV_WIDTH].reshape(dec_batch, dec_seq, ATT_KV_HEADS, ATT_HEAD_DIM)
        sk.append(jnp.concatenate([cache_k_win[l], kn], axis=1)[:, -wb:])
        sv.append(jnp.concatenate([cache_v_win[l], vn], axis=1)[:, -wb:])
        ss.append(st_s)

        x = _ffn(x, gains, wg, wu, wd, l, 1)

    y_prompt = jnp.stack([x[b * lp + BLOCK:(b + 1) * lp] for b in range(batch)])
    y_sample = x[mp:mp + ms].reshape(dec_batch, dec_seq, D_MODEL)
    return (y_prompt, y_sample, jnp.stack(pk), jnp.stack(pv), jnp.stack(ps),
            jnp.stack(sk), jnp.stack(sv), jnp.stack(ss))
```

```python
import functools

import jax
import jax.numpy as jnp
from jax import lax
from jax.experimental import pallas as pl
from jax.experimental.pallas import tpu as pltpu

F32 = jnp.float32
BF16 = jnp.bfloat16

D_MODEL = 2048
D_FF = 5632
N_META = 16
BLOCK = 128
WINDOW = 128
ATT_HEADS = 16
ATT_KV_HEADS = 2
ATT_GROUP = 8
ATT_HEAD_DIM = 64
ATT_WIDTH = 1024
KV_WIDTH = 128
GLA_HEADS = 4
GLA_DK = 128
GLA_DV = 256
GLA_KWIDTH = 512
GLA_WIDTH = 1024
GLA_RANK = 16
GLA_GATE_NORM = 16.0
GLA_CHUNK = 64
GLA_SUB = 8
EPS = 1e-6
LOG2_E = 1.4426950408889634
PAD = BLOCK - N_META

COL_QA, COL_KA, COL_VA, COL_QG, COL_KG, COL_VG, COL_RG, COL_LR = 0, 1024, 1152, 1280, 1792, 2304, 3328, 4352
COL_TILE = 256
Z_WIDTH = 4608
LR_PAD = 128

VMEM_LIMIT = 56 * 1024 * 1024
VMEM_LIMIT_FFN = 60 * 1024 * 1024

TM = 1072
TM_MERGE = 536
TF = 256
TN = 768
X_PARTS = 4
SAMPLE_TILE = 4


def _rms(x, gain):
    ms = jnp.mean(x * x, axis=-1, keepdims=True)
    return x * lax.rsqrt(ms + EPS) * gain


def _dot(a, b):
    return jnp.dot(a, b, preferred_element_type=F32)


def _dot_nt(a, b):
    return lax.dot_general(a, b, (((1,), (1,)), ((), ())), preferred_element_type=F32)


def _dot_tn(a, b):
    return lax.dot_general(a, b, (((0,), (0,)), ((), ())), preferred_element_type=F32)


def _row_parts(n_rows, parts):
    units = n_rows // 8
    out, start = [], 0
    for r in range(parts):
        size = (units // parts + (1 if r < units % parts else 0)) * 8
        out.append((start, size))
        start += size
    assert start == n_rows
    return out


def _ffn_kernel(x_hbm, gpre_ref, gpost_ref, wg_hbm, wu_hbm, wd_hbm, o_ref, xn_ref, x_buf, wg_buf, wu_buf, wd_buf,
                sem, xsem, *, l, f):
    i = pl.program_id(0)
    n_tiles = pl.num_programs(0)
    n_chunks = D_FF // TF
    parts = _row_parts(TM, X_PARTS)

    def w_copies(j, slot):
        cols = pl.ds(j * TF, TF)
        return (pltpu.make_async_copy(wg_hbm.at[l, f, :, cols], wg_buf.at[slot], sem.at[0, slot]),
                pltpu.make_async_copy(wu_hbm.at[l, f, :, cols], wu_buf.at[slot], sem.at[1, slot]),
                pltpu.make_async_copy(wd_hbm.at[l, f, cols, :], wd_buf.at[slot], sem.at[2, slot]))

    def x_copy(tile, r):
        start, size = parts[r]
        return pltpu.make_async_copy(x_hbm.at[pl.ds(tile * TM + start, size), :],
                                     x_buf.at[pl.ds(start, size), :], xsem.at[r])

    @pl.when(i == 0)
    def _():
        for r in range(X_PARTS):
            x_copy(0, r).start()
        for c in w_copies(0, 0):
            c.start()

    for r, (start, size) in enumerate(parts):
        x_copy(i, r).wait()
        rows = pl.ds(start, size)
        xn_ref[rows, :] = _rms(x_buf[rows, :], gpre_ref[...]).astype(BF16)
    o_ref[...] = jnp.zeros_like(o_ref)

    def pair(p, carry):
        for slot in range(2):
            j = 2 * p + slot
            for c in w_copies(j, slot):
                c.wait()
            for c in w_copies(lax.rem(j + 1, n_chunks), 1 - slot):
                c.start()
            xn = xn_ref[...]
            g = _dot(xn, wg_buf[slot].astype(BF16))
            u = _dot(xn, wu_buf[slot].astype(BF16))
            h = (g * jax.nn.sigmoid(g) * u).astype(BF16)
            o_ref[...] += _dot(h, wd_buf[slot].astype(BF16))
        return carry

    lax.fori_loop(0, n_chunks // 2, pair, 0)

    for r, (start, size) in enumerate(parts):
        rows = pl.ds(start, size)
        o_ref[rows, :] = x_buf[rows, :] + 0.5 * _rms(o_ref[rows, :], gpost_ref[...])

        @pl.when(i + 1 < n_tiles)
        def _():
            x_copy(i + 1, r).start()

    @pl.when(i == n_tiles - 1)
    def _():
        for c in w_copies(0, 0):
            c.wait()


def _gain_spec(l, idx):
    return pl.BlockSpec((None, None, 1, D_MODEL), lambda *_: (l, idx, 0, 0))


def _ffn(x, gains, wg, wu, wd, l, f):
    m = x.shape[0]
    return pl.pallas_call(
        functools.partial(_ffn_kernel, l=l, f=f),
        grid=(m // TM,),
        in_specs=[
            pl.BlockSpec(memory_space=pl.ANY),
            _gain_spec(l, 4 * f),
            _gain_spec(l, 4 * f + 1),
            pl.BlockSpec(memory_space=pl.ANY),
            pl.BlockSpec(memory_space=pl.ANY),
            pl.BlockSpec(memory_space=pl.ANY),
        ],
        out_specs=pl.BlockSpec((TM, D_MODEL), lambda i: (i, 0)),
        out_shape=jax.ShapeDtypeStruct((m, D_MODEL), F32),
        scratch_shapes=[
            pltpu.VMEM((TM, D_MODEL), BF16),
            pltpu.VMEM((TM, D_MODEL), F32),
            pltpu.VMEM((2, D_MODEL, TF), F32),
            pltpu.VMEM((2, D_MODEL, TF), F32),
            pltpu.VMEM((2, TF, D_MODEL), F32),
            pltpu.SemaphoreType.DMA((3, 2)),
            pltpu.SemaphoreType.DMA((X_PARTS,)),
        ],
        compiler_params=pltpu.CompilerParams(
            dimension_semantics=("arbitrary",), vmem_limit_bytes=VMEM_LIMIT_FFN),
        name="ffn",
    )(x, gains, gains, wg, wu, wd)


def _proj_kernel(x_ref, g_ref, wt_ref, cs_ref, wa_ref, ba_ref, z_ref, la_ref, xn_ref, *, n_cols):
    j = pl.program_id(1)
    last = pl.num_programs(1) - 1
    n_valid = n_cols - (Z_WIDTH - TN)

    @pl.when(j == 0)
    def _():
        xn_ref[...] = _rms(x_ref[...], g_ref[...]).astype(BF16)

    @pl.when(j < last)
    def _():
        z_ref[...] = _dot_nt(xn_ref[...], wt_ref[...].astype(BF16)) * cs_ref[...]

    @pl.when(j == last)
    def _():
        z_ref[:, :n_valid] = _dot_nt(xn_ref[...], wt_ref[:n_valid, :].astype(BF16)) * cs_ref[:, :n_valid]
        z_ref[:, n_valid:] = jnp.zeros((z_ref.shape[0], TN - n_valid), F32)
        lr_off = COL_LR - (Z_WIDTH - TN)
        lr = z_ref[:, lr_off:lr_off + LR_PAD].astype(BF16)
        logit = _dot(lr, wa_ref[...]) + ba_ref[...]
        log_sig = jnp.minimum(logit, 0.0) - jnp.log1p(jnp.exp(-jnp.abs(logit)))
        la_ref[...] = log_sig * (1.0 / GLA_GATE_NORM)


def _proj(x, gains, w_in_t, colscale, wa_p, ba, l):
    m = x.shape[0]
    n_cols = w_in_t.shape[1]
    assert Z_WIDTH - TN < COL_LR and COL_LR + GLA_RANK <= n_cols <= Z_WIDTH
    return pl.pallas_call(
        functools.partial(_proj_kernel, n_cols=n_cols),
        grid=(m // TM, Z_WIDTH // TN),
        in_specs=[
            pl.BlockSpec((TM, D_MODEL), lambda i, j: (i, 0)),
            _gain_spec(l, 2),
            pl.BlockSpec((None, TN, D_MODEL), lambda i, j: (l, j, 0)),
            pl.BlockSpec((1, TN), lambda i, j: (0, j)),
            pl.BlockSpec((None, LR_PAD, GLA_KWIDTH), lambda i, j: (l, 0, 0)),
            pl.BlockSpec((None, 1, GLA_KWIDTH), lambda i, j: (l, 0, 0)),
        ],
        out_specs=[
            pl.BlockSpec((TM, TN), lambda i, j: (i, j)),
            pl.BlockSpec((TM, GLA_KWIDTH), lambda i, j: (i, 0)),
        ],
        out_shape=[
            jax.ShapeDtypeStruct((m, Z_WIDTH), F32),
            jax.ShapeDtypeStruct((m, GLA_KWIDTH), F32),
        ],
        scratch_shapes=[pltpu.VMEM((TM, D_MODEL), BF16)],
        compiler_params=pltpu.CompilerParams(
            dimension_semantics=("parallel", "arbitrary"), vmem_limit_bytes=VMEM_LIMIT),
        name="proj",
    )(x, gains, w_in_t, colscale, wa_p, ba)


def _pair_blockdiag(x128, kv):
    lane = lax.broadcasted_iota(jnp.int32, x128.shape, 1)
    own = jnp.where((lane >= kv * ATT_HEAD_DIM) & (lane < (kv + 1) * ATT_HEAD_DIM), x128, 0.0)
    other = pltpu.roll(own, ATT_HEAD_DIM, axis=1)
    lo, hi = (own, other) if kv == 0 else (other, own)
    return jnp.concatenate([lo, hi], axis=0).astype(BF16)


def _attn_block(i, sink_ref, q_ref, kc_ref, kp_ref, vc_ref, vp_ref, o_ref):
    kb = 2 * BLOCK
    row = lax.broadcasted_iota(jnp.int32, (BLOCK, kb), 0)
    col = lax.broadcasted_iota(jnp.int32, (BLOCK, kb), 1)
    diff = row + BLOCK - col
    key_pos = col + (i - 1) * BLOCK
    mask = (diff >= 0) & (diff <= WINDOW) & (key_pos >= PAD)
    lane = lax.broadcasted_iota(jnp.int32, (BLOCK, 2 * ATT_HEAD_DIM), 1)
    kk = jnp.concatenate([kp_ref[...], kc_ref[...]], axis=0)
    vv = jnp.concatenate([vp_ref[...], vc_ref[...]], axis=0)
    slabs = ATT_GROUP // 2
    for kv in range(ATT_KV_HEADS):
        k2 = _pair_blockdiag(kk, kv)
        v2 = _pair_blockdiag(vv, kv)
        c0 = kv * slabs * 2 * ATT_HEAD_DIM
        q4 = q_ref[:, c0:c0 + slabs * 2 * ATT_HEAD_DIM]
        q4 = jnp.concatenate([q4[:, p * 128:(p + 1) * 128] for p in range(slabs)], axis=0).astype(BF16)
        s = _dot_nt(q4, k2)
        probs, inv = [], []
        for p in range(slabs):
            halves, rden = [], []
            for e in range(2):
                sink = sink_ref[kv * ATT_GROUP + 2 * p + e]
                sp = jnp.where(mask, s[p * BLOCK:(p + 1) * BLOCK, e * kb:(e + 1) * kb], -jnp.inf)
                mx = jnp.maximum(jnp.max(sp, axis=-1, keepdims=True), sink)
                pe = jnp.exp(sp - mx)
                rden.append(1.0 / (jnp.sum(pe, axis=-1, keepdims=True) + jnp.exp(sink - mx)))
                halves.append(pe.astype(BF16))
            probs.append(jnp.concatenate(halves, axis=1))
            inv.append(jnp.where(lane < ATT_HEAD_DIM, rden[0], rden[1]))
        o = _dot(jnp.concatenate(probs, axis=0), v2)
        for p in range(slabs):
            o_ref[:, c0 + p * 128:c0 + (p + 1) * 128] = (o[p * BLOCK:(p + 1) * BLOCK] * inv[p]).astype(o_ref.dtype)


def _cumsum_rows(la, tri):
    hi = la.astype(BF16)
    r1 = la - hi.astype(F32)
    mid = r1.astype(BF16)
    lo = (r1 - mid.astype(F32)).astype(BF16)
    return _dot(tri, hi) + _dot(tri, mid) + _dot(tri, lo)


def _gla_chunk(c, rows, q_refs, k_refs, v_refs, la_ref, o_ref, state_ref, maybe_first):
    per = COL_TILE // GLA_DK
    C = GLA_CHUNK

    rowc = lax.broadcasted_iota(jnp.int32, (C, C), 0)
    colc = lax.broadcasted_iota(jnp.int32, (C, C), 1)
    tri = (rowc >= colc).astype(BF16)
    b_all = _cumsum_rows(la_ref[rows, :], tri) * LOG2_E

    pos = c * C + lax.broadcasted_iota(jnp.int32, (C, 1), 0)
    valid = (pos >= PAD).astype(F32)

    lane = lax.broadcasted_iota(jnp.int32, (GLA_SUB, C), 1)
    rsub = lax.broadcasted_iota(jnp.int32, (GLA_SUB, C), 0)
    scale = GLA_DK ** -0.5

    heads = range(GLA_HEADS)
    q, k, bh = [], [], []
    for h in heads:
        sub = slice((h % per) * GLA_DK, (h % per + 1) * GLA_DK)
        q.append(q_refs[h // per][rows, sub] * scale)
        k.append(k_refs[h // per][rows, sub] * valid)
        bh.append(b_all[:, h * GLA_DK:(h + 1) * GLA_DK])

    blocks = [[] for _ in heads]
    for i in range(C // GLA_SUB):
        lo_r = i * GLA_SUB
        sl = slice(lo_r, lo_r + GLA_SUB)
        w = [jnp.zeros((GLA_SUB, C), F32) for _ in heads]
        for s in range(GLA_SUB):
            r = lo_r + s
            at_r = lane == r
            for h in heads:
                zed = q[h][sl] * k[h][r:r + 1] * jnp.exp2(bh[h][sl] - bh[h][r:r + 1])
                w[h] = jnp.where(at_r, jnp.sum(zed, axis=-1, keepdims=True), w[h])
        causal = lane <= rsub + lo_r
        for h in heads:
            wh = jnp.where(causal, w[h], 0.0)
            if i > 0:
                ref_b = bh[h][lo_r - 1:lo_r]
                qi = (q[h][sl] * jnp.exp2(bh[h][sl] - ref_b)).astype(BF16)
                kj = k[h][:lo_r] * jnp.exp2(ref_b - bh[h][:lo_r])
                kj = jnp.concatenate([kj, jnp.zeros((C - lo_r, GLA_DK), F32)], axis=0).astype(BF16)
                wh = jnp.where(lane < lo_r, _dot_nt(qi, kj), wh)
            blocks[h].append(wh)

    for h in heads:
        v = v_refs[h][rows, :].astype(BF16)
        bl = bh[h][C - 1:C, :]
        st = state_ref[h]
        if maybe_first:
            st = jnp.where(c == 0, 0.0, st)
        o = _dot_nt((q[h] * jnp.exp2(bh[h])).astype(BF16), st.astype(BF16))
        a = jnp.concatenate(blocks[h], axis=0).astype(BF16)
        o_ref[rows, h * GLA_DV:(h + 1) * GLA_DV] = o + _dot(a, v)
        k_out = (k[h] * jnp.exp2(bl - bh[h])).astype(BF16)
        state_ref[h] = st * jnp.exp2(bl) + _dot_tn(v, k_out)


def _mixer_kernel(sink_ref, *refs, nb, n_blocks):
    nq = GLA_KWIDTH // COL_TILE
    attn_in, rest = refs[:5], refs[5:]
    q_refs, k_refs, v_refs = rest[:nq], rest[nq:2 * nq], rest[2 * nq:2 * nq + GLA_HEADS]
    la_ref, att_ref, go_ref, st_ref, state_ref = rest[2 * nq + GLA_HEADS:]
    step = pl.program_id(0)

    @pl.when(step < n_blocks)
    def _():
        i = lax.rem(step, nb)
        _attn_block(i, sink_ref, *attn_in, att_ref)
        per_block = BLOCK // GLA_CHUNK
        for half in range(per_block):
            _gla_chunk(i * per_block + half, pl.ds(half * GLA_CHUNK, GLA_CHUNK), q_refs, k_refs, v_refs, la_ref,
                       go_ref, state_ref, maybe_first=(half == 0))
        st_ref[...] = state_ref[...]

    @pl.when(step >= n_blocks)
    def _():
        att_ref[...] = jnp.zeros_like(att_ref)
        go_ref[...] = jnp.zeros_like(go_ref)


def _mixer_prompt(z, la, sinks, batch, nb, m_rows):
    n_blocks = batch * nb
    nq = GLA_KWIDTH // COL_TILE

    def cur(c):
        return lambda s, sink: (jnp.minimum(s, n_blocks - 1), c)

    def prev(c):
        return lambda s, sink: (jnp.maximum(jnp.minimum(s, n_blocks - 1) - 1, 0), c)

    def rows(c):
        return lambda s, sink: (s, c)

    return pl.pallas_call(
        functools.partial(_mixer_kernel, nb=nb, n_blocks=n_blocks),
        grid_spec=pltpu.PrefetchScalarGridSpec(
            num_scalar_prefetch=1,
            grid=(m_rows // BLOCK,),
            in_specs=(
                [pl.BlockSpec((BLOCK, ATT_WIDTH), cur(COL_QA // ATT_WIDTH)),
                 pl.BlockSpec((BLOCK, KV_WIDTH), cur(COL_KA // KV_WIDTH)),
                 pl.BlockSpec((BLOCK, KV_WIDTH), prev(COL_KA // KV_WIDTH)),
                 pl.BlockSpec((BLOCK, KV_WIDTH), cur(COL_VA // KV_WIDTH)),
                 pl.BlockSpec((BLOCK, KV_WIDTH), prev(COL_VA // KV_WIDTH))]
                + [pl.BlockSpec((BLOCK, COL_TILE), rows(COL_QG // COL_TILE + t)) for t in range(nq)]
                + [pl.BlockSpec((BLOCK, COL_TILE), rows(COL_KG // COL_TILE + t)) for t in range(nq)]
                + [pl.BlockSpec((BLOCK, GLA_DV), rows(COL_VG // GLA_DV + t)) for t in range(GLA_HEADS)]
                + [pl.BlockSpec((BLOCK, GLA_KWIDTH), rows(0))]
            ),
            out_specs=[
                pl.BlockSpec((BLOCK, ATT_WIDTH), rows(0)),
                pl.BlockSpec((BLOCK, GLA_WIDTH), rows(0)),
                pl.BlockSpec((None, GLA_HEADS, GLA_DV, GLA_DK),
                             lambda s, sink: (jnp.minimum(s // nb, batch - 1), 0, 0, 0)),
            ],
            scratch_shapes=[pltpu.VMEM((GLA_HEADS, GLA_DV, GLA_DK), F32)],
        ),
        out_shape=[
            jax.ShapeDtypeStruct((m_rows, ATT_WIDTH), BF16),
            jax.ShapeDtypeStruct((m_rows, GLA_WIDTH), F32),
            jax.ShapeDtypeStruct((batch, GLA_HEADS, GLA_DV, GLA_DK), F32),
        ],
        compiler_params=pltpu.CompilerParams(
            dimension_semantics=("arbitrary",), vmem_limit_bytes=VMEM_LIMIT),
        name="mixer_prompt",
    )(sinks, *([z] * (5 + 2 * nq + GLA_HEADS)), la)


def _sample_kernel(qa_ref, zs_ref, la_ref, kbuf_ref, vbuf_ref, sink_ref, s0_ref,
                   att_ref, go_ref, st_ref, *, dec_seq):
    for b in range(qa_ref.shape[0]):
        _sample_one(qa_ref.at[b], zs_ref.at[b], la_ref.at[b], kbuf_ref.at[b], vbuf_ref.at[b], sink_ref,
                    s0_ref.at[b], att_ref.at[b], go_ref.at[b], st_ref.at[b], dec_seq=dec_seq)


def _sample_one(qa_ref, zs_ref, la_ref, kbuf_ref, vbuf_ref, sink_ref, s0_ref,
                att_ref, go_ref, st_ref, *, dec_seq):
    S = dec_seq
    R = S * ATT_GROUP
    t_row = lax.broadcasted_iota(jnp.int32, (R, 1), 0) // ATT_GROUP
    c_idx = lax.broadcasted_iota(jnp.int32, (R, kbuf_ref.shape[0]), 1)
    mask_c = c_idx >= t_row
    for kv in range(ATT_KV_HEADS):
        ks = slice(kv * ATT_HEAD_DIM, (kv + 1) * ATT_HEAD_DIM)
        q = qa_ref[kv]
        sink = sink_ref[kv]
        sc = _dot_nt(q.astype(BF16), kbuf_ref[:, ks].astype(BF16))
        sc = jnp.where(mask_c, sc, -jnp.inf)
        mx = jnp.maximum(jnp.max(sc, axis=-1, keepdims=True), sink)
        kn = zs_ref[:, COL_KA + kv * ATT_HEAD_DIM:COL_KA + (kv + 1) * ATT_HEAD_DIM]
        vn = zs_ref[:, COL_VA + kv * ATT_HEAD_DIM:COL_VA + (kv + 1) * ATT_HEAD_DIM]
        sn = []
        for s in range(S):
            v = jnp.sum(q * kn[s:s + 1], axis=-1, keepdims=True)
            v = jnp.where(t_row >= s, v, -jnp.inf)
            sn.append(v)
            mx = jnp.maximum(mx, v)
        pc = jnp.exp(sc - mx)
        den = jnp.sum(pc, axis=-1, keepdims=True) + jnp.exp(sink - mx)
        o = _dot(pc.astype(BF16), vbuf_ref[:, ks].astype(BF16))
        for s in range(S):
            pn = jnp.exp(sn[s] - mx)
            den = den + pn
            o = o + pn * vn[s:s + 1]
        att_ref[kv] = o / den

    rows8 = zs_ref.shape[0]
    rid = lax.broadcasted_iota(jnp.int32, (rows8, 1), 0)
    la = la_ref[...]
    run = la[0:1]
    b_all = jnp.where(rid == 0, run, 0.0)
    for t in range(1, S):
        run = run + la[t:t + 1]
        b_all = jnp.where(rid == t, run, b_all)
    real = (rid < S).astype(F32)
    scale = GLA_DK ** -0.5
    for h in range(GLA_HEADS):
        ks = slice(h * GLA_DK, (h + 1) * GLA_DK)
        q = zs_ref[:, COL_QG + h * GLA_DK:COL_QG + (h + 1) * GLA_DK] * scale
        k = zs_ref[:, COL_KG + h * GLA_DK:COL_KG + (h + 1) * GLA_DK]
        v = zs_ref[:, COL_VG + h * GLA_DV:COL_VG + (h + 1) * GLA_DV]
        bh = b_all[:, ks] * real
        bl = bh[S - 1:S]
        s0 = s0_ref[h]
        k_out = k * jnp.exp(jnp.minimum(bl - bh, 0.0)) * real
        q_in = q * jnp.exp(bh) * real
        o = _dot(q_in.astype(BF16), s0.astype(BF16))
        for s in range(S):
            zed = q * k[s:s + 1] * jnp.exp(jnp.minimum(bh - bh[s:s + 1], 0.0))
            w = jnp.sum(zed, axis=-1, keepdims=True)
            w = jnp.where((rid >= s) & (rid < S), w, 0.0)
            o = o + w * v[s:s + 1]
        go_ref[:, h * GLA_DV:(h + 1) * GLA_DV] = o
        e_rows = jnp.broadcast_to(jnp.exp(bl), (GLA_DK, GLA_DK)).T
        decay = jnp.concatenate([e_rows] * (GLA_DV // GLA_DK), axis=1)
        st_ref[h] = decay * s0 + _dot_tn(k_out.astype(BF16), v.astype(BF16))


def _sample_mixer(qa_s, zs8, la8, kbuf, vbuf, sink_rows, s0, dec_seq, l):
    nb = zs8.shape[0]
    rows8 = zs8.shape[1]
    R = dec_seq * ATT_GROUP
    wb = kbuf.shape[2]
    return pl.pallas_call(
        functools.partial(_sample_kernel, dec_seq=dec_seq),
        grid=(nb // SAMPLE_TILE,),
        in_specs=[
            pl.BlockSpec((SAMPLE_TILE, ATT_KV_HEADS, R, ATT_HEAD_DIM), lambda b: (b, 0, 0, 0)),
            pl.BlockSpec((SAMPLE_TILE, rows8, Z_WIDTH), lambda b: (b, 0, 0)),
            pl.BlockSpec((SAMPLE_TILE, rows8, GLA_KWIDTH), lambda b: (b, 0, 0)),
            pl.BlockSpec((None, SAMPLE_TILE, wb, KV_WIDTH), lambda b: (l, b, 0, 0)),
            pl.BlockSpec((None, SAMPLE_TILE, wb, KV_WIDTH), lambda b: (l, b, 0, 0)),
            pl.BlockSpec((ATT_KV_HEADS, R, 1), lambda b: (0, 0, 0)),
            pl.BlockSpec((None, SAMPLE_TILE, GLA_HEADS, GLA_DK, GLA_DV), lambda b: (l, b, 0, 0, 0)),
        ],
        out_specs=[
            pl.BlockSpec((SAMPLE_TILE, ATT_KV_HEADS, R, ATT_HEAD_DIM), lambda b: (b, 0, 0, 0)),
            pl.BlockSpec((SAMPLE_TILE, rows8, GLA_WIDTH), lambda b: (b, 0, 0)),
            pl.BlockSpec((SAMPLE_TILE, GLA_HEADS, GLA_DK, GLA_DV), lambda b: (b, 0, 0, 0)),
        ],
        out_shape=[
            jax.ShapeDtypeStruct((nb, ATT_KV_HEADS, R, ATT_HEAD_DIM), F32),
            jax.ShapeDtypeStruct((nb, rows8, GLA_WIDTH), F32),
            jax.ShapeDtypeStruct((nb, GLA_HEADS, GLA_DK, GLA_DV), F32),
        ],
        compiler_params=pltpu.CompilerParams(
            dimension_semantics=("parallel",), vmem_limit_bytes=VMEM_LIMIT),
        name="sample_mixer",
    )(qa_s, zs8, la8, kbuf, vbuf, sink_rows, s0)


def _merge_kernel(*refs):
    x_ref, att_ref, go_ref = refs[:3]
    rg_refs = refs[3:3 + GLA_HEADS]
    ggla_ref, wo_ref, gpost_ref, o_ref = refs[3 + GLA_HEADS:]
    parts = []
    for h in range(GLA_HEADS):
        vs = slice(h * GLA_DV, (h + 1) * GLA_DV)
        rg = rg_refs[h][...]
        parts.append((_rms(go_ref[:, vs], ggla_ref[...]) * (rg * jax.nn.sigmoid(rg))).astype(BF16))
    go = jnp.concatenate(parts, axis=-1)
    y = _dot(att_ref[...], wo_ref[:ATT_WIDTH, :]) + _dot(go, wo_ref[ATT_WIDTH:, :])
    o_ref[...] = x_ref[...] + _rms(y, gpost_ref[...])


def _merge(x, att, go, z, g_gla, w_out, gains, l):
    m = x.shape[0]
    return pl.pallas_call(
        _merge_kernel,
        grid=(m // TM_MERGE,),
        in_specs=(
            [pl.BlockSpec((TM_MERGE, D_MODEL), lambda i: (i, 0)),
             pl.BlockSpec((TM_MERGE, ATT_WIDTH), lambda i: (i, 0)),
             pl.BlockSpec((TM_MERGE, GLA_WIDTH), lambda i: (i, 0))]
            + [pl.BlockSpec((TM_MERGE, GLA_DV), lambda i, h=h: (i, COL_RG // GLA_DV + h)) for h in range(GLA_HEADS)]
            + [pl.BlockSpec((None, 1, GLA_DV), lambda i: (l, 0, 0)),
               pl.BlockSpec((None, ATT_WIDTH + GLA_WIDTH, D_MODEL), lambda i: (l, 0, 0)),
               _gain_spec(l, 3)]
        ),
        out_specs=pl.BlockSpec((TM_MERGE, D_MODEL), lambda i: (i, 0)),
        out_shape=jax.ShapeDtypeStruct((m, D_MODEL), F32),
        compiler_params=pltpu.CompilerParams(
            dimension_semantics=("parallel",), vmem_limit_bytes=VMEM_LIMIT),
        name="merge",
    )(x, att, go, *([z] * GLA_HEADS), g_gla, w_out, gains)


def _extract_kernel(x_hbm, y_hbm, sem, *, lp, seq):
    copies = [pltpu.make_async_copy(x_hbm.at[pl.ds(b * lp + (lp - seq), seq), :], y_hbm.at[b], sem.at[b])
              for b in range(y_hbm.shape[0])]
    for c in copies:
        c.start()
    for c in copies:
        c.wait()


def _extract_prompt(x, batch, seq, lp):
    return pl.pallas_call(
        functools.partial(_extract_kernel, lp=lp, seq=seq),
        in_specs=[pl.BlockSpec(memory_space=pl.ANY)],
        out_specs=pl.BlockSpec(memory_space=pl.ANY),
        out_shape=jax.ShapeDtypeStruct((batch, seq, x.shape[1]), x.dtype),
        scratch_shapes=[pltpu.SemaphoreType.DMA((batch,))],
        name="extract_prompt",
    )(x)


def kernel(x_prompt, x_sample, cache_k_win, cache_v_win, state_gla, meta_tokens, norm_gains,
           w_ffn_gate, w_ffn_up, w_ffn_down, w_in, w_gate_up, b_gate, attn_sinks, gla_norm, w_out):
    batch, seq, _ = x_prompt.shape
    dec_batch, dec_seq, _ = x_sample.shape
    depth = norm_gains.shape[0]
    lp = seq + BLOCK
    nb = lp // BLOCK
    mp = batch * lp
    ms = dec_batch * dec_seq
    m_pad = -(-(mp + ms) // TM) * TM
    wb = cache_k_win.shape[2]
    rows8 = 8
    assert dec_seq <= rows8 and lp % BLOCK == 0 and dec_batch % SAMPLE_TILE == 0

    head = jnp.concatenate([jnp.zeros((PAD, D_MODEL), F32), meta_tokens.astype(F32)], axis=0)
    pieces = []
    for b in range(batch):
        pieces += [head, x_prompt[b]]
    pieces += [x_sample.reshape(ms, D_MODEL), jnp.zeros((m_pad - mp - ms, D_MODEL), F32)]
    x = jnp.concatenate(pieces, axis=0)

    wg, wu, wd = w_ffn_gate, w_ffn_up, w_ffn_down
    wo = w_out.astype(BF16)
    w_in_t = jnp.swapaxes(w_in, 1, 2)
    col = jnp.arange(Z_WIDTH)
    colscale = jnp.where((col >= COL_QA) & (col < COL_QA + ATT_WIDTH), ATT_HEAD_DIM ** -0.5, 1.0).astype(F32)[None]
    wa_p = jnp.pad(w_gate_up, ((0, 0), (0, LR_PAD - GLA_RANK), (0, 0))).astype(BF16)
    ba = b_gate.reshape(depth, 1, GLA_KWIDTH)
    g_gla = gla_norm.reshape(depth, 1, GLA_DV)
    gains = norm_gains.reshape(depth, 6, 1, D_MODEL)
    kbuf = cache_k_win.reshape(depth, dec_batch, wb, KV_WIDTH)
    vbuf = cache_v_win.reshape(depth, dec_batch, wb, KV_WIDTH)
    tail = jnp.zeros((m_pad - mp - ms, GLA_WIDTH), F32)

    pk, pv, ps, sk, sv, ss = [], [], [], [], [], []
    for l in range(depth):
        x = _ffn(x, gains, wg, wu, wd, l, 0)

        z, la = _proj(x, gains, w_in_t, colscale, wa_p, ba, l)
        att, go, st_p = _mixer_prompt(z, la, attn_sinks[l], batch, nb, m_pad)

        zs = z[mp:mp + ms].reshape(dec_batch, dec_seq, Z_WIDTH)
        zs8 = jnp.pad(zs, ((0, 0), (0, rows8 - dec_seq), (0, 0)))
        la8 = jnp.pad(la[mp:mp + ms].reshape(dec_batch, dec_seq, GLA_KWIDTH), ((0, 0), (0, rows8 - dec_seq), (0, 0)))
        qa_s = zs[:, :, COL_QA:COL_QA + ATT_WIDTH].reshape(dec_batch, dec_seq, ATT_KV_HEADS, ATT_GROUP, ATT_HEAD_DIM)
        qa_s = qa_s.transpose(0, 2, 1, 3, 4).reshape(dec_batch, ATT_KV_HEADS, dec_seq * ATT_GROUP, ATT_HEAD_DIM)
        sink_rows = jnp.tile(attn_sinks[l].reshape(ATT_KV_HEADS, 1, ATT_GROUP), (1, dec_seq, 1))
        sink_rows = sink_rows.reshape(ATT_KV_HEADS, dec_seq * ATT_GROUP, 1)
        att_s, go_s, st_s = _sample_mixer(qa_s, zs8, la8, kbuf, vbuf, sink_rows, state_gla, dec_seq, l)
        att_s = att_s.reshape(dec_batch, ATT_KV_HEADS, dec_seq, ATT_GROUP, ATT_HEAD_DIM)
        att_s = att_s.transpose(0, 2, 1, 3, 4).reshape(ms, ATT_WIDTH)

        att = lax.dynamic_update_slice(att, jnp.concatenate([att_s, tail], axis=0).astype(BF16), (mp, 0))
        go = lax.dynamic_update_slice(go, jnp.concatenate([go_s[:, :dec_seq].reshape(ms, GLA_WIDTH), tail], axis=0), (mp, 0))
        x = _merge(x, att, go, z, g_gla, wo, gains, l)

        win = [z[b * lp + lp - WINDOW:(b + 1) * lp, COL_KA:COL_VA + KV_WIDTH] for b in range(batch)]
        win = jnp.stack(win).reshape(batch, WINDOW, 2, ATT_KV_HEADS, ATT_HEAD_DIM)
        pk.append(win[:, :, 0])
        pv.append(win[:, :, 1])
        ps.append(st_p.transpose(0, 1, 3, 2))
        kn = zs[:, :, COL_KA:COL_KA + KV_WIDTH].reshape(dec_batch, dec_seq, ATT_KV_HEADS, ATT_HEAD_DIM)
        vn = zs[:, :, COL_VA:COL_VA + KV_WIDTH].reshape(dec_batch, dec_seq, ATT_KV_HEADS, ATT_HEAD_DIM)
        sk.append(jnp.concatenate([cache_k_win[l], kn], axis=1)[:, -wb:])
        sv.append(jnp.concatenate([cache_v_win[l], vn], axis=1)[:, -wb:])
        ss.append(st_s)

        x = _ffn(x, gains, wg, wu, wd, l, 1)

    y_prompt = _extract_prompt(x, batch, seq, lp)
    y_sample = x[mp:mp + ms].reshape(dec_batch, dec_seq, D_MODEL)
    return (y_prompt, y_sample, jnp.stack(pk), jnp.stack(pv), jnp.stack(ps),
            jnp.stack(sk), jnp.stack(sv), jnp.stack(ss))
```

```python
import functools

import jax
import jax.numpy as jnp
from jax import lax
from jax.experimental import pallas as pl
from jax.experimental.pallas import tpu as pltpu

F32 = jnp.float32
BF16 = jnp.bfloat16

D_MODEL = 2048
D_FF = 5632
N_META = 16
BLOCK = 128
WINDOW = 128
ATT_HEADS = 16
ATT_KV_HEADS = 2
ATT_GROUP = 8
ATT_HEAD_DIM = 64
ATT_WIDTH = 1024
KV_WIDTH = 128
GLA_HEADS = 4
GLA_DK = 128
GLA_DV = 256
GLA_KWIDTH = 512
GLA_WIDTH = 1024
GLA_RANK = 16
GLA_GATE_NORM = 16.0
GLA_CHUNK = 64
GLA_SUB = 8
EPS = 1e-6
LOG2_E = 1.4426950408889634
PAD = BLOCK - N_META

COL_QA, COL_KA, COL_VA, COL_QG, COL_KG, COL_VG, COL_RG, COL_LR = 0, 1024, 1152, 1280, 1792, 2304, 3328, 4352
COL_TILE = 256
Z_WIDTH = 4608
LR_PAD = 128

VMEM_LIMIT = 56 * 1024 * 1024
VMEM_LIMIT_FFN = 60 * 1024 * 1024

TM = 1072
TM_MERGE = 536
TF = 256
TN = 768
X_PARTS = 4
SAMPLE_TILE = 4


def _rms(x, gain):
    ms = jnp.mean(x * x, axis=-1, keepdims=True)
    return x * lax.rsqrt(ms + EPS) * gain


def _dot(a, b):
    return jnp.dot(a, b, preferred_element_type=F32)


def _dot_nt(a, b):
    return lax.dot_general(a, b, (((1,), (1,)), ((), ())), preferred_element_type=F32)


def _dot_tn(a, b):
    return lax.dot_general(a, b, (((0,), (0,)), ((), ())), preferred_element_type=F32)


def _row_parts(n_rows, parts):
    units = n_rows // 8
    out, start = [], 0
    for r in range(parts):
        size = (units // parts + (1 if r < units % parts else 0)) * 8
        out.append((start, size))
        start += size
    assert start == n_rows
    return out


def _ffn_kernel(x_hbm, gpre_ref, gpost_ref, wg_hbm, wu_hbm, wd_hbm, o_ref, xn_ref, x_buf, wg_buf, wu_buf, wd_buf,
                sem, xsem, *, l, f):
    i = pl.program_id(0)
    n_tiles = pl.num_programs(0)
    n_chunks = D_FF // TF
    parts = _row_parts(TM, X_PARTS)

    def w_copies(j, slot):
        cols = pl.ds(j * TF, TF)
        return (pltpu.make_async_copy(wg_hbm.at[l, f, :, cols], wg_buf.at[slot], sem.at[0, slot]),
                pltpu.make_async_copy(wu_hbm.at[l, f, :, cols], wu_buf.at[slot], sem.at[1, slot]),
                pltpu.make_async_copy(wd_hbm.at[l, f, cols, :], wd_buf.at[slot], sem.at[2, slot]))

    def x_copy(tile, r):
        start, size = parts[r]
        return pltpu.make_async_copy(x_hbm.at[pl.ds(tile * TM + start, size), :],
                                     x_buf.at[pl.ds(start, size), :], xsem.at[r])

    @pl.when(i == 0)
    def _():
        for r in range(X_PARTS):
            x_copy(0, r).start()
        for c in w_copies(0, 0):
            c.start()

    for r, (start, size) in enumerate(parts):
        x_copy(i, r).wait()
        rows = pl.ds(start, size)
        xn_ref[rows, :] = _rms(x_buf[rows, :], gpre_ref[...]).astype(BF16)
    o_ref[...] = jnp.zeros_like(o_ref)

    def pair(p, carry):
        for slot in range(2):
            j = 2 * p + slot
            for c in w_copies(j, slot):
                c.wait()
            for c in w_copies(lax.rem(j + 1, n_chunks), 1 - slot):
                c.start()
            xn = xn_ref[...]
            g = _dot(xn, wg_buf[slot].astype(BF16))
            u = _dot(xn, wu_buf[slot].astype(BF16))
            h = (g * jax.nn.sigmoid(g) * u).astype(BF16)
            o_ref[...] += _dot(h, wd_buf[slot].astype(BF16))
        return carry

    lax.fori_loop(0, n_chunks // 2, pair, 0)

    for r, (start, size) in enumerate(parts):
        rows = pl.ds(start, size)
        o_ref[rows, :] = x_buf[rows, :] + 0.5 * _rms(o_ref[rows, :], gpost_ref[...])

        @pl.when(i + 1 < n_tiles)
        def _():
            x_copy(i + 1, r).start()

    @pl.when(i == n_tiles - 1)
    def _():
        for c in w_copies(0, 0):
            c.wait()


def _gain_spec(l, idx):
    return pl.BlockSpec((None, None, 1, D_MODEL), lambda *_: (l, idx, 0, 0))


def _ffn(x, gains, wg, wu, wd, l, f):
    m = x.shape[0]
    return pl.pallas_call(
        functools.partial(_ffn_kernel, l=l, f=f),
        grid=(m // TM,),
        in_specs=[
            pl.BlockSpec(memory_space=pl.ANY),
            _gain_spec(l, 4 * f),
            _gain_spec(l, 4 * f + 1),
            pl.BlockSpec(memory_space=pl.ANY),
            pl.BlockSpec(memory_space=pl.ANY),
            pl.BlockSpec(memory_space=pl.ANY),
        ],
        out_specs=pl.BlockSpec((TM, D_MODEL), lambda i: (i, 0)),
        out_shape=jax.ShapeDtypeStruct((m, D_MODEL), F32),
        scratch_shapes=[
            pltpu.VMEM((TM, D_MODEL), BF16),
            pltpu.VMEM((TM, D_MODEL), F32),
            pltpu.VMEM((2, D_MODEL, TF), F32),
            pltpu.VMEM((2, D_MODEL, TF), F32),
            pltpu.VMEM((2, TF, D_MODEL), F32),
            pltpu.SemaphoreType.DMA((3, 2)),
            pltpu.SemaphoreType.DMA((X_PARTS,)),
        ],
        compiler_params=pltpu.CompilerParams(
            dimension_semantics=("arbitrary",), vmem_limit_bytes=VMEM_LIMIT_FFN),
        name="ffn",
    )(x, gains, gains, wg, wu, wd)


def _proj_kernel(x_ref, g_ref, wt_ref, cs_ref, wa_ref, ba_ref, z_ref, la_ref, xn_ref, *, n_cols):
    j = pl.program_id(1)
    last = pl.num_programs(1) - 1
    n_valid = n_cols - (Z_WIDTH - TN)

    @pl.when(j == 0)
    def _():
        xn_ref[...] = _rms(x_ref[...], g_ref[...]).astype(BF16)

    @pl.when(j < last)
    def _():
        z_ref[...] = _dot_nt(xn_ref[...], wt_ref[...].astype(BF16)) * cs_ref[...]

    @pl.when(j == last)
    def _():
        z_ref[:, :n_valid] = _dot_nt(xn_ref[...], wt_ref[:n_valid, :].astype(BF16)) * cs_ref[:, :n_valid]
        z_ref[:, n_valid:] = jnp.zeros((z_ref.shape[0], TN - n_valid), F32)
        lr_off = COL_LR - (Z_WIDTH - TN)
        lr = z_ref[:, lr_off:lr_off + LR_PAD].astype(BF16)
        logit = _dot(lr, wa_ref[...]) + ba_ref[...]
        log_sig = jnp.minimum(logit, 0.0) - jnp.log1p(jnp.exp(-jnp.abs(logit)))
        la_ref[...] = log_sig * (1.0 / GLA_GATE_NORM)


def _proj(x, gains, w_in_t, colscale, wa_p, ba, l):
    m = x.shape[0]
    n_cols = w_in_t.shape[1]
    assert Z_WIDTH - TN < COL_LR and COL_LR + GLA_RANK <= n_cols <= Z_WIDTH
    return pl.pallas_call(
        functools.partial(_proj_kernel, n_cols=n_cols),
        grid=(m // TM, Z_WIDTH // TN),
        in_specs=[
            pl.BlockSpec((TM, D_MODEL), lambda i, j: (i, 0)),
            _gain_spec(l, 2),
            pl.BlockSpec((None, TN, D_MODEL), lambda i, j: (l, j, 0)),
            pl.BlockSpec((1, TN), lambda i, j: (0, j)),
            pl.BlockSpec((None, LR_PAD, GLA_KWIDTH), lambda i, j: (l, 0, 0)),
            pl.BlockSpec((None, 1, GLA_KWIDTH), lambda i, j: (l, 0, 0)),
        ],
        out_specs=[
            pl.BlockSpec((TM, TN), lambda i, j: (i, j)),
            pl.BlockSpec((TM, GLA_KWIDTH), lambda i, j: (i, 0)),
        ],
        out_shape=[
            jax.ShapeDtypeStruct((m, Z_WIDTH), F32),
            jax.ShapeDtypeStruct((m, GLA_KWIDTH), F32),
        ],
        scratch_shapes=[pltpu.VMEM((TM, D_MODEL), BF16)],
        compiler_params=pltpu.CompilerParams(
            dimension_semantics=("parallel", "arbitrary"), vmem_limit_bytes=VMEM_LIMIT),
        name="proj",
    )(x, gains, w_in_t, colscale, wa_p, ba)


def _pair_blockdiag(x128, kv):
    lane = lax.broadcasted_iota(jnp.int32, x128.shape, 1)
    own = jnp.where((lane >= kv * ATT_HEAD_DIM) & (lane < (kv + 1) * ATT_HEAD_DIM), x128, 0.0)
    other = pltpu.roll(own, ATT_HEAD_DIM, axis=1)
    lo, hi = (own, other) if kv == 0 else (other, own)
    return jnp.concatenate([lo, hi], axis=0).astype(BF16)


def _attn_block(i, sink_ref, q_ref, kc_ref, kp_ref, vc_ref, vp_ref, o_ref):
    kb = 2 * BLOCK
    row = lax.broadcasted_iota(jnp.int32, (BLOCK, kb), 0)
    col = lax.broadcasted_iota(jnp.int32, (BLOCK, kb), 1)
    diff = row + BLOCK - col
    key_pos = col + (i - 1) * BLOCK
    mask = (diff >= 0) & (diff <= WINDOW) & (key_pos >= PAD)
    lane = lax.broadcasted_iota(jnp.int32, (BLOCK, 2 * ATT_HEAD_DIM), 1)
    kk = jnp.concatenate([kp_ref[...], kc_ref[...]], axis=0)
    vv = jnp.concatenate([vp_ref[...], vc_ref[...]], axis=0)
    slabs = ATT_GROUP // 2
    for kv in range(ATT_KV_HEADS):
        k2 = _pair_blockdiag(kk, kv)
        v2 = _pair_blockdiag(vv, kv)
        c0 = kv * slabs * 2 * ATT_HEAD_DIM
        q4 = q_ref[:, c0:c0 + slabs * 2 * ATT_HEAD_DIM]
        q4 = jnp.concatenate([q4[:, p * 128:(p + 1) * 128] for p in range(slabs)], axis=0).astype(BF16)
        s = _dot_nt(q4, k2)
        probs, inv = [], []
        for p in range(slabs):
            halves, rden = [], []
            for e in range(2):
                sink = sink_ref[kv * ATT_GROUP + 2 * p + e]
                sp = jnp.where(mask, s[p * BLOCK:(p + 1) * BLOCK, e * kb:(e + 1) * kb], -jnp.inf)
                mx = jnp.maximum(jnp.max(sp, axis=-1, keepdims=True), sink)
                pe = jnp.exp(sp - mx)
                rden.append(1.0 / (jnp.sum(pe, axis=-1, keepdims=True) + jnp.exp(sink - mx)))
                halves.append(pe.astype(BF16))
            probs.append(jnp.concatenate(halves, axis=1))
            inv.append(jnp.where(lane < ATT_HEAD_DIM, rden[0], rden[1]))
        o = _dot(jnp.concatenate(probs, axis=0), v2)
        for p in range(slabs):
            o_ref[:, c0 + p * 128:c0 + (p + 1) * 128] = (o[p * BLOCK:(p + 1) * BLOCK] * inv[p]).astype(o_ref.dtype)


def _cumsum_rows(la, tri):
    hi = la.astype(BF16)
    r1 = la - hi.astype(F32)
    mid = r1.astype(BF16)
    lo = (r1 - mid.astype(F32)).astype(BF16)
    return _dot(tri, hi) + _dot(tri, mid) + _dot(tri, lo)


def _gla_chunk(c, rows, q_refs, k_refs, v_refs, la_ref, o_ref, state_ref, maybe_first):
    per = COL_TILE // GLA_DK
    C = GLA_CHUNK

    rowc = lax.broadcasted_iota(jnp.int32, (C, C), 0)
    colc = lax.broadcasted_iota(jnp.int32, (C, C), 1)
    tri = (rowc >= colc).astype(BF16)
    b_all = _cumsum_rows(la_ref[rows, :], tri) * LOG2_E

    pos = c * C + lax.broadcasted_iota(jnp.int32, (C, 1), 0)
    valid = (pos >= PAD).astype(F32)

    lane = lax.broadcasted_iota(jnp.int32, (GLA_SUB, C), 1)
    rsub = lax.broadcasted_iota(jnp.int32, (GLA_SUB, C), 0)
    scale = GLA_DK ** -0.5

    heads = range(GLA_HEADS)
    q, k, bh = [], [], []
    for h in heads:
        sub = slice((h % per) * GLA_DK, (h % per + 1) * GLA_DK)
        q.append(q_refs[h // per][rows, sub] * scale)
        k.append(k_refs[h // per][rows, sub] * valid)
        bh.append(b_all[:, h * GLA_DK:(h + 1) * GLA_DK])

    blocks = [[] for _ in heads]
    for i in range(C // GLA_SUB):
        lo_r = i * GLA_SUB
        sl = slice(lo_r, lo_r + GLA_SUB)
        w = [jnp.zeros((GLA_SUB, C), F32) for _ in heads]
        for s in range(GLA_SUB):
            r = lo_r + s
            at_r = lane == r
            for h in heads:
                zed = q[h][sl] * k[h][r:r + 1] * jnp.exp2(bh[h][sl] - bh[h][r:r + 1])
                w[h] = jnp.where(at_r, jnp.sum(zed, axis=-1, keepdims=True), w[h])
        causal = lane <= rsub + lo_r
        for h in heads:
            wh = jnp.where(causal, w[h], 0.0)
            if i > 0:
                ref_b = bh[h][lo_r - 1:lo_r]
                qi = (q[h][sl] * jnp.exp2(bh[h][sl] - ref_b)).astype(BF16)
                kj = k[h][:lo_r] * jnp.exp2(ref_b - bh[h][:lo_r])
                kj = jnp.concatenate([kj, jnp.zeros((C - lo_r, GLA_DK), F32)], axis=0).astype(BF16)
                wh = jnp.where(lane < lo_r, _dot_nt(qi, kj), wh)
            blocks[h].append(wh)

    for h in heads:
        v = v_refs[h][rows, :].astype(BF16)
        bl = bh[h][C - 1:C, :]
        st = state_ref[h]
        if maybe_first:
            st = jnp.where(c == 0, 0.0, st)
        o = _dot_nt((q[h] * jnp.exp2(bh[h])).astype(BF16), st.astype(BF16))
        a = jnp.concatenate(blocks[h], axis=0).astype(BF16)
        o_ref[rows, h * GLA_DV:(h + 1) * GLA_DV] = o + _dot(a, v)
        k_out = (k[h] * jnp.exp2(bl - bh[h])).astype(BF16)
        state_ref[h] = st * jnp.exp2(bl) + _dot_tn(v, k_out)


def _mixer_kernel(sink_ref, *refs, nb, n_blocks):
    nq = GLA_KWIDTH // COL_TILE
    attn_in, rest = refs[:5], refs[5:]
    q_refs, k_refs, v_refs = rest[:nq], rest[nq:2 * nq], rest[2 * nq:2 * nq + GLA_HEADS]
    la_ref, att_ref, go_ref, st_ref, state_ref = rest[2 * nq + GLA_HEADS:]
    step = pl.program_id(0)

    @pl.when(step < n_blocks)
    def _():
        i = lax.rem(step, nb)
        _attn_block(i, sink_ref, *attn_in, att_ref)
        per_block = BLOCK // GLA_CHUNK
        for half in range(per_block):
            _gla_chunk(i * per_block + half, pl.ds(half * GLA_CHUNK, GLA_CHUNK), q_refs, k_refs, v_refs, la_ref,
                       go_ref, state_ref, maybe_first=(half == 0))
        st_ref[...] = state_ref[...]

    @pl.when(step >= n_blocks)
    def _():
        att_ref[...] = jnp.zeros_like(att_ref)
        go_ref[...] = jnp.zeros_like(go_ref)


def _mixer_prompt(z, la, sinks, batch, nb, m_rows):
    n_blocks = batch * nb
    nq = GLA_KWIDTH // COL_TILE

    def cur(c):
        return lambda s, sink: (jnp.minimum(s, n_blocks - 1), c)

    def prev(c):
        return lambda s, sink: (jnp.maximum(jnp.minimum(s, n_blocks - 1) - 1, 0), c)

    def rows(c):
        return lambda s, sink: (s, c)

    return pl.pallas_call(
        functools.partial(_mixer_kernel, nb=nb, n_blocks=n_blocks),
        grid_spec=pltpu.PrefetchScalarGridSpec(
            num_scalar_prefetch=1,
            grid=(m_rows // BLOCK,),
            in_specs=(
                [pl.BlockSpec((BLOCK, ATT_WIDTH), cur(COL_QA // ATT_WIDTH)),
                 pl.BlockSpec((BLOCK, KV_WIDTH), cur(COL_KA // KV_WIDTH)),
                 pl.BlockSpec((BLOCK, KV_WIDTH), prev(COL_KA // KV_WIDTH)),
                 pl.BlockSpec((BLOCK, KV_WIDTH), cur(COL_VA // KV_WIDTH)),
                 pl.BlockSpec((BLOCK, KV_WIDTH), prev(COL_VA // KV_WIDTH))]
                + [pl.BlockSpec((BLOCK, COL_TILE), rows(COL_QG // COL_TILE + t)) for t in range(nq)]
                + [pl.BlockSpec((BLOCK, COL_TILE), rows(COL_KG // COL_TILE + t)) for t in range(nq)]
                + [pl.BlockSpec((BLOCK, GLA_DV), rows(COL_VG // GLA_DV + t)) for t in range(GLA_HEADS)]
                + [pl.BlockSpec((BLOCK, GLA_KWIDTH), rows(0))]
            ),
            out_specs=[
                pl.BlockSpec((BLOCK, ATT_WIDTH), rows(0)),
                pl.BlockSpec((BLOCK, GLA_WIDTH), rows(0)),
                pl.BlockSpec((None, GLA_HEADS, GLA_DV, GLA_DK),
                             lambda s, sink: (jnp.minimum(s // nb, batch - 1), 0, 0, 0)),
            ],
            scratch_shapes=[pltpu.VMEM((GLA_HEADS, GLA_DV, GLA_DK), F32)],
        ),
        out_shape=[
            jax.ShapeDtypeStruct((m_rows, ATT_WIDTH), BF16),
            jax.ShapeDtypeStruct((m_rows, GLA_WIDTH), F32),
            jax.ShapeDtypeStruct((batch, GLA_HEADS, GLA_DV, GLA_DK), F32),
        ],
        compiler_params=pltpu.CompilerParams(
            dimension_semantics=("arbitrary",), vmem_limit_bytes=VMEM_LIMIT),
        name="mixer_prompt",
    )(sinks, *([z] * (5 + 2 * nq + GLA_HEADS)), la)


def _sample_kernel(qa_ref, zs_ref, la_ref, kbuf_ref, vbuf_ref, sink_ref, s0_ref,
                   att_ref, go_ref, st_ref, *, dec_seq):
    for b in range(qa_ref.shape[0]):
        _sample_one(qa_ref.at[b], zs_ref.at[b], la_ref.at[b], kbuf_ref.at[b], vbuf_ref.at[b], sink_ref,
                    s0_ref.at[b], att_ref.at[b], go_ref.at[b], st_ref.at[b], dec_seq=dec_seq)


def _sample_one(qa_ref, zs_ref, la_ref, kbuf_ref, vbuf_ref, sink_ref, s0_ref,
                att_ref, go_ref, st_ref, *, dec_seq):
    S = dec_seq
    R = S * ATT_GROUP
    t_row = lax.broadcasted_iota(jnp.int32, (R, 1), 0) // ATT_GROUP
    c_idx = lax.broadcasted_iota(jnp.int32, (R, kbuf_ref.shape[0]), 1)
    mask_c = c_idx >= t_row
    for kv in range(ATT_KV_HEADS):
        ks = slice(kv * ATT_HEAD_DIM, (kv + 1) * ATT_HEAD_DIM)
        q = qa_ref[kv]
        sink = sink_ref[kv]
        sc = _dot_nt(q.astype(BF16), kbuf_ref[:, ks].astype(BF16))
        sc = jnp.where(mask_c, sc, -jnp.inf)
        mx = jnp.maximum(jnp.max(sc, axis=-1, keepdims=True), sink)
        kn = zs_ref[:, COL_KA + kv * ATT_HEAD_DIM:COL_KA + (kv + 1) * ATT_HEAD_DIM]
        vn = zs_ref[:, COL_VA + kv * ATT_HEAD_DIM:COL_VA + (kv + 1) * ATT_HEAD_DIM]
        sn = []
        for s in range(S):
            v = jnp.sum(q * kn[s:s + 1], axis=-1, keepdims=True)
            v = jnp.where(t_row >= s, v, -jnp.inf)
            sn.append(v)
            mx = jnp.maximum(mx, v)
        pc = jnp.exp(sc - mx)
        den = jnp.sum(pc, axis=-1, keepdims=True) + jnp.exp(sink - mx)
        o = _dot(pc.astype(BF16), vbuf_ref[:, ks].astype(BF16))
        for s in range(S):
            pn = jnp.exp(sn[s] - mx)
            den = den + pn
            o = o + pn * vn[s:s + 1]
        att_ref[kv] = o / den

    rows8 = zs_ref.shape[0]
    rid = lax.broadcasted_iota(jnp.int32, (rows8, 1), 0)
    la = la_ref[...]
    run = la[0:1]
    b_all = jnp.where(rid == 0, run, 0.0)
    for t in range(1, S):
        run = run + la[t:t + 1]
        b_all = jnp.where(rid == t, run, b_all)
    real = (rid < S).astype(F32)
    scale = GLA_DK ** -0.5
    for h in range(GLA_HEADS):
        ks = slice(h * GLA_DK, (h + 1) * GLA_DK)
        q = zs_ref[:, COL_QG + h * GLA_DK:COL_QG + (h + 1) * GLA_DK] * scale
        k = zs_ref[:, COL_KG + h * GLA_DK:COL_KG + (h + 1) * GLA_DK]
        v = zs_ref[:, COL_VG + h * GLA_DV:COL_VG + (h + 1) * GLA_DV]
        bh = b_all[:, ks] * real
        bl = bh[S - 1:S]
        s0 = s0_ref[h]
        k_out = k * jnp.exp(jnp.minimum(bl - bh, 0.0)) * real
        q_in = q * jnp.exp(bh) * real
        o = _dot(q_in.astype(BF16), s0.astype(BF16))
        for s in range(S):
            zed = q * k[s:s + 1] * jnp.exp(jnp.minimum(bh - bh[s:s + 1], 0.0))
            w = jnp.sum(zed, axis=-1, keepdims=True)
            w = jnp.where((rid >= s) & (rid < S), w, 0.0)
            o = o + w * v[s:s + 1]
        go_ref[:, h * GLA_DV:(h + 1) * GLA_DV] = o
        e_rows = jnp.broadcast_to(jnp.exp(bl), (GLA_DK, GLA_DK)).T
        decay = jnp.concatenate([e_rows] * (GLA_DV // GLA_DK), axis=1)
        st_ref[h] = decay * s0 + _dot_tn(k_out.astype(BF16), v.astype(BF16))


def _sample_mixer(qa_s, zs8, la8, kbuf, vbuf, sink_rows, s0, dec_seq, l):
    nb = zs8.shape[0]
    rows8 = zs8.shape[1]
    R = dec_seq * ATT_GROUP
    wb = kbuf.shape[2]
    return pl.pallas_call(
        functools.partial(_sample_kernel, dec_seq=dec_seq),
        grid=(nb // SAMPLE_TILE,),
        in_specs=[
            pl.BlockSpec((SAMPLE_TILE, ATT_KV_HEADS, R, ATT_HEAD_DIM), lambda b: (b, 0, 0, 0)),
            pl.BlockSpec((SAMPLE_TILE, rows8, Z_WIDTH), lambda b: (b, 0, 0)),
            pl.BlockSpec((SAMPLE_TILE, rows8, GLA_KWIDTH), lambda b: (b, 0, 0)),
            pl.BlockSpec((None, SAMPLE_TILE, wb, KV_WIDTH), lambda b: (l, b, 0, 0)),
            pl.BlockSpec((None, SAMPLE_TILE, wb, KV_WIDTH), lambda b: (l, b, 0, 0)),
            pl.BlockSpec((ATT_KV_HEADS, R, 1), lambda b: (0, 0, 0)),
            pl.BlockSpec((None, SAMPLE_TILE, GLA_HEADS, GLA_DK, GLA_DV), lambda b: (l, b, 0, 0, 0)),
        ],
        out_specs=[
            pl.BlockSpec((SAMPLE_TILE, ATT_KV_HEADS, R, ATT_HEAD_DIM), lambda b: (b, 0, 0, 0)),
            pl.BlockSpec((SAMPLE_TILE, rows8, GLA_WIDTH), lambda b: (b, 0, 0)),
            pl.BlockSpec((SAMPLE_TILE, GLA_HEADS, GLA_DK, GLA_DV), lambda b: (b, 0, 0, 0)),
        ],
        out_shape=[
            jax.ShapeDtypeStruct((nb, ATT_KV_HEADS, R, ATT_HEAD_DIM), F32),
            jax.ShapeDtypeStruct((nb, rows8, GLA_WIDTH), F32),
            jax.ShapeDtypeStruct((nb, GLA_HEADS, GLA_DK, GLA_DV), F32),
        ],
        compiler_params=pltpu.CompilerParams(
            dimension_semantics=("parallel",), vmem_limit_bytes=VMEM_LIMIT),
        name="sample_mixer",
    )(qa_s, zs8, la8, kbuf, vbuf, sink_rows, s0)


def _merge_kernel(*refs):
    x_ref, att_ref, go_ref = refs[:3]
    rg_refs = refs[3:3 + GLA_HEADS]
    ggla_ref, wo_ref, gpost_ref, o_ref = refs[3 + GLA_HEADS:]
    parts = []
    for h in range(GLA_HEADS):
        vs = slice(h * GLA_DV, (h + 1) * GLA_DV)
        rg = rg_refs[h][...]
        parts.append((_rms(go_ref[:, vs], ggla_ref[...]) * (rg * jax.nn.sigmoid(rg))).astype(BF16))
    go = jnp.concatenate(parts, axis=-1)
    y = _dot(att_ref[...], wo_ref[:ATT_WIDTH, :]) + _dot(go, wo_ref[ATT_WIDTH:, :])
    o_ref[...] = x_ref[...] + _rms(y, gpost_ref[...])


def _merge(x, att, go, z, g_gla, w_out, gains, l):
    m = x.shape[0]
    return pl.pallas_call(
        _merge_kernel,
        grid=(m // TM_MERGE,),
        in_specs=(
            [pl.BlockSpec((TM_MERGE, D_MODEL), lambda i: (i, 0)),
             pl.BlockSpec((TM_MERGE, ATT_WIDTH), lambda i: (i, 0)),
             pl.BlockSpec((TM_MERGE, GLA_WIDTH), lambda i: (i, 0))]
            + [pl.BlockSpec((TM_MERGE, GLA_DV), lambda i, h=h: (i, COL_RG // GLA_DV + h)) for h in range(GLA_HEADS)]
            + [pl.BlockSpec((None, 1, GLA_DV), lambda i: (l, 0, 0)),
               pl.BlockSpec((None, ATT_WIDTH + GLA_WIDTH, D_MODEL), lambda i: (l, 0, 0)),
               _gain_spec(l, 3)]
        ),
        out_specs=pl.BlockSpec((TM_MERGE, D_MODEL), lambda i: (i, 0)),
        out_shape=jax.ShapeDtypeStruct((m, D_MODEL), F32),
        compiler_params=pltpu.CompilerParams(
            dimension_semantics=("parallel",), vmem_limit_bytes=VMEM_LIMIT),
        name="merge",
    )(x, att, go, *([z] * GLA_HEADS), g_gla, w_out, gains)


def _copy_kernel(x_ref, y_ref):
    y_ref[...] = x_ref[...]


def _extract_prompt(x, batch, seq, lp):
    skip = (lp - seq) // BLOCK
    return pl.pallas_call(
        _copy_kernel,
        grid=(batch, seq // BLOCK),
        in_specs=[pl.BlockSpec((BLOCK, x.shape[1]), lambda b, k: (b * (lp // BLOCK) + skip + k, 0))],
        out_specs=pl.BlockSpec((None, BLOCK, x.shape[1]), lambda b, k: (b, k, 0)),
        out_shape=jax.ShapeDtypeStruct((batch, seq, x.shape[1]), x.dtype),
        compiler_params=pltpu.CompilerParams(dimension_semantics=("parallel", "parallel")),
        name="extract_prompt",
    )(x)


def kernel(x_prompt, x_sample, cache_k_win, cache_v_win, state_gla, meta_tokens, norm_gains,
           w_ffn_gate, w_ffn_up, w_ffn_down, w_in, w_gate_up, b_gate, attn_sinks, gla_norm, w_out):
    batch, seq, _ = x_prompt.shape
    dec_batch, dec_seq, _ = x_sample.shape
    depth = norm_gains.shape[0]
    lp = seq + BLOCK
    nb = lp // BLOCK
    mp = batch * lp
    ms = dec_batch * dec_seq
    m_pad = -(-(mp + ms) // TM) * TM
    wb = cache_k_win.shape[2]
    rows8 = 8
    assert dec_seq <= rows8 and lp % BLOCK == 0 and dec_batch % SAMPLE_TILE == 0

    head = jnp.concatenate([jnp.zeros((PAD, D_MODEL), F32), meta_tokens.astype(F32)], axis=0)
    pieces = []
    for b in range(batch):
        pieces += [head, x_prompt[b]]
    pieces += [x_sample.reshape(ms, D_MODEL), jnp.zeros((m_pad - mp - ms, D_MODEL), F32)]
    x = jnp.concatenate(pieces, axis=0)

    wg, wu, wd = w_ffn_gate, w_ffn_up, w_ffn_down
    wo = w_out.astype(BF16)
    w_in_t = jnp.swapaxes(w_in, 1, 2)
    col = jnp.arange(Z_WIDTH)
    colscale = jnp.where((col >= COL_QA) & (col < COL_QA + ATT_WIDTH), ATT_HEAD_DIM ** -0.5, 1.0).astype(F32)[None]
    wa_p = jnp.pad(w_gate_up, ((0, 0), (0, LR_PAD - GLA_RANK), (0, 0))).astype(BF16)
    ba = b_gate.reshape(depth, 1, GLA_KWIDTH)
    g_gla = gla_norm.reshape(depth, 1, GLA_DV)
    gains = norm_gains.reshape(depth, 6, 1, D_MODEL)
    kbuf = cache_k_win.reshape(depth, dec_batch, wb, KV_WIDTH)
    vbuf = cache_v_win.reshape(depth, dec_batch, wb, KV_WIDTH)
    tail = jnp.zeros((m_pad - mp - ms, GLA_WIDTH), F32)

    pk, pv, ps, sk, sv, ss = [], [], [], [], [], []
    for l in range(depth):
        x = _ffn(x, gains, wg, wu, wd, l, 0)

        z, la = _proj(x, gains, w_in_t, colscale, wa_p, ba, l)
        att, go, st_p = _mixer_prompt(z, la, attn_sinks[l], batch, nb, m_pad)

        zs = z[mp:mp + ms].reshape(dec_batch, dec_seq, Z_WIDTH)
        zs8 = jnp.pad(zs, ((0, 0), (0, rows8 - dec_seq), (0, 0)))
        la8 = jnp.pad(la[mp:mp + ms].reshape(dec_batch, dec_seq, GLA_KWIDTH), ((0, 0), (0, rows8 - dec_seq), (0, 0)))
        qa_s = zs[:, :, COL_QA:COL_QA + ATT_WIDTH].reshape(dec_batch, dec_seq, ATT_KV_HEADS, ATT_GROUP, ATT_HEAD_DIM)
        qa_s = qa_s.transpose(0, 2, 1, 3, 4).reshape(dec_batch, ATT_KV_HEADS, dec_seq * ATT_GROUP, ATT_HEAD_DIM)
        sink_rows = jnp.tile(attn_sinks[l].reshape(ATT_KV_HEADS, 1, ATT_GROUP), (1, dec_seq, 1))
        sink_rows = sink_rows.reshape(ATT_KV_HEADS, dec_seq * ATT_GROUP, 1)
        att_s, go_s, st_s = _sample_mixer(qa_s, zs8, la8, kbuf, vbuf, sink_rows, state_gla, dec_seq, l)
        att_s = att_s.reshape(dec_batch, ATT_KV_HEADS, dec_seq, ATT_GROUP, ATT_HEAD_DIM)
        att_s = att_s.transpose(0, 2, 1, 3, 4).reshape(ms, ATT_WIDTH)

        att = lax.dynamic_update_slice(att, jnp.concatenate([att_s, tail], axis=0).astype(BF16), (mp, 0))
        go = lax.dynamic_update_slice(go, jnp.concatenate([go_s[:, :dec_seq].reshape(ms, GLA_WIDTH), tail], axis=0), (mp, 0))
        x = _merge(x, att, go, z, g_gla, wo, gains, l)

        win = [z[b * lp + lp - WINDOW:(b + 1) * lp, COL_KA:COL_VA + KV_WIDTH] for b in range(batch)]
        win = jnp.stack(win).reshape(batch, WINDOW, 2, ATT_KV_HEADS, ATT_HEAD_DIM)
        pk.append(win[:, :, 0])
        pv.append(win[:, :, 1])
        ps.append(st_p.transpose(0, 1, 3, 2))
        kn = zs[:, :, COL_KA:COL_KA + KV_WIDTH].reshape(dec_batch, dec_seq, ATT_KV_HEADS, ATT_HEAD_DIM)
        vn = zs[:, :, COL_VA:COL_VA + KV_WIDTH].reshape(dec_batch, dec_seq, ATT_KV_HEADS, ATT_HEAD_DIM)
        sk.append(jnp.concatenate([cache_k_win[l], kn], axis=1)[:, -wb:])
        sv.append(jnp.concatenate([cache_v_win[l], vn], axis=1)[:, -wb:])
        ss.append(st_s)

        x = _ffn(x, gains, wg, wu, wd, l, 1)

    y_prompt = _extract_prompt(x, batch, seq, lp)
    y_sample = x[mp:mp + ms].reshape(dec_batch, dec_seq, D_MODEL)
    return (y_prompt, y_sample, jnp.stack(pk), jnp.stack(pv), jnp.stack(ps),
            jnp.stack(sk), jnp.stack(sv), jnp.stack(ss))
```

```python
import functools

import jax
import jax.numpy as jnp
from jax import lax
from jax.experimental import pallas as pl
from jax.experimental.pallas import tpu as pltpu

F32 = jnp.float32
BF16 = jnp.bfloat16

D_MODEL = 2048
D_FF = 5632
N_META = 16
BLOCK = 128
WINDOW = 128
ATT_HEADS = 16
ATT_KV_HEADS = 2
ATT_GROUP = 8
ATT_HEAD_DIM = 64
ATT_WIDTH = 1024
KV_WIDTH = 128
GLA_HEADS = 4
GLA_DK = 128
GLA_DV = 256
GLA_KWIDTH = 512
GLA_WIDTH = 1024
GLA_RANK = 16
GLA_GATE_NORM = 16.0
GLA_CHUNK = 64
GLA_SUB = 8
EPS = 1e-6
LOG2_E = 1.4426950408889634
PAD = BLOCK - N_META

COL_QA, COL_KA, COL_VA, COL_QG, COL_KG, COL_VG, COL_RG, COL_LR = 0, 1024, 1152, 1280, 1792, 2304, 3328, 4352
COL_TILE = 256
Z_WIDTH = 4608
LR_PAD = 128

VMEM_LIMIT = 56 * 1024 * 1024
VMEM_LIMIT_FFN = 60 * 1024 * 1024

TM = 1072
TM_MERGE = 536
TF = 256
TN = 768
X_PARTS = 4
SAMPLE_TILE = 4
COPY_BLOCKS = 4


def _rms(x, gain):
    ms = jnp.mean(x * x, axis=-1, keepdims=True)
    return x * lax.rsqrt(ms + EPS) * gain


def _dot(a, b):
    return jnp.dot(a, b, preferred_element_type=F32)


def _dot_nt(a, b):
    return lax.dot_general(a, b, (((1,), (1,)), ((), ())), preferred_element_type=F32)


def _dot_tn(a, b):
    return lax.dot_general(a, b, (((0,), (0,)), ((), ())), preferred_element_type=F32)


def _row_parts(n_rows, parts):
    units = n_rows // 8
    out, start = [], 0
    for r in range(parts):
        size = (units // parts + (1 if r < units % parts else 0)) * 8
        out.append((start, size))
        start += size
    assert start == n_rows
    return out


def _ffn_kernel(x_hbm, gpre_ref, gpost_ref, wg_hbm, wu_hbm, wd_hbm, o_ref, xn_ref, x_buf, wg_buf, wu_buf, wd_buf,
                sem, xsem, *, l, f):
    i = pl.program_id(0)
    n_tiles = pl.num_programs(0)
    n_chunks = D_FF // TF
    parts = _row_parts(TM, X_PARTS)

    def w_copies(j, slot):
        cols = pl.ds(j * TF, TF)
        return (pltpu.make_async_copy(wg_hbm.at[l, f, :, cols], wg_buf.at[slot], sem.at[0, slot]),
                pltpu.make_async_copy(wu_hbm.at[l, f, :, cols], wu_buf.at[slot], sem.at[1, slot]),
                pltpu.make_async_copy(wd_hbm.at[l, f, cols, :], wd_buf.at[slot], sem.at[2, slot]))

    def x_copy(tile, r):
        start, size = parts[r]
        return pltpu.make_async_copy(x_hbm.at[pl.ds(tile * TM + start, size), :],
                                     x_buf.at[pl.ds(start, size), :], xsem.at[r])

    @pl.when(i == 0)
    def _():
        for r in range(X_PARTS):
            x_copy(0, r).start()
        for c in w_copies(0, 0):
            c.start()

    for r, (start, size) in enumerate(parts):
        x_copy(i, r).wait()
        rows = pl.ds(start, size)
        xn_ref[rows, :] = _rms(x_buf[rows, :], gpre_ref[...]).astype(BF16)
    o_ref[...] = jnp.zeros_like(o_ref)

    def pair(p, carry):
        for slot in range(2):
            j = 2 * p + slot
            for c in w_copies(j, slot):
                c.wait()
            for c in w_copies(lax.rem(j + 1, n_chunks), 1 - slot):
                c.start()
            xn = xn_ref[...]
            g = _dot(xn, wg_buf[slot].astype(BF16))
            u = _dot(xn, wu_buf[slot].astype(BF16))
            h = (g * jax.nn.sigmoid(g) * u).astype(BF16)
            o_ref[...] += _dot(h, wd_buf[slot].astype(BF16))
        return carry

    lax.fori_loop(0, n_chunks // 2, pair, 0)

    for r, (start, size) in enumerate(parts):
        rows = pl.ds(start, size)
        o_ref[rows, :] = x_buf[rows, :] + 0.5 * _rms(o_ref[rows, :], gpost_ref[...])

        @pl.when(i + 1 < n_tiles)
        def _():
            x_copy(i + 1, r).start()

    @pl.when(i == n_tiles - 1)
    def _():
        for c in w_copies(0, 0):
            c.wait()


def _gain_spec(l, idx):
    return pl.BlockSpec((None, None, 1, D_MODEL), lambda *_: (l, idx, 0, 0))


def _ffn(x, gains, wg, wu, wd, l, f):
    m = x.shape[0]
    return pl.pallas_call(
        functools.partial(_ffn_kernel, l=l, f=f),
        grid=(m // TM,),
        in_specs=[
            pl.BlockSpec(memory_space=pl.ANY),
            _gain_spec(l, 4 * f),
            _gain_spec(l, 4 * f + 1),
            pl.BlockSpec(memory_space=pl.ANY),
            pl.BlockSpec(memory_space=pl.ANY),
            pl.BlockSpec(memory_space=pl.ANY),
        ],
        out_specs=pl.BlockSpec((TM, D_MODEL), lambda i: (i, 0)),
        out_shape=jax.ShapeDtypeStruct((m, D_MODEL), F32),
        scratch_shapes=[
            pltpu.VMEM((TM, D_MODEL), BF16),
            pltpu.VMEM((TM, D_MODEL), F32),
            pltpu.VMEM((2, D_MODEL, TF), F32),
            pltpu.VMEM((2, D_MODEL, TF), F32),
            pltpu.VMEM((2, TF, D_MODEL), F32),
            pltpu.SemaphoreType.DMA((3, 2)),
            pltpu.SemaphoreType.DMA((X_PARTS,)),
        ],
        compiler_params=pltpu.CompilerParams(
            dimension_semantics=("arbitrary",), vmem_limit_bytes=VMEM_LIMIT_FFN),
        name="ffn",
    )(x, gains, gains, wg, wu, wd)


def _proj_kernel(x_ref, g_ref, wt_ref, cs_ref, wa_ref, ba_ref, z_ref, la_ref, xn_ref, *, n_cols):
    j = pl.program_id(1)
    last = pl.num_programs(1) - 1
    n_valid = n_cols - (Z_WIDTH - TN)

    @pl.when(j == 0)
    def _():
        xn_ref[...] = _rms(x_ref[...], g_ref[...]).astype(BF16)

    @pl.when(j < last)
    def _():
        z_ref[...] = _dot_nt(xn_ref[...], wt_ref[...].astype(BF16)) * cs_ref[...]

    @pl.when(j == last)
    def _():
        z_ref[:, :n_valid] = _dot_nt(xn_ref[...], wt_ref[:n_valid, :].astype(BF16)) * cs_ref[:, :n_valid]
        z_ref[:, n_valid:] = jnp.zeros((z_ref.shape[0], TN - n_valid), F32)
        lr_off = COL_LR - (Z_WIDTH - TN)
        lr = z_ref[:, lr_off:lr_off + LR_PAD].astype(BF16)
        logit = _dot(lr, wa_ref[...]) + ba_ref[...]
        log_sig = jnp.minimum(logit, 0.0) - jnp.log1p(jnp.exp(-jnp.abs(logit)))
        la_ref[...] = log_sig * (1.0 / GLA_GATE_NORM)


def _proj(x, gains, w_in_t, colscale, wa_p, ba, l):
    m = x.shape[0]
    n_cols = w_in_t.shape[1]
    assert Z_WIDTH - TN < COL_LR and COL_LR + GLA_RANK <= n_cols <= Z_WIDTH
    return pl.pallas_call(
        functools.partial(_proj_kernel, n_cols=n_cols),
        grid=(m // TM, Z_WIDTH // TN),
        in_specs=[
            pl.BlockSpec((TM, D_MODEL), lambda i, j: (i, 0)),
            _gain_spec(l, 2),
            pl.BlockSpec((None, TN, D_MODEL), lambda i, j: (l, j, 0)),
            pl.BlockSpec((1, TN), lambda i, j: (0, j)),
            pl.BlockSpec((None, LR_PAD, GLA_KWIDTH), lambda i, j: (l, 0, 0)),
            pl.BlockSpec((None, 1, GLA_KWIDTH), lambda i, j: (l, 0, 0)),
        ],
        out_specs=[
            pl.BlockSpec((TM, TN), lambda i, j: (i, j)),
            pl.BlockSpec((TM, GLA_KWIDTH), lambda i, j: (i, 0)),
        ],
        out_shape=[
            jax.ShapeDtypeStruct((m, Z_WIDTH), F32),
            jax.ShapeDtypeStruct((m, GLA_KWIDTH), F32),
        ],
        scratch_shapes=[pltpu.VMEM((TM, D_MODEL), BF16)],
        compiler_params=pltpu.CompilerParams(
            dimension_semantics=("parallel", "arbitrary"), vmem_limit_bytes=VMEM_LIMIT),
        name="proj",
    )(x, gains, w_in_t, colscale, wa_p, ba)


def _pair_blockdiag(x128, kv):
    lane = lax.broadcasted_iota(jnp.int32, x128.shape, 1)
    own = jnp.where((lane >= kv * ATT_HEAD_DIM) & (lane < (kv + 1) * ATT_HEAD_DIM), x128, 0.0)
    other = pltpu.roll(own, ATT_HEAD_DIM, axis=1)
    lo, hi = (own, other) if kv == 0 else (other, own)
    return jnp.concatenate([lo, hi], axis=0).astype(BF16)


def _attn_block(i, sink_ref, q_ref, kc_ref, kp_ref, vc_ref, vp_ref, o_ref):
    kb = 2 * BLOCK
    row = lax.broadcasted_iota(jnp.int32, (BLOCK, kb), 0)
    col = lax.broadcasted_iota(jnp.int32, (BLOCK, kb), 1)
    diff = row + BLOCK - col
    key_pos = col + (i - 1) * BLOCK
    mask = (diff >= 0) & (diff <= WINDOW) & (key_pos >= PAD)
    lane = lax.broadcasted_iota(jnp.int32, (BLOCK, 2 * ATT_HEAD_DIM), 1)
    kk = jnp.concatenate([kp_ref[...], kc_ref[...]], axis=0)
    vv = jnp.concatenate([vp_ref[...], vc_ref[...]], axis=0)
    slabs = ATT_GROUP // 2
    for kv in range(ATT_KV_HEADS):
        k2 = _pair_blockdiag(kk, kv)
        v2 = _pair_blockdiag(vv, kv)
        c0 = kv * slabs * 2 * ATT_HEAD_DIM
        q4 = q_ref[:, c0:c0 + slabs * 2 * ATT_HEAD_DIM]
        q4 = jnp.concatenate([q4[:, p * 128:(p + 1) * 128] for p in range(slabs)], axis=0).astype(BF16)
        s = _dot_nt(q4, k2)
        probs, inv = [], []
        for p in range(slabs):
            halves, rden = [], []
            for e in range(2):
                sink = sink_ref[kv * ATT_GROUP + 2 * p + e]
                sp = jnp.where(mask, s[p * BLOCK:(p + 1) * BLOCK, e * kb:(e + 1) * kb], -jnp.inf)
                mx = jnp.maximum(jnp.max(sp, axis=-1, keepdims=True), sink)
                pe = jnp.exp(sp - mx)
                rden.append(1.0 / (jnp.sum(pe, axis=-1, keepdims=True) + jnp.exp(sink - mx)))
                halves.append(pe.astype(BF16))
            probs.append(jnp.concatenate(halves, axis=1))
            inv.append(jnp.where(lane < ATT_HEAD_DIM, rden[0], rden[1]))
        o = _dot(jnp.concatenate(probs, axis=0), v2)
        for p in range(slabs):
            o_ref[:, c0 + p * 128:c0 + (p + 1) * 128] = (o[p * BLOCK:(p + 1) * BLOCK] * inv[p]).astype(o_ref.dtype)


def _cumsum_rows(la, tri):
    hi = la.astype(BF16)
    r1 = la - hi.astype(F32)
    mid = r1.astype(BF16)
    lo = (r1 - mid.astype(F32)).astype(BF16)
    return _dot(tri, hi) + _dot(tri, mid) + _dot(tri, lo)


def _gla_chunk(c, rows, q_refs, k_refs, v_refs, la_ref, o_ref, state_ref, maybe_first):
    per = COL_TILE // GLA_DK
    C = GLA_CHUNK

    rowc = lax.broadcasted_iota(jnp.int32, (C, C), 0)
    colc = lax.broadcasted_iota(jnp.int32, (C, C), 1)
    tri = (rowc >= colc).astype(BF16)
    b_all = _cumsum_rows(la_ref[rows, :], tri) * LOG2_E

    pos = c * C + lax.broadcasted_iota(jnp.int32, (C, 1), 0)
    valid = (pos >= PAD).astype(F32)

    lane = lax.broadcasted_iota(jnp.int32, (GLA_SUB, C), 1)
    rsub = lax.broadcasted_iota(jnp.int32, (GLA_SUB, C), 0)
    scale = GLA_DK ** -0.5

    heads = range(GLA_HEADS)
    q, k, bh = [], [], []
    for h in heads:
        sub = slice((h % per) * GLA_DK, (h % per + 1) * GLA_DK)
        q.append(q_refs[h // per][rows, sub] * scale)
        k.append(k_refs[h // per][rows, sub] * valid)
        bh.append(b_all[:, h * GLA_DK:(h + 1) * GLA_DK])

    blocks = [[] for _ in heads]
    for i in range(C // GLA_SUB):
        lo_r = i * GLA_SUB
        sl = slice(lo_r, lo_r + GLA_SUB)
        w = [jnp.zeros((GLA_SUB, C), F32) for _ in heads]
        for s in range(GLA_SUB):
            r = lo_r + s
            at_r = lane == r
            for h in heads:
                zed = q[h][sl] * k[h][r:r + 1] * jnp.exp2(bh[h][sl] - bh[h][r:r + 1])
                w[h] = jnp.where(at_r, jnp.sum(zed, axis=-1, keepdims=True), w[h])
        causal = lane <= rsub + lo_r
        for h in heads:
            wh = jnp.where(causal, w[h], 0.0)
            if i > 0:
                ref_b = bh[h][lo_r - 1:lo_r]
                qi = (q[h][sl] * jnp.exp2(bh[h][sl] - ref_b)).astype(BF16)
                kj = k[h][:lo_r] * jnp.exp2(ref_b - bh[h][:lo_r])
                kj = jnp.concatenate([kj, jnp.zeros((C - lo_r, GLA_DK), F32)], axis=0).astype(BF16)
                wh = jnp.where(lane < lo_r, _dot_nt(qi, kj), wh)
            blocks[h].append(wh)

    for h in heads:
        v = v_refs[h][rows, :].astype(BF16)
        bl = bh[h][C - 1:C, :]
        st = state_ref[h]
        if maybe_first:
            st = jnp.where(c == 0, 0.0, st)
        o = _dot_nt((q[h] * jnp.exp2(bh[h])).astype(BF16), st.astype(BF16))
        a = jnp.concatenate(blocks[h], axis=0).astype(BF16)
        o_ref[rows, h * GLA_DV:(h + 1) * GLA_DV] = o + _dot(a, v)
        k_out = (k[h] * jnp.exp2(bl - bh[h])).astype(BF16)
        state_ref[h] = st * jnp.exp2(bl) + _dot_tn(v, k_out)


def _mixer_kernel(sink_ref, *refs, nb, n_blocks):
    nq = GLA_KWIDTH // COL_TILE
    attn_in, rest = refs[:5], refs[5:]
    q_refs, k_refs, v_refs = rest[:nq], rest[nq:2 * nq], rest[2 * nq:2 * nq + GLA_HEADS]
    la_ref, att_ref, go_ref, st_ref, state_ref = rest[2 * nq + GLA_HEADS:]
    step = pl.program_id(0)

    @pl.when(step < n_blocks)
    def _():
        i = lax.rem(step, nb)
        _attn_block(i, sink_ref, *attn_in, att_ref)
        per_block = BLOCK // GLA_CHUNK
        for half in range(per_block):
            _gla_chunk(i * per_block + half, pl.ds(half * GLA_CHUNK, GLA_CHUNK), q_refs, k_refs, v_refs, la_ref,
                       go_ref, state_ref, maybe_first=(half == 0))
        st_ref[...] = state_ref[...]

    @pl.when(step >= n_blocks)
    def _():
        att_ref[...] = jnp.zeros_like(att_ref)
        go_ref[...] = jnp.zeros_like(go_ref)


def _mixer_prompt(z, la, sinks, batch, nb, m_rows):
    n_blocks = batch * nb
    nq = GLA_KWIDTH // COL_TILE

    def cur(c):
        return lambda s, sink: (jnp.minimum(s, n_blocks - 1), c)

    def prev(c):
        return lambda s, sink: (jnp.maximum(jnp.minimum(s, n_blocks - 1) - 1, 0), c)

    def rows(c):
        return lambda s, sink: (s, c)

    return pl.pallas_call(
        functools.partial(_mixer_kernel, nb=nb, n_blocks=n_blocks),
        grid_spec=pltpu.PrefetchScalarGridSpec(
            num_scalar_prefetch=1,
            grid=(m_rows // BLOCK,),
            in_specs=(
                [pl.BlockSpec((BLOCK, ATT_WIDTH), cur(COL_QA // ATT_WIDTH)),
                 pl.BlockSpec((BLOCK, KV_WIDTH), cur(COL_KA // KV_WIDTH)),
                 pl.BlockSpec((BLOCK, KV_WIDTH), prev(COL_KA // KV_WIDTH)),
                 pl.BlockSpec((BLOCK, KV_WIDTH), cur(COL_VA // KV_WIDTH)),
                 pl.BlockSpec((BLOCK, KV_WIDTH), prev(COL_VA // KV_WIDTH))]
                + [pl.BlockSpec((BLOCK, COL_TILE), rows(COL_QG // COL_TILE + t)) for t in range(nq)]
                + [pl.BlockSpec((BLOCK, COL_TILE), rows(COL_KG // COL_TILE + t)) for t in range(nq)]
                + [pl.BlockSpec((BLOCK, GLA_DV), rows(COL_VG // GLA_DV + t)) for t in range(GLA_HEADS)]
                + [pl.BlockSpec((BLOCK, GLA_KWIDTH), rows(0))]
            ),
            out_specs=[
                pl.BlockSpec((BLOCK, ATT_WIDTH), rows(0)),
                pl.BlockSpec((BLOCK, GLA_WIDTH), rows(0)),
                pl.BlockSpec((None, GLA_HEADS, GLA_DV, GLA_DK),
                             lambda s, sink: (jnp.minimum(s // nb, batch - 1), 0, 0, 0)),
            ],
            scratch_shapes=[pltpu.VMEM((GLA_HEADS, GLA_DV, GLA_DK), F32)],
        ),
        out_shape=[
            jax.ShapeDtypeStruct((m_rows, ATT_WIDTH), BF16),
            jax.ShapeDtypeStruct((m_rows, GLA_WIDTH), F32),
            jax.ShapeDtypeStruct((batch, GLA_HEADS, GLA_DV, GLA_DK), F32),
        ],
        compiler_params=pltpu.CompilerParams(
            dimension_semantics=("arbitrary",), vmem_limit_bytes=VMEM_LIMIT),
        name="mixer_prompt",
    )(sinks, *([z] * (5 + 2 * nq + GLA_HEADS)), la)


def _sample_kernel(qa_ref, zs_ref, la_ref, kbuf_ref, vbuf_ref, sink_ref, s0_ref,
                   att_ref, go_ref, st_ref, *, dec_seq):
    for b in range(qa_ref.shape[0]):
        _sample_one(qa_ref.at[b], zs_ref.at[b], la_ref.at[b], kbuf_ref.at[b], vbuf_ref.at[b], sink_ref,
                    s0_ref.at[b], att_ref.at[b], go_ref.at[b], st_ref.at[b], dec_seq=dec_seq)


def _sample_one(qa_ref, zs_ref, la_ref, kbuf_ref, vbuf_ref, sink_ref, s0_ref,
                att_ref, go_ref, st_ref, *, dec_seq):
    S = dec_seq
    R = S * ATT_GROUP
    t_row = lax.broadcasted_iota(jnp.int32, (R, 1), 0) // ATT_GROUP
    c_idx = lax.broadcasted_iota(jnp.int32, (R, kbuf_ref.shape[0]), 1)
    mask_c = c_idx >= t_row
    for kv in range(ATT_KV_HEADS):
        ks = slice(kv * ATT_HEAD_DIM, (kv + 1) * ATT_HEAD_DIM)
        q = qa_ref[kv]
        sink = sink_ref[kv]
        sc = _dot_nt(q.astype(BF16), kbuf_ref[:, ks].astype(BF16))
        sc = jnp.where(mask_c, sc, -jnp.inf)
        mx = jnp.maximum(jnp.max(sc, axis=-1, keepdims=True), sink)
        kn = zs_ref[:, COL_KA + kv * ATT_HEAD_DIM:COL_KA + (kv + 1) * ATT_HEAD_DIM]
        vn = zs_ref[:, COL_VA + kv * ATT_HEAD_DIM:COL_VA + (kv + 1) * ATT_HEAD_DIM]
        sn = []
        for s in range(S):
            v = jnp.sum(q * kn[s:s + 1], axis=-1, keepdims=True)
            v = jnp.where(t_row >= s, v, -jnp.inf)
            sn.append(v)
            mx = jnp.maximum(mx, v)
        pc = jnp.exp(sc - mx)
        den = jnp.sum(pc, axis=-1, keepdims=True) + jnp.exp(sink - mx)
        o = _dot(pc.astype(BF16), vbuf_ref[:, ks].astype(BF16))
        for s in range(S):
            pn = jnp.exp(sn[s] - mx)
            den = den + pn
            o = o + pn * vn[s:s + 1]
        att_ref[kv] = o / den

    rows8 = zs_ref.shape[0]
    rid = lax.broadcasted_iota(jnp.int32, (rows8, 1), 0)
    la = la_ref[...]
    run = la[0:1]
    b_all = jnp.where(rid == 0, run, 0.0)
    for t in range(1, S):
        run = run + la[t:t + 1]
        b_all = jnp.where(rid == t, run, b_all)
    real = (rid < S).astype(F32)
    scale = GLA_DK ** -0.5
    for h in range(GLA_HEADS):
        ks = slice(h * GLA_DK, (h + 1) * GLA_DK)
        q = zs_ref[:, COL_QG + h * GLA_DK:COL_QG + (h + 1) * GLA_DK] * scale
        k = zs_ref[:, COL_KG + h * GLA_DK:COL_KG + (h + 1) * GLA_DK]
        v = zs_ref[:, COL_VG + h * GLA_DV:COL_VG + (h + 1) * GLA_DV]
        bh = b_all[:, ks] * real
        bl = bh[S - 1:S]
        s0 = s0_ref[h]
        k_out = k * jnp.exp(jnp.minimum(bl - bh, 0.0)) * real
        q_in = q * jnp.exp(bh) * real
        o = _dot(q_in.astype(BF16), s0.astype(BF16))
        for s in range(S):
            zed = q * k[s:s + 1] * jnp.exp(jnp.minimum(bh - bh[s:s + 1], 0.0))
            w = jnp.sum(zed, axis=-1, keepdims=True)
            w = jnp.where((rid >= s) & (rid < S), w, 0.0)
            o = o + w * v[s:s + 1]
        go_ref[:, h * GLA_DV:(h + 1) * GLA_DV] = o
        e_rows = jnp.broadcast_to(jnp.exp(bl), (GLA_DK, GLA_DK)).T
        decay = jnp.concatenate([e_rows] * (GLA_DV // GLA_DK), axis=1)
        st_ref[h] = decay * s0 + _dot_tn(k_out.astype(BF16), v.astype(BF16))


def _sample_mixer(qa_s, zs8, la8, kbuf, vbuf, sink_rows, s0, dec_seq, l):
    nb = zs8.shape[0]
    rows8 = zs8.shape[1]
    R = dec_seq * ATT_GROUP
    wb = kbuf.shape[2]
    return pl.pallas_call(
        functools.partial(_sample_kernel, dec_seq=dec_seq),
        grid=(nb // SAMPLE_TILE,),
        in_specs=[
            pl.BlockSpec((SAMPLE_TILE, ATT_KV_HEADS, R, ATT_HEAD_DIM), lambda b: (b, 0, 0, 0)),
            pl.BlockSpec((SAMPLE_TILE, rows8, Z_WIDTH), lambda b: (b, 0, 0)),
            pl.BlockSpec((SAMPLE_TILE, rows8, GLA_KWIDTH), lambda b: (b, 0, 0)),
            pl.BlockSpec((None, SAMPLE_TILE, wb, KV_WIDTH), lambda b: (l, b, 0, 0)),
            pl.BlockSpec((None, SAMPLE_TILE, wb, KV_WIDTH), lambda b: (l, b, 0, 0)),
            pl.BlockSpec((ATT_KV_HEADS, R, 1), lambda b: (0, 0, 0)),
            pl.BlockSpec((None, SAMPLE_TILE, GLA_HEADS, GLA_DK, GLA_DV), lambda b: (l, b, 0, 0, 0)),
        ],
        out_specs=[
            pl.BlockSpec((SAMPLE_TILE, ATT_KV_HEADS, R, ATT_HEAD_DIM), lambda b: (b, 0, 0, 0)),
            pl.BlockSpec((SAMPLE_TILE, rows8, GLA_WIDTH), lambda b: (b, 0, 0)),
            pl.BlockSpec((SAMPLE_TILE, GLA_HEADS, GLA_DK, GLA_DV), lambda b: (b, 0, 0, 0)),
        ],
        out_shape=[
            jax.ShapeDtypeStruct((nb, ATT_KV_HEADS, R, ATT_HEAD_DIM), F32),
            jax.ShapeDtypeStruct((nb, rows8, GLA_WIDTH), F32),
            jax.ShapeDtypeStruct((nb, GLA_HEADS, GLA_DK, GLA_DV), F32),
        ],
        compiler_params=pltpu.CompilerParams(
            dimension_semantics=("parallel",), vmem_limit_bytes=VMEM_LIMIT),
        name="sample_mixer",
    )(qa_s, zs8, la8, kbuf, vbuf, sink_rows, s0)


def _merge_kernel(*refs):
    x_ref, att_ref, go_ref = refs[:3]
    rg_refs = refs[3:3 + GLA_HEADS]
    ggla_ref, wo_ref, gpost_ref, o_ref = refs[3 + GLA_HEADS:]
    parts = []
    for h in range(GLA_HEADS):
        vs = slice(h * GLA_DV, (h + 1) * GLA_DV)
        rg = rg_refs[h][...]
        parts.append((_rms(go_ref[:, vs], ggla_ref[...]) * (rg * jax.nn.sigmoid(rg))).astype(BF16))
    mixed = jnp.concatenate([att_ref[...]] + parts, axis=-1)
    o_ref[...] = x_ref[...] + _rms(_dot(mixed, wo_ref[...]), gpost_ref[...])


def _merge(x, att, go, z, g_gla, w_out, gains, l):
    m = x.shape[0]
    return pl.pallas_call(
        _merge_kernel,
        grid=(m // TM_MERGE,),
        in_specs=(
            [pl.BlockSpec((TM_MERGE, D_MODEL), lambda i: (i, 0)),
             pl.BlockSpec((TM_MERGE, ATT_WIDTH), lambda i: (i, 0)),
             pl.BlockSpec((TM_MERGE, GLA_WIDTH), lambda i: (i, 0))]
            + [pl.BlockSpec((TM_MERGE, GLA_DV), lambda i, h=h: (i, COL_RG // GLA_DV + h)) for h in range(GLA_HEADS)]
            + [pl.BlockSpec((None, 1, GLA_DV), lambda i: (l, 0, 0)),
               pl.BlockSpec((None, ATT_WIDTH + GLA_WIDTH, D_MODEL), lambda i: (l, 0, 0)),
               _gain_spec(l, 3)]
        ),
        out_specs=pl.BlockSpec((TM_MERGE, D_MODEL), lambda i: (i, 0)),
        out_shape=jax.ShapeDtypeStruct((m, D_MODEL), F32),
        compiler_params=pltpu.CompilerParams(
            dimension_semantics=("parallel",), vmem_limit_bytes=VMEM_LIMIT),
        name="merge",
    )(x, att, go, *([z] * GLA_HEADS), g_gla, w_out, gains)


def _copy_kernel(*refs):
    y_ref = refs[-1]
    for t, x_ref in enumerate(refs[:-1]):
        y_ref[t * BLOCK:(t + 1) * BLOCK, :] = x_ref[...]


def _extract_prompt(x, batch, seq, lp):
    skip = (lp - seq) // BLOCK
    return pl.pallas_call(
        _copy_kernel,
        grid=(batch, seq // (COPY_BLOCKS * BLOCK)),
        in_specs=[pl.BlockSpec((BLOCK, x.shape[1]),
                               lambda b, k, t=t: (b * (lp // BLOCK) + skip + COPY_BLOCKS * k + t, 0))
                  for t in range(COPY_BLOCKS)],
        out_specs=pl.BlockSpec((None, COPY_BLOCKS * BLOCK, x.shape[1]), lambda b, k: (b, k, 0)),
        out_shape=jax.ShapeDtypeStruct((batch, seq, x.shape[1]), x.dtype),
        compiler_params=pltpu.CompilerParams(dimension_semantics=("parallel", "parallel")),
        name="extract_prompt",
    )(*([x] * COPY_BLOCKS))


def kernel(x_prompt, x_sample, cache_k_win, cache_v_win, state_gla, meta_tokens, norm_gains,
           w_ffn_gate, w_ffn_up, w_ffn_down, w_in, w_gate_up, b_gate, attn_sinks, gla_norm, w_out):
    batch, seq, _ = x_prompt.shape
    dec_batch, dec_seq, _ = x_sample.shape
    depth = norm_gains.shape[0]
    lp = seq + BLOCK
    nb = lp // BLOCK
    mp = batch * lp
    ms = dec_batch * dec_seq
    m_pad = -(-(mp + ms) // TM) * TM
    wb = cache_k_win.shape[2]
    rows8 = 8
    assert dec_seq <= rows8 and lp % BLOCK == 0 and dec_batch % SAMPLE_TILE == 0

    head = jnp.concatenate([jnp.zeros((PAD, D_MODEL), F32), meta_tokens.astype(F32)], axis=0)
    pieces = []
    for b in range(batch):
        pieces += [head, x_prompt[b]]
    pieces += [x_sample.reshape(ms, D_MODEL), jnp.zeros((m_pad - mp - ms, D_MODEL), F32)]
    x = jnp.concatenate(pieces, axis=0)

    wg, wu, wd = w_ffn_gate, w_ffn_up, w_ffn_down
    wo = w_out.astype(BF16)
    w_in_t = jnp.swapaxes(w_in, 1, 2)
    col = jnp.arange(Z_WIDTH)
    colscale = jnp.where((col >= COL_QA) & (col < COL_QA + ATT_WIDTH), ATT_HEAD_DIM ** -0.5, 1.0).astype(F32)[None]
    wa_p = jnp.pad(w_gate_up, ((0, 0), (0, LR_PAD - GLA_RANK), (0, 0))).astype(BF16)
    ba = b_gate.reshape(depth, 1, GLA_KWIDTH)
    g_gla = gla_norm.reshape(depth, 1, GLA_DV)
    gains = norm_gains.reshape(depth, 6, 1, D_MODEL)
    kbuf = cache_k_win.reshape(depth, dec_batch, wb, KV_WIDTH)
    vbuf = cache_v_win.reshape(depth, dec_batch, wb, KV_WIDTH)
    tail = jnp.zeros((m_pad - mp - ms, GLA_WIDTH), F32)

    pk, pv, ps, sk, sv, ss = [], [], [], [], [], []
    for l in range(depth):
        x = _ffn(x, gains, wg, wu, wd, l, 0)

        z, la = _proj(x, gains, w_in_t, colscale, wa_p, ba, l)
        att, go, st_p = _mixer_prompt(z, la, attn_sinks[l], batch, nb, m_pad)

        zs = z[mp:mp + ms].reshape(dec_batch, dec_seq, Z_WIDTH)
        zs8 = jnp.pad(zs, ((0, 0), (0, rows8 - dec_seq), (0, 0)))
        la8 = jnp.pad(la[mp:mp + ms].reshape(dec_batch, dec_seq, GLA_KWIDTH), ((0, 0), (0, rows8 - dec_seq), (0, 0)))
        qa_s = zs[:, :, COL_QA:COL_QA + ATT_WIDTH].reshape(dec_batch, dec_seq, ATT_KV_HEADS, ATT_GROUP, ATT_HEAD_DIM)
        qa_s = qa_s.transpose(0, 2, 1, 3, 4).reshape(dec_batch, ATT_KV_HEADS, dec_seq * ATT_GROUP, ATT_HEAD_DIM)
        sink_rows = jnp.tile(attn_sinks[l].reshape(ATT_KV_HEADS, 1, ATT_GROUP), (1, dec_seq, 1))
        sink_rows = sink_rows.reshape(ATT_KV_HEADS, dec_seq * ATT_GROUP, 1)
        att_s, go_s, st_s = _sample_mixer(qa_s, zs8, la8, kbuf, vbuf, sink_rows, state_gla, dec_seq, l)
        att_s = att_s.reshape(dec_batch, ATT_KV_HEADS, dec_seq, ATT_GROUP, ATT_HEAD_DIM)
        att_s = att_s.transpose(0, 2, 1, 3, 4).reshape(ms, ATT_WIDTH)

        att = lax.dynamic_update_slice(att, jnp.concatenate([att_s, tail], axis=0).astype(BF16), (mp, 0))
        go = lax.dynamic_update_slice(go, jnp.concatenate([go_s[:, :dec_seq].reshape(ms, GLA_WIDTH), tail], axis=0), (mp, 0))
        x = _merge(x, att, go, z, g_gla, wo, gains, l)

        win = [z[b * lp + lp - WINDOW:(b + 1) * lp, COL_KA:COL_VA + KV_WIDTH] for b in range(batch)]
        win = jnp.stack(win).reshape(batch, WINDOW, 2, ATT_KV_HEADS, ATT_HEAD_DIM)
        pk.append(win[:, :, 0])
        pv.append(win[:, :, 1])
        ps.append(st_p.transpose(0, 1, 3, 2))
        kn = zs[:, :, COL_KA:COL_KA + KV_WIDTH].reshape(dec_batch, dec_seq, ATT_KV_HEADS, ATT_HEAD_DIM)
        vn = zs[:, :, COL_VA:COL_VA + KV_WIDTH].reshape(dec_batch, dec_seq, ATT_KV_HEADS, ATT_HEAD_DIM)
        sk.append(jnp.concatenate([cache_k_win[l], kn], axis=1)[:, -wb:])
        sv.append(jnp.concatenate([cache_v_win[l], vn], axis=1)[:, -wb:])
        ss.append(st_s)

        x = _ffn(x, gains, wg, wu, wd, l, 1)

    y_prompt = _extract_prompt(x, batch, seq, lp)
    y_sample = x[mp:mp + ms].reshape(dec_batch, dec_seq, D_MODEL)
    return (y_prompt, y_sample, jnp.stack(pk), jnp.stack(pv), jnp.stack(ps),
            jnp.stack(sk), jnp.stack(sv), jnp.stack(ss))
```

```python
import functools

import jax
import jax.numpy as jnp
from jax import lax
from jax.experimental import pallas as pl
from jax.experimental.pallas import tpu as pltpu

F32 = jnp.float32
BF16 = jnp.bfloat16

D_MODEL = 2048
D_FF = 5632
N_META = 16
BLOCK = 128
WINDOW = 128
ATT_HEADS = 16
ATT_KV_HEADS = 2
ATT_GROUP = 8
ATT_HEAD_DIM = 64
ATT_WIDTH = 1024
KV_WIDTH = 128
GLA_HEADS = 4
GLA_DK = 128
GLA_DV = 256
GLA_KWIDTH = 512
GLA_WIDTH = 1024
GLA_RANK = 16
GLA_GATE_NORM = 16.0
GLA_CHUNK = 64
GLA_SUB = 8
EPS = 1e-6
LOG2_E = 1.4426950408889634
PAD = BLOCK - N_META

COL_QA, COL_KA, COL_VA, COL_QG, COL_KG, COL_VG, COL_RG, COL_LR = 0, 1024, 1152, 1280, 1792, 2304, 3328, 4352
COL_TILE = 256
Z_WIDTH = 4608
LR_PAD = 128

VMEM_LIMIT = 56 * 1024 * 1024
VMEM_LIMIT_FFN = 60 * 1024 * 1024

TM = 1072
TM_MERGE = 536
TF = 256
TN = 768
X_PARTS = 4
SAMPLE_TILE = 4
COPY_BLOCKS = 4


def _rms(x, gain):
    ms = jnp.mean(x * x, axis=-1, keepdims=True)
    return x * lax.rsqrt(ms + EPS) * gain


def _dot(a, b):
    return jnp.dot(a, b, preferred_element_type=F32)


def _dot_nt(a, b):
    return lax.dot_general(a, b, (((1,), (1,)), ((), ())), preferred_element_type=F32)


def _dot_tn(a, b):
    return lax.dot_general(a, b, (((0,), (0,)), ((), ())), preferred_element_type=F32)


def _row_parts(n_rows, parts):
    units = n_rows // 8
    out, start = [], 0
    for r in range(parts):
        size = (units // parts + (1 if r < units % parts else 0)) * 8
        out.append((start, size))
        start += size
    assert start == n_rows
    return out


def _ffn_kernel(x_hbm, gpre_ref, gpost_ref, wg_hbm, wu_hbm, wd_hbm, o_ref, xn_ref, x_buf, wg_buf, wu_buf, wd_buf,
                sem, xsem, *, l, f):
    i = pl.program_id(0)
    n_tiles = pl.num_programs(0)
    n_chunks = D_FF // TF
    parts = _row_parts(TM, X_PARTS)

    def w_copies(j, slot):
        cols = pl.ds(j * TF, TF)
        return (pltpu.make_async_copy(wg_hbm.at[l, f, :, cols], wg_buf.at[slot], sem.at[0, slot]),
                pltpu.make_async_copy(wu_hbm.at[l, f, :, cols], wu_buf.at[slot], sem.at[1, slot]),
                pltpu.make_async_copy(wd_hbm.at[l, f, cols, :], wd_buf.at[slot], sem.at[2, slot]))

    def x_copy(tile, r):
        start, size = parts[r]
        return pltpu.make_async_copy(x_hbm.at[pl.ds(tile * TM + start, size), :],
                                     x_buf.at[pl.ds(start, size), :], xsem.at[r])

    @pl.when(i == 0)
    def _():
        for r in range(X_PARTS):
            x_copy(0, r).start()
        for c in w_copies(0, 0):
            c.start()

    for r, (start, size) in enumerate(parts):
        x_copy(i, r).wait()
        rows = pl.ds(start, size)
        xn_ref[rows, :] = _rms(x_buf[rows, :], gpre_ref[...]).astype(BF16)
    o_ref[...] = jnp.zeros_like(o_ref)

    def pair(p, carry):
        for slot in range(2):
            j = 2 * p + slot
            for c in w_copies(j, slot):
                c.wait()
            for c in w_copies(lax.rem(j + 1, n_chunks), 1 - slot):
                c.start()
            xn = xn_ref[...]
            g = _dot(xn, wg_buf[slot].astype(BF16))
            u = _dot(xn, wu_buf[slot].astype(BF16))
            h = (g * jax.nn.sigmoid(g) * u).astype(BF16)
            o_ref[...] += _dot(h, wd_buf[slot].astype(BF16))
        return carry

    lax.fori_loop(0, n_chunks // 2, pair, 0)

    for r, (start, size) in enumerate(parts):
        rows = pl.ds(start, size)
        o_ref[rows, :] = x_buf[rows, :] + 0.5 * _rms(o_ref[rows, :], gpost_ref[...])

        @pl.when(i + 1 < n_tiles)
        def _():
            x_copy(i + 1, r).start()

    @pl.when(i == n_tiles - 1)
    def _():
        for c in w_copies(0, 0):
            c.wait()


def _gain_spec(l, idx):
    return pl.BlockSpec((None, None, 1, D_MODEL), lambda *_: (l, idx, 0, 0))


def _ffn(x, gains, wg, wu, wd, l, f):
    m = x.shape[0]
    return pl.pallas_call(
        functools.partial(_ffn_kernel, l=l, f=f),
        grid=(m // TM,),
        in_specs=[
            pl.BlockSpec(memory_space=pl.ANY),
            _gain_spec(l, 4 * f),
            _gain_spec(l, 4 * f + 1),
            pl.BlockSpec(memory_space=pl.ANY),
            pl.BlockSpec(memory_space=pl.ANY),
            pl.BlockSpec(memory_space=pl.ANY),
        ],
        out_specs=pl.BlockSpec((TM, D_MODEL), lambda i: (i, 0)),
        out_shape=jax.ShapeDtypeStruct((m, D_MODEL), F32),
        scratch_shapes=[
            pltpu.VMEM((TM, D_MODEL), BF16),
            pltpu.VMEM((TM, D_MODEL), F32),
            pltpu.VMEM((2, D_MODEL, TF), F32),
            pltpu.VMEM((2, D_MODEL, TF), F32),
            pltpu.VMEM((2, TF, D_MODEL), F32),
            pltpu.SemaphoreType.DMA((3, 2)),
            pltpu.SemaphoreType.DMA((X_PARTS,)),
        ],
        compiler_params=pltpu.CompilerParams(
            dimension_semantics=("arbitrary",), vmem_limit_bytes=VMEM_LIMIT_FFN),
        name="ffn",
    )(x, gains, gains, wg, wu, wd)


def _proj_kernel(x_ref, g_ref, wt_ref, cs_ref, wa_ref, ba_ref, z_ref, la_ref, xn_ref, *, n_cols):
    j = pl.program_id(1)
    last = pl.num_programs(1) - 1
    n_valid = n_cols - (Z_WIDTH - TN)

    @pl.when(j == 0)
    def _():
        xn_ref[...] = _rms(x_ref[...], g_ref[...]).astype(BF16)

    @pl.when(j < last)
    def _():
        z_ref[...] = _dot_nt(xn_ref[...], wt_ref[...].astype(BF16)) * cs_ref[...]

    @pl.when(j == last)
    def _():
        z_ref[:, :n_valid] = _dot_nt(xn_ref[...], wt_ref[:n_valid, :].astype(BF16)) * cs_ref[:, :n_valid]
        z_ref[:, n_valid:] = jnp.zeros((z_ref.shape[0], TN - n_valid), F32)
        lr_off = COL_LR - (Z_WIDTH - TN)
        lr = z_ref[:, lr_off:lr_off + LR_PAD].astype(BF16)
        logit = _dot(lr, wa_ref[...]) + ba_ref[...]
        log_sig = jnp.minimum(logit, 0.0) - jnp.log1p(jnp.exp(-jnp.abs(logit)))
        la_ref[...] = log_sig * (1.0 / GLA_GATE_NORM)


def _proj(x, gains, w_in_t, colscale, wa_p, ba, l):
    m = x.shape[0]
    n_cols = w_in_t.shape[1]
    assert Z_WIDTH - TN < COL_LR and COL_LR + GLA_RANK <= n_cols <= Z_WIDTH
    return pl.pallas_call(
        functools.partial(_proj_kernel, n_cols=n_cols),
        grid=(m // TM, Z_WIDTH // TN),
        in_specs=[
            pl.BlockSpec((TM, D_MODEL), lambda i, j: (i, 0)),
            _gain_spec(l, 2),
            pl.BlockSpec((None, TN, D_MODEL), lambda i, j: (l, j, 0)),
            pl.BlockSpec((1, TN), lambda i, j: (0, j)),
            pl.BlockSpec((None, LR_PAD, GLA_KWIDTH), lambda i, j: (l, 0, 0)),
            pl.BlockSpec((None, 1, GLA_KWIDTH), lambda i, j: (l, 0, 0)),
        ],
        out_specs=[
            pl.BlockSpec((TM, TN), lambda i, j: (i, j)),
            pl.BlockSpec((TM, GLA_KWIDTH), lambda i, j: (i, 0)),
        ],
        out_shape=[
            jax.ShapeDtypeStruct((m, Z_WIDTH), F32),
            jax.ShapeDtypeStruct((m, GLA_KWIDTH), F32),
        ],
        scratch_shapes=[pltpu.VMEM((TM, D_MODEL), BF16)],
        compiler_params=pltpu.CompilerParams(
            dimension_semantics=("parallel", "arbitrary"), vmem_limit_bytes=VMEM_LIMIT),
        name="proj",
    )(x, gains, w_in_t, colscale, wa_p, ba)


def _pair_blockdiag(x128, kv):
    lane = lax.broadcasted_iota(jnp.int32, x128.shape, 1)
    own = jnp.where((lane >= kv * ATT_HEAD_DIM) & (lane < (kv + 1) * ATT_HEAD_DIM), x128, 0.0)
    other = pltpu.roll(own, ATT_HEAD_DIM, axis=1)
    lo, hi = (own, other) if kv == 0 else (other, own)
    return jnp.concatenate([lo, hi], axis=0).astype(BF16)


def _attn_block(i, sink_ref, q_ref, kc_ref, kp_ref, vc_ref, vp_ref, o_ref):
    kb = 2 * BLOCK
    row = lax.broadcasted_iota(jnp.int32, (BLOCK, kb), 0)
    col = lax.broadcasted_iota(jnp.int32, (BLOCK, kb), 1)
    diff = row + BLOCK - col
    key_pos = col + (i - 1) * BLOCK
    mask = (diff >= 0) & (diff <= WINDOW) & (key_pos >= PAD)
    lane = lax.broadcasted_iota(jnp.int32, (BLOCK, 2 * ATT_HEAD_DIM), 1)
    kk = jnp.concatenate([kp_ref[...], kc_ref[...]], axis=0)
    vv = jnp.concatenate([vp_ref[...], vc_ref[...]], axis=0)
    slabs = ATT_GROUP // 2
    for kv in range(ATT_KV_HEADS):
        k2 = _pair_blockdiag(kk, kv)
        v2 = _pair_blockdiag(vv, kv)
        c0 = kv * slabs * 2 * ATT_HEAD_DIM
        q4 = q_ref[:, c0:c0 + slabs * 2 * ATT_HEAD_DIM]
        q4 = jnp.concatenate([q4[:, p * 128:(p + 1) * 128] for p in range(slabs)], axis=0).astype(BF16)
        s = _dot_nt(q4, k2)
        probs, inv = [], []
        for p in range(slabs):
            halves, rden = [], []
            for e in range(2):
                sink = sink_ref[kv * ATT_GROUP + 2 * p + e]
                sp = jnp.where(mask, s[p * BLOCK:(p + 1) * BLOCK, e * kb:(e + 1) * kb], -jnp.inf)
                mx = jnp.maximum(jnp.max(sp, axis=-1, keepdims=True), sink)
                pe = jnp.exp(sp - mx)
                rden.append(1.0 / (jnp.sum(pe, axis=-1, keepdims=True) + jnp.exp(sink - mx)))
                halves.append(pe.astype(BF16))
            probs.append(jnp.concatenate(halves, axis=1))
            inv.append(jnp.where(lane < ATT_HEAD_DIM, rden[0], rden[1]))
        o = _dot(jnp.concatenate(probs, axis=0), v2)
        for p in range(slabs):
            o_ref[:, c0 + p * 128:c0 + (p + 1) * 128] = (o[p * BLOCK:(p + 1) * BLOCK] * inv[p]).astype(o_ref.dtype)


def _cumsum_rows(la, tri):
    hi = la.astype(BF16)
    r1 = la - hi.astype(F32)
    mid = r1.astype(BF16)
    lo = (r1 - mid.astype(F32)).astype(BF16)
    return _dot(tri, hi) + _dot(tri, mid) + _dot(tri, lo)


def _gla_chunk(c, rows, q_refs, k_refs, v_refs, la_ref, o_ref, state_ref, maybe_first):
    per = COL_TILE // GLA_DK
    C = GLA_CHUNK

    rowc = lax.broadcasted_iota(jnp.int32, (C, C), 0)
    colc = lax.broadcasted_iota(jnp.int32, (C, C), 1)
    tri = (rowc >= colc).astype(BF16)
    b_all = _cumsum_rows(la_ref[rows, :], tri) * LOG2_E

    pos = c * C + lax.broadcasted_iota(jnp.int32, (C, 1), 0)
    valid = (pos >= PAD).astype(F32)

    lane = lax.broadcasted_iota(jnp.int32, (GLA_SUB, C), 1)
    rsub = lax.broadcasted_iota(jnp.int32, (GLA_SUB, C), 0)
    scale = GLA_DK ** -0.5

    heads = range(GLA_HEADS)
    q, k, bh = [], [], []
    for h in heads:
        sub = slice((h % per) * GLA_DK, (h % per + 1) * GLA_DK)
        q.append(q_refs[h // per][rows, sub] * scale)
        k.append(k_refs[h // per][rows, sub] * valid)
        bh.append(b_all[:, h * GLA_DK:(h + 1) * GLA_DK])

    blocks = [[] for _ in heads]
    for i in range(C // GLA_SUB):
        lo_r = i * GLA_SUB
        sl = slice(lo_r, lo_r + GLA_SUB)
        w = [jnp.zeros((GLA_SUB, C), F32) for _ in heads]
        for s in range(GLA_SUB):
            r = lo_r + s
            at_r = lane == r
            for h in heads:
                zed = q[h][sl] * k[h][r:r + 1] * jnp.exp2(bh[h][sl] - bh[h][r:r + 1])
                w[h] = jnp.where(at_r, jnp.sum(zed, axis=-1, keepdims=True), w[h])
        causal = lane <= rsub + lo_r
        for h in heads:
            wh = jnp.where(causal, w[h], 0.0)
            if i > 0:
                ref_b = bh[h][lo_r - 1:lo_r]
                qi = (q[h][sl] * jnp.exp2(bh[h][sl] - ref_b)).astype(BF16)
                kj = k[h][:lo_r] * jnp.exp2(ref_b - bh[h][:lo_r])
                kj = jnp.concatenate([kj, jnp.zeros((C - lo_r, GLA_DK), F32)], axis=0).astype(BF16)
                wh = jnp.where(lane < lo_r, _dot_nt(qi, kj), wh)
            blocks[h].append(wh)

    for h in heads:
        v = v_refs[h][rows, :].astype(BF16)
        bl = bh[h][C - 1:C, :]
        st = state_ref[h]
        if maybe_first:
            st = jnp.where(c == 0, 0.0, st)
        o = _dot_nt((q[h] * jnp.exp2(bh[h])).astype(BF16), st.astype(BF16))
        a = jnp.concatenate(blocks[h], axis=0).astype(BF16)
        o_ref[rows, h * GLA_DV:(h + 1) * GLA_DV] = o + _dot(a, v)
        k_out = (k[h] * jnp.exp2(bl - bh[h])).astype(BF16)
        state_ref[h] = st * jnp.exp2(bl) + _dot_tn(v, k_out)


def _mixer_kernel(sink_ref, *refs, nb, n_blocks):
    nq = GLA_KWIDTH // COL_TILE
    attn_in, rest = refs[:5], refs[5:]
    q_refs, k_refs, v_refs = rest[:nq], rest[nq:2 * nq], rest[2 * nq:2 * nq + GLA_HEADS]
    la_ref, att_ref, go_ref, st_ref, state_ref = rest[2 * nq + GLA_HEADS:]
    step = pl.program_id(0)

    @pl.when(step < n_blocks)
    def _():
        i = lax.rem(step, nb)
        _attn_block(i, sink_ref, *attn_in, att_ref)
        per_block = BLOCK // GLA_CHUNK
        for half in range(per_block):
            _gla_chunk(i * per_block + half, pl.ds(half * GLA_CHUNK, GLA_CHUNK), q_refs, k_refs, v_refs, la_ref,
                       go_ref, state_ref, maybe_first=(half == 0))
        st_ref[...] = state_ref[...]

    @pl.when(step >= n_blocks)
    def _():
        att_ref[...] = jnp.zeros_like(att_ref)
        go_ref[...] = jnp.zeros_like(go_ref)


def _mixer_prompt(z, la, sinks, batch, nb, m_rows):
    n_blocks = batch * nb
    nq = GLA_KWIDTH // COL_TILE

    def cur(c):
        return lambda s, sink: (jnp.minimum(s, n_blocks - 1), c)

    def prev(c):
        return lambda s, sink: (jnp.maximum(jnp.minimum(s, n_blocks - 1) - 1, 0), c)

    def rows(c):
        return lambda s, sink: (s, c)

    return pl.pallas_call(
        functools.partial(_mixer_kernel, nb=nb, n_blocks=n_blocks),
        grid_spec=pltpu.PrefetchScalarGridSpec(
            num_scalar_prefetch=1,
            grid=(m_rows // BLOCK,),
            in_specs=(
                [pl.BlockSpec((BLOCK, ATT_WIDTH), cur(COL_QA // ATT_WIDTH)),
                 pl.BlockSpec((BLOCK, KV_WIDTH), cur(COL_KA // KV_WIDTH)),
                 pl.BlockSpec((BLOCK, KV_WIDTH), prev(COL_KA // KV_WIDTH)),
                 pl.BlockSpec((BLOCK, KV_WIDTH), cur(COL_VA // KV_WIDTH)),
                 pl.BlockSpec((BLOCK, KV_WIDTH), prev(COL_VA // KV_WIDTH))]
                + [pl.BlockSpec((BLOCK, COL_TILE), rows(COL_QG // COL_TILE + t)) for t in range(nq)]
                + [pl.BlockSpec((BLOCK, COL_TILE), rows(COL_KG // COL_TILE + t)) for t in range(nq)]
                + [pl.BlockSpec((BLOCK, GLA_DV), rows(COL_VG // GLA_DV + t)) for t in range(GLA_HEADS)]
                + [pl.BlockSpec((BLOCK, GLA_KWIDTH), rows(0))]
            ),
            out_specs=[
                pl.BlockSpec((BLOCK, ATT_WIDTH), rows(0)),
                pl.BlockSpec((BLOCK, GLA_WIDTH), rows(0)),
                pl.BlockSpec((None, GLA_HEADS, GLA_DV, GLA_DK),
                             lambda s, sink: (jnp.minimum(s // nb, batch - 1), 0, 0, 0)),
            ],
            scratch_shapes=[pltpu.VMEM((GLA_HEADS, GLA_DV, GLA_DK), F32)],
        ),
        out_shape=[
            jax.ShapeDtypeStruct((m_rows, ATT_WIDTH), BF16),
            jax.ShapeDtypeStruct((m_rows, GLA_WIDTH), F32),
            jax.ShapeDtypeStruct((batch, GLA_HEADS, GLA_DV, GLA_DK), F32),
        ],
        compiler_params=pltpu.CompilerParams(
            dimension_semantics=("arbitrary",), vmem_limit_bytes=VMEM_LIMIT),
        name="mixer_prompt",
    )(sinks, *([z] * (5 + 2 * nq + GLA_HEADS)), la)


def _sample_kernel(qa_ref, zs_ref, la_ref, kbuf_ref, vbuf_ref, sink_ref, s0_ref,
                   att_ref, go_ref, st_ref, *, dec_seq):
    tile = range(qa_ref.shape[0])
    for kv in range(ATT_KV_HEADS):
        for b in tile:
            att_ref[b, kv] = _sample_attn(qa_ref.at[b], zs_ref.at[b], kbuf_ref.at[b], vbuf_ref.at[b], sink_ref,
                                          kv, dec_seq)
    decay = [_sample_decay(la_ref.at[b], dec_seq) for b in tile]
    for h in range(GLA_HEADS):
        for b in tile:
            o, st = _sample_gla_head(zs_ref.at[b], s0_ref.at[b], decay[b], h, dec_seq)
            go_ref[b, :, h * GLA_DV:(h + 1) * GLA_DV] = o
            st_ref[b, h] = st


def _sample_attn(qa_ref, zs_ref, kbuf_ref, vbuf_ref, sink_ref, kv, S):
    R = S * ATT_GROUP
    t_row = lax.broadcasted_iota(jnp.int32, (R, 1), 0) // ATT_GROUP
    c_idx = lax.broadcasted_iota(jnp.int32, (R, kbuf_ref.shape[0]), 1)
    mask_c = c_idx >= t_row
    ks = slice(kv * ATT_HEAD_DIM, (kv + 1) * ATT_HEAD_DIM)
    q = qa_ref[kv]
    sink = sink_ref[kv]
    sc = _dot_nt(q.astype(BF16), kbuf_ref[:, ks].astype(BF16))
    sc = jnp.where(mask_c, sc, -jnp.inf)
    mx = jnp.maximum(jnp.max(sc, axis=-1, keepdims=True), sink)
    kn = zs_ref[:, COL_KA + kv * ATT_HEAD_DIM:COL_KA + (kv + 1) * ATT_HEAD_DIM]
    vn = zs_ref[:, COL_VA + kv * ATT_HEAD_DIM:COL_VA + (kv + 1) * ATT_HEAD_DIM]
    sn = []
    for s in range(S):
        v = jnp.sum(q * kn[s:s + 1], axis=-1, keepdims=True)
        v = jnp.where(t_row >= s, v, -jnp.inf)
        sn.append(v)
        mx = jnp.maximum(mx, v)
    pc = jnp.exp(sc - mx)
    den = jnp.sum(pc, axis=-1, keepdims=True) + jnp.exp(sink - mx)
    o = _dot(pc.astype(BF16), vbuf_ref[:, ks].astype(BF16))
    for s in range(S):
        pn = jnp.exp(sn[s] - mx)
        den = den + pn
        o = o + pn * vn[s:s + 1]
    return o / den


def _sample_decay(la_ref, S):
    rows8 = la_ref.shape[0]
    rid = lax.broadcasted_iota(jnp.int32, (rows8, 1), 0)
    la = la_ref[...]
    run = la[0:1]
    b_all = jnp.where(rid == 0, run, 0.0)
    for t in range(1, S):
        run = run + la[t:t + 1]
        b_all = jnp.where(rid == t, run, b_all)
    return b_all, rid, (rid < S).astype(F32)


def _sample_gla_head(zs_ref, s0_ref, decay, h, S):
    b_all, rid, real = decay
    scale = GLA_DK ** -0.5
    q = zs_ref[:, COL_QG + h * GLA_DK:COL_QG + (h + 1) * GLA_DK] * scale
    k = zs_ref[:, COL_KG + h * GLA_DK:COL_KG + (h + 1) * GLA_DK]
    v = zs_ref[:, COL_VG + h * GLA_DV:COL_VG + (h + 1) * GLA_DV]
    bh = b_all[:, h * GLA_DK:(h + 1) * GLA_DK] * real
    bl = bh[S - 1:S]
    s0 = s0_ref[h]
    k_out = k * jnp.exp(jnp.minimum(bl - bh, 0.0)) * real
    q_in = q * jnp.exp(bh) * real
    o = _dot(q_in.astype(BF16), s0.astype(BF16))
    for s in range(S):
        zed = q * k[s:s + 1] * jnp.exp(jnp.minimum(bh - bh[s:s + 1], 0.0))
        w = jnp.sum(zed, axis=-1, keepdims=True)
        w = jnp.where((rid >= s) & (rid < S), w, 0.0)
        o = o + w * v[s:s + 1]
    e_rows = jnp.broadcast_to(jnp.exp(bl), (GLA_DK, GLA_DK)).T
    decay_rows = jnp.concatenate([e_rows] * (GLA_DV // GLA_DK), axis=1)
    return o, decay_rows * s0 + _dot_tn(k_out.astype(BF16), v.astype(BF16))


def _sample_mixer(qa_s, zs8, la8, kbuf, vbuf, sink_rows, s0, dec_seq, l):
    nb = zs8.shape[0]
    rows8 = zs8.shape[1]
    R = dec_seq * ATT_GROUP
    wb = kbuf.shape[2]
    return pl.pallas_call(
        functools.partial(_sample_kernel, dec_seq=dec_seq),
        grid=(nb // SAMPLE_TILE,),
        in_specs=[
            pl.BlockSpec((SAMPLE_TILE, ATT_KV_HEADS, R, ATT_HEAD_DIM), lambda b: (b, 0, 0, 0)),
            pl.BlockSpec((SAMPLE_TILE, rows8, Z_WIDTH), lambda b: (b, 0, 0)),
            pl.BlockSpec((SAMPLE_TILE, rows8, GLA_KWIDTH), lambda b: (b, 0, 0)),
            pl.BlockSpec((None, SAMPLE_TILE, wb, KV_WIDTH), lambda b: (l, b, 0, 0)),
            pl.BlockSpec((None, SAMPLE_TILE, wb, KV_WIDTH), lambda b: (l, b, 0, 0)),
            pl.BlockSpec((ATT_KV_HEADS, R, 1), lambda b: (0, 0, 0)),
            pl.BlockSpec((None, SAMPLE_TILE, GLA_HEADS, GLA_DK, GLA_DV), lambda b: (l, b, 0, 0, 0)),
        ],
        out_specs=[
            pl.BlockSpec((SAMPLE_TILE, ATT_KV_HEADS, R, ATT_HEAD_DIM), lambda b: (b, 0, 0, 0)),
            pl.BlockSpec((SAMPLE_TILE, rows8, GLA_WIDTH), lambda b: (b, 0, 0)),
            pl.BlockSpec((SAMPLE_TILE, GLA_HEADS, GLA_DK, GLA_DV), lambda b: (b, 0, 0, 0)),
        ],
        out_shape=[
            jax.ShapeDtypeStruct((nb, ATT_KV_HEADS, R, ATT_HEAD_DIM), F32),
            jax.ShapeDtypeStruct((nb, rows8, GLA_WIDTH), F32),
            jax.ShapeDtypeStruct((nb, GLA_HEADS, GLA_DK, GLA_DV), F32),
        ],
        compiler_params=pltpu.CompilerParams(
            dimension_semantics=("parallel",), vmem_limit_bytes=VMEM_LIMIT),
        name="sample_mixer",
    )(qa_s, zs8, la8, kbuf, vbuf, sink_rows, s0)


def _merge_kernel(*refs):
    x_ref, att_ref, go_ref = refs[:3]
    rg_refs = refs[3:3 + GLA_HEADS]
    ggla_ref, wo_ref, gpost_ref, o_ref = refs[3 + GLA_HEADS:]
    parts = []
    for h in range(GLA_HEADS):
        vs = slice(h * GLA_DV, (h + 1) * GLA_DV)
        rg = rg_refs[h][...]
        parts.append((_rms(go_ref[:, vs], ggla_ref[...]) * (rg * jax.nn.sigmoid(rg))).astype(BF16))
    mixed = jnp.concatenate([att_ref[...]] + parts, axis=-1)
    o_ref[...] = x_ref[...] + _rms(_dot(mixed, wo_ref[...]), gpost_ref[...])


def _merge(x, att, go, z, g_gla, w_out, gains, l):
    m = x.shape[0]
    return pl.pallas_call(
        _merge_kernel,
        grid=(m // TM_MERGE,),
        in_specs=(
            [pl.BlockSpec((TM_MERGE, D_MODEL), lambda i: (i, 0)),
             pl.BlockSpec((TM_MERGE, ATT_WIDTH), lambda i: (i, 0)),
             pl.BlockSpec((TM_MERGE, GLA_WIDTH), lambda i: (i, 0))]
            + [pl.BlockSpec((TM_MERGE, GLA_DV), lambda i, h=h: (i, COL_RG // GLA_DV + h)) for h in range(GLA_HEADS)]
            + [pl.BlockSpec((None, 1, GLA_DV), lambda i: (l, 0, 0)),
               pl.BlockSpec((None, ATT_WIDTH + GLA_WIDTH, D_MODEL), lambda i: (l, 0, 0)),
               _gain_spec(l, 3)]
        ),
        out_specs=pl.BlockSpec((TM_MERGE, D_MODEL), lambda i: (i, 0)),
        out_shape=jax.ShapeDtypeStruct((m, D_MODEL), F32),
        compiler_params=pltpu.CompilerParams(
            dimension_semantics=("parallel",), vmem_limit_bytes=VMEM_LIMIT),
        name="merge",
    )(x, att, go, *([z] * GLA_HEADS), g_gla, w_out, gains)


def _copy_kernel(*refs):
    y_ref = refs[-1]
    for t, x_ref in enumerate(refs[:-1]):
        y_ref[t * BLOCK:(t + 1) * BLOCK, :] = x_ref[...]


def _extract_prompt(x, batch, seq, lp):
    skip = (lp - seq) // BLOCK
    return pl.pallas_call(
        _copy_kernel,
        grid=(batch, seq // (COPY_BLOCKS * BLOCK)),
        in_specs=[pl.BlockSpec((BLOCK, x.shape[1]),
                               lambda b, k, t=t: (b * (lp // BLOCK) + skip + COPY_BLOCKS * k + t, 0))
                  for t in range(COPY_BLOCKS)],
        out_specs=pl.BlockSpec((None, COPY_BLOCKS * BLOCK, x.shape[1]), lambda b, k: (b, k, 0)),
        out_shape=jax.ShapeDtypeStruct((batch, seq, x.shape[1]), x.dtype),
        compiler_params=pltpu.CompilerParams(dimension_semantics=("parallel", "parallel")),
        name="extract_prompt",
    )(*([x] * COPY_BLOCKS))


def kernel(x_prompt, x_sample, cache_k_win, cache_v_win, state_gla, meta_tokens, norm_gains,
           w_ffn_gate, w_ffn_up, w_ffn_down, w_in, w_gate_up, b_gate, attn_sinks, gla_norm, w_out):
    batch, seq, _ = x_prompt.shape
    dec_batch, dec_seq, _ = x_sample.shape
    depth = norm_gains.shape[0]
    lp = seq + BLOCK
    nb = lp // BLOCK
    mp = batch * lp
    ms = dec_batch * dec_seq
    m_pad = -(-(mp + ms) // TM) * TM
    wb = cache_k_win.shape[2]
    rows8 = 8
    assert dec_seq <= rows8 and lp % BLOCK == 0 and dec_batch % SAMPLE_TILE == 0

    head = jnp.concatenate([jnp.zeros((PAD, D_MODEL), F32), meta_tokens.astype(F32)], axis=0)
    pieces = []
    for b in range(batch):
        pieces += [head, x_prompt[b]]
    pieces += [x_sample.reshape(ms, D_MODEL), jnp.zeros((m_pad - mp - ms, D_MODEL), F32)]
    x = jnp.concatenate(pieces, axis=0)

    wg, wu, wd = w_ffn_gate, w_ffn_up, w_ffn_down
    wo = w_out.astype(BF16)
    w_in_t = jnp.swapaxes(w_in, 1, 2)
    col = jnp.arange(Z_WIDTH)
    colscale = jnp.where((col >= COL_QA) & (col < COL_QA + ATT_WIDTH), ATT_HEAD_DIM ** -0.5, 1.0).astype(F32)[None]
    wa_p = jnp.pad(w_gate_up, ((0, 0), (0, LR_PAD - GLA_RANK), (0, 0))).astype(BF16)
    ba = b_gate.reshape(depth, 1, GLA_KWIDTH)
    g_gla = gla_norm.reshape(depth, 1, GLA_DV)
    gains = norm_gains.reshape(depth, 6, 1, D_MODEL)
    kbuf = cache_k_win.reshape(depth, dec_batch, wb, KV_WIDTH)
    vbuf = cache_v_win.reshape(depth, dec_batch, wb, KV_WIDTH)
    tail = jnp.zeros((m_pad - mp - ms, GLA_WIDTH), F32)

    pk, pv, ps, sk, sv, ss = [], [], [], [], [], []
    for l in range(depth):
        x = _ffn(x, gains, wg, wu, wd, l, 0)

        z, la = _proj(x, gains, w_in_t, colscale, wa_p, ba, l)
        att, go, st_p = _mixer_prompt(z, la, attn_sinks[l], batch, nb, m_pad)

        zs = z[mp:mp + ms].reshape(dec_batch, dec_seq, Z_WIDTH)
        zs8 = jnp.pad(zs, ((0, 0), (0, rows8 - dec_seq), (0, 0)))
        la8 = jnp.pad(la[mp:mp + ms].reshape(dec_batch, dec_seq, GLA_KWIDTH), ((0, 0), (0, rows8 - dec_seq), (0, 0)))
        qa_s = zs[:, :, COL_QA:COL_QA + ATT_WIDTH].reshape(dec_batch, dec_seq, ATT_KV_HEADS, ATT_GROUP, ATT_HEAD_DIM)
        qa_s = qa_s.transpose(0, 2, 1, 3, 4).reshape(dec_batch, ATT_KV_HEADS, dec_seq * ATT_GROUP, ATT_HEAD_DIM)
        sink_rows = jnp.tile(attn_sinks[l].reshape(ATT_KV_HEADS, 1, ATT_GROUP), (1, dec_seq, 1))
        sink_rows = sink_rows.reshape(ATT_KV_HEADS, dec_seq * ATT_GROUP, 1)
        att_s, go_s, st_s = _sample_mixer(qa_s, zs8, la8, kbuf, vbuf, sink_rows, state_gla, dec_seq, l)
        att_s = att_s.reshape(dec_batch, ATT_KV_HEADS, dec_seq, ATT_GROUP, ATT_HEAD_DIM)
        att_s = att_s.transpose(0, 2, 1, 3, 4).reshape(ms, ATT_WIDTH)

        att = lax.dynamic_update_slice(att, jnp.concatenate([att_s, tail], axis=0).astype(BF16), (mp, 0))
        go = lax.dynamic_update_slice(go, jnp.concatenate([go_s[:, :dec_seq].reshape(ms, GLA_WIDTH), tail], axis=0), (mp, 0))
        x = _merge(x, att, go, z, g_gla, wo, gains, l)

        win = [z[b * lp + lp - WINDOW:(b + 1) * lp, COL_KA:COL_VA + KV_WIDTH] for b in range(batch)]
        win = jnp.stack(win).reshape(batch, WINDOW, 2, ATT_KV_HEADS, ATT_HEAD_DIM)
        pk.append(win[:, :, 0])
        pv.append(win[:, :, 1])
        ps.append(st_p.transpose(0, 1, 3, 2))
        kn = zs[:, :, COL_KA:COL_KA + KV_WIDTH].reshape(dec_batch, dec_seq, ATT_KV_HEADS, ATT_HEAD_DIM)
        vn = zs[:, :, COL_VA:COL_VA + KV_WIDTH].reshape(dec_batch, dec_seq, ATT_KV_HEADS, ATT_HEAD_DIM)
        sk.append(jnp.concatenate([cache_k_win[l], kn], axis=1)[:, -wb:])
        sv.append(jnp.concatenate([cache_v_win[l], vn], axis=1)[:, -wb:])
        ss.append(st_s)

        x = _ffn(x, gains, wg, wu, wd, l, 1)

    y_prompt = _extract_prompt(x, batch, seq, lp)
    y_sample = x[mp:mp + ms].reshape(dec_batch, dec_seq, D_MODEL)
    return (y_prompt, y_sample, jnp.stack(pk), jnp.stack(pv), jnp.stack(ps),
            jnp.stack(sk), jnp.stack(sv), jnp.stack(ss))
```

```python
import functools

import jax
import jax.numpy as jnp
from jax import lax
from jax.experimental import pallas as pl
from jax.experimental.pallas import tpu as pltpu

F32 = jnp.float32
BF16 = jnp.bfloat16

D_MODEL = 2048
D_FF = 5632
N_META = 16
BLOCK = 128
WINDOW = 128
ATT_HEADS = 16
ATT_KV_HEADS = 2
ATT_GROUP = 8
ATT_HEAD_DIM = 64
ATT_WIDTH = 1024
KV_WIDTH = 128
GLA_HEADS = 4
GLA_DK = 128
GLA_DV = 256
GLA_KWIDTH = 512
GLA_WIDTH = 1024
GLA_RANK = 16
GLA_GATE_NORM = 16.0
GLA_CHUNK = 64
GLA_SUB = 8
EPS = 1e-6
LOG2_E = 1.4426950408889634
PAD = BLOCK - N_META

COL_QA, COL_KA, COL_VA, COL_QG, COL_KG, COL_VG, COL_RG, COL_LR = 0, 1024, 1152, 1280, 1792, 2304, 3328, 4352
COL_TILE = 256
Z_WIDTH = 4608
LR_PAD = 128

VMEM_LIMIT = 56 * 1024 * 1024
VMEM_LIMIT_FFN = 60 * 1024 * 1024

TM = 1072
TM_MERGE = 536
TF = 256
TN = 768
X_PARTS = 4
SAMPLE_TILE = 4
COPY_BLOCKS = 4


def _rms(x, gain):
    ms = jnp.mean(x * x, axis=-1, keepdims=True)
    return x * lax.rsqrt(ms + EPS) * gain


def _dot(a, b):
    return jnp.dot(a, b, preferred_element_type=F32)


def _dot_nt(a, b):
    return lax.dot_general(a, b, (((1,), (1,)), ((), ())), preferred_element_type=F32)


def _dot_tn(a, b):
    return lax.dot_general(a, b, (((0,), (0,)), ((), ())), preferred_element_type=F32)


def _row_parts(n_rows, parts):
    units = n_rows // 8
    out, start = [], 0
    for r in range(parts):
        size = (units // parts + (1 if r < units % parts else 0)) * 8
        out.append((start, size))
        start += size
    assert start == n_rows
    return out


def _ffn_kernel(x_hbm, gpre_ref, gpost_ref, wg_hbm, wu_hbm, wd_hbm, o_ref, xn_ref, x_buf, wg_buf, wu_buf, wd_buf,
                sem, xsem, *, l, f):
    i = pl.program_id(0)
    n_tiles = pl.num_programs(0)
    n_chunks = D_FF // TF
    parts = _row_parts(TM, X_PARTS)

    def w_copies(j, slot):
        cols = pl.ds(j * TF, TF)
        return (pltpu.make_async_copy(wg_hbm.at[l, f, :, cols], wg_buf.at[slot], sem.at[0, slot]),
                pltpu.make_async_copy(wu_hbm.at[l, f, :, cols], wu_buf.at[slot], sem.at[1, slot]),
                pltpu.make_async_copy(wd_hbm.at[l, f, cols, :], wd_buf.at[slot], sem.at[2, slot]))

    def x_copy(tile, r):
        start, size = parts[r]
        return pltpu.make_async_copy(x_hbm.at[pl.ds(tile * TM + start, size), :],
                                     x_buf.at[pl.ds(start, size), :], xsem.at[r])

    @pl.when(i == 0)
    def _():
        for r in range(X_PARTS):
            x_copy(0, r).start()
        for c in w_copies(0, 0):
            c.start()

    for r, (start, size) in enumerate(parts):
        x_copy(i, r).wait()
        rows = pl.ds(start, size)
        xn_ref[rows, :] = _rms(x_buf[rows, :], gpre_ref[...]).astype(BF16)
    o_ref[...] = jnp.zeros_like(o_ref)

    def pair(p, carry):
        for slot in range(2):
            j = 2 * p + slot
            for c in w_copies(j, slot):
                c.wait()
            for n, c in enumerate(w_copies(lax.rem(j + 1, n_chunks), 1 - slot)):
                c.start(priority=n % 2)
            xn = xn_ref[...]
            g = _dot(xn, wg_buf[slot].astype(BF16))
            u = _dot(xn, wu_buf[slot].astype(BF16))
            h = (g * jax.nn.sigmoid(g) * u).astype(BF16)
            o_ref[...] += _dot(h, wd_buf[slot].astype(BF16))
        return carry

    lax.fori_loop(0, n_chunks // 2, pair, 0)

    for r, (start, size) in enumerate(parts):
        rows = pl.ds(start, size)
        o_ref[rows, :] = x_buf[rows, :] + 0.5 * _rms(o_ref[rows, :], gpost_ref[...])

        @pl.when(i + 1 < n_tiles)
        def _():
            x_copy(i + 1, r).start()

    @pl.when(i == n_tiles - 1)
    def _():
        for c in w_copies(0, 0):
            c.wait()


def _gain_spec(l, idx):
    return pl.BlockSpec((None, None, 1, D_MODEL), lambda *_: (l, idx, 0, 0))


def _ffn(x, gains, wg, wu, wd, l, f):
    m = x.shape[0]
    return pl.pallas_call(
        functools.partial(_ffn_kernel, l=l, f=f),
        grid=(m // TM,),
        in_specs=[
            pl.BlockSpec(memory_space=pl.ANY),
            _gain_spec(l, 4 * f),
            _gain_spec(l, 4 * f + 1),
            pl.BlockSpec(memory_space=pl.ANY),
            pl.BlockSpec(memory_space=pl.ANY),
            pl.BlockSpec(memory_space=pl.ANY),
        ],
        out_specs=pl.BlockSpec((TM, D_MODEL), lambda i: (i, 0)),
        out_shape=jax.ShapeDtypeStruct((m, D_MODEL), F32),
        scratch_shapes=[
            pltpu.VMEM((TM, D_MODEL), BF16),
            pltpu.VMEM((TM, D_MODEL), F32),
            pltpu.VMEM((2, D_MODEL, TF), F32),
            pltpu.VMEM((2, D_MODEL, TF), F32),
            pltpu.VMEM((2, TF, D_MODEL), F32),
            pltpu.SemaphoreType.DMA((3, 2)),
            pltpu.SemaphoreType.DMA((X_PARTS,)),
        ],
        compiler_params=pltpu.CompilerParams(
            dimension_semantics=("arbitrary",), vmem_limit_bytes=VMEM_LIMIT_FFN),
        name="ffn",
    )(x, gains, gains, wg, wu, wd)


def _proj_kernel(x_ref, g_ref, wt_ref, cs_ref, wa_ref, ba_ref, z_ref, la_ref, xn_ref, *, n_cols):
    j = pl.program_id(1)
    last = pl.num_programs(1) - 1
    n_valid = n_cols - (Z_WIDTH - TN)

    @pl.when(j == 0)
    def _():
        xn_ref[...] = _rms(x_ref[...], g_ref[...]).astype(BF16)

    @pl.when(j < last)
    def _():
        z_ref[...] = _dot_nt(xn_ref[...], wt_ref[...].astype(BF16)) * cs_ref[...]

    @pl.when(j == last)
    def _():
        z_ref[:, :n_valid] = _dot_nt(xn_ref[...], wt_ref[:n_valid, :].astype(BF16)) * cs_ref[:, :n_valid]
        z_ref[:, n_valid:] = jnp.zeros((z_ref.shape[0], TN - n_valid), F32)
        lr_off = COL_LR - (Z_WIDTH - TN)
        lr = z_ref[:, lr_off:lr_off + LR_PAD].astype(BF16)
        logit = _dot(lr, wa_ref[...]) + ba_ref[...]
        log_sig = jnp.minimum(logit, 0.0) - jnp.log1p(jnp.exp(-jnp.abs(logit)))
        la_ref[...] = log_sig * (1.0 / GLA_GATE_NORM)


def _proj(x, gains, w_in_t, colscale, wa_p, ba, l):
    m = x.shape[0]
    n_cols = w_in_t.shape[1]
    assert Z_WIDTH - TN < COL_LR and COL_LR + GLA_RANK <= n_cols <= Z_WIDTH
    return pl.pallas_call(
        functools.partial(_proj_kernel, n_cols=n_cols),
        grid=(m // TM, Z_WIDTH // TN),
        in_specs=[
            pl.BlockSpec((TM, D_MODEL), lambda i, j: (i, 0)),
            _gain_spec(l, 2),
            pl.BlockSpec((None, TN, D_MODEL), lambda i, j: (l, j, 0)),
            pl.BlockSpec((1, TN), lambda i, j: (0, j)),
            pl.BlockSpec((None, LR_PAD, GLA_KWIDTH), lambda i, j: (l, 0, 0)),
            pl.BlockSpec((None, 1, GLA_KWIDTH), lambda i, j: (l, 0, 0)),
        ],
        out_specs=[
            pl.BlockSpec((TM, TN), lambda i, j: (i, j)),
            pl.BlockSpec((TM, GLA_KWIDTH), lambda i, j: (i, 0)),
        ],
        out_shape=[
            jax.ShapeDtypeStruct((m, Z_WIDTH), F32),
            jax.ShapeDtypeStruct((m, GLA_KWIDTH), F32),
        ],
        scratch_shapes=[pltpu.VMEM((TM, D_MODEL), BF16)],
        compiler_params=pltpu.CompilerParams(
            dimension_semantics=("parallel", "arbitrary"), vmem_limit_bytes=VMEM_LIMIT),
        name="proj",
    )(x, gains, w_in_t, colscale, wa_p, ba)


def _pair_blockdiag(x128, kv):
    lane = lax.broadcasted_iota(jnp.int32, x128.shape, 1)
    own = jnp.where((lane >= kv * ATT_HEAD_DIM) & (lane < (kv + 1) * ATT_HEAD_DIM), x128, 0.0)
    other = pltpu.roll(own, ATT_HEAD_DIM, axis=1)
    lo, hi = (own, other) if kv == 0 else (other, own)
    return jnp.concatenate([lo, hi], axis=0).astype(BF16)


def _attn_block(i, sink_ref, q_ref, kc_ref, kp_ref, vc_ref, vp_ref, o_ref):
    kb = 2 * BLOCK
    row = lax.broadcasted_iota(jnp.int32, (BLOCK, kb), 0)
    col = lax.broadcasted_iota(jnp.int32, (BLOCK, kb), 1)
    diff = row + BLOCK - col
    key_pos = col + (i - 1) * BLOCK
    mask = (diff >= 0) & (diff <= WINDOW) & (key_pos >= PAD)
    lane = lax.broadcasted_iota(jnp.int32, (BLOCK, 2 * ATT_HEAD_DIM), 1)
    kk = jnp.concatenate([kp_ref[...], kc_ref[...]], axis=0)
    vv = jnp.concatenate([vp_ref[...], vc_ref[...]], axis=0)
    slabs = ATT_GROUP // 2
    for kv in range(ATT_KV_HEADS):
        k2 = _pair_blockdiag(kk, kv)
        v2 = _pair_blockdiag(vv, kv)
        c0 = kv * slabs * 2 * ATT_HEAD_DIM
        q4 = q_ref[:, c0:c0 + slabs * 2 * ATT_HEAD_DIM]
        q4 = jnp.concatenate([q4[:, p * 128:(p + 1) * 128] for p in range(slabs)], axis=0).astype(BF16)
        s = _dot_nt(q4, k2)
        probs, inv = [], []
        for p in range(slabs):
            halves, rden = [], []
            for e in range(2):
                sink = sink_ref[kv * ATT_GROUP + 2 * p + e]
                sp = jnp.where(mask, s[p * BLOCK:(p + 1) * BLOCK, e * kb:(e + 1) * kb], -jnp.inf)
                mx = jnp.maximum(jnp.max(sp, axis=-1, keepdims=True), sink)
                pe = jnp.exp(sp - mx)
                rden.append(1.0 / (jnp.sum(pe, axis=-1, keepdims=True) + jnp.exp(sink - mx)))
                halves.append(pe.astype(BF16))
            probs.append(jnp.concatenate(halves, axis=1))
            inv.append(jnp.where(lane < ATT_HEAD_DIM, rden[0], rden[1]))
        o = _dot(jnp.concatenate(probs, axis=0), v2)
        for p in range(slabs):
            o_ref[:, c0 + p * 128:c0 + (p + 1) * 128] = (o[p * BLOCK:(p + 1) * BLOCK] * inv[p]).astype(o_ref.dtype)


def _cumsum_rows(la, tri):
    hi = la.astype(BF16)
    r1 = la - hi.astype(F32)
    mid = r1.astype(BF16)
    lo = (r1 - mid.astype(F32)).astype(BF16)
    return _dot(tri, hi) + _dot(tri, mid) + _dot(tri, lo)


def _gla_chunk(c, rows, q_refs, k_refs, v_refs, la_ref, o_ref, state_ref, maybe_first):
    per = COL_TILE // GLA_DK
    C = GLA_CHUNK

    rowc = lax.broadcasted_iota(jnp.int32, (C, C), 0)
    colc = lax.broadcasted_iota(jnp.int32, (C, C), 1)
    tri = (rowc >= colc).astype(BF16)
    b_all = _cumsum_rows(la_ref[rows, :], tri) * LOG2_E

    pos = c * C + lax.broadcasted_iota(jnp.int32, (C, 1), 0)
    valid = (pos >= PAD).astype(F32)

    lane = lax.broadcasted_iota(jnp.int32, (GLA_SUB, C), 1)
    rsub = lax.broadcasted_iota(jnp.int32, (GLA_SUB, C), 0)
    scale = GLA_DK ** -0.5

    heads = range(GLA_HEADS)
    q, k, bh = [], [], []
    for h in heads:
        sub = slice((h % per) * GLA_DK, (h % per + 1) * GLA_DK)
        q.append(q_refs[h // per][rows, sub] * scale)
        k.append(k_refs[h // per][rows, sub] * valid)
        bh.append(b_all[:, h * GLA_DK:(h + 1) * GLA_DK])

    blocks = [[] for _ in heads]
    for i in range(C // GLA_SUB):
        lo_r = i * GLA_SUB
        sl = slice(lo_r, lo_r + GLA_SUB)
        w = [jnp.zeros((GLA_SUB, C), F32) for _ in heads]
        for s in range(GLA_SUB):
            r = lo_r + s
            at_r = lane == r
            for h in heads:
                zed = q[h][sl] * k[h][r:r + 1] * jnp.exp2(bh[h][sl] - bh[h][r:r + 1])
                w[h] = jnp.where(at_r, jnp.sum(zed, axis=-1, keepdims=True), w[h])
        causal = lane <= rsub + lo_r
        for h in heads:
            wh = jnp.where(causal, w[h], 0.0)
            if i > 0:
                ref_b = bh[h][lo_r - 1:lo_r]
                qi = (q[h][sl] * jnp.exp2(bh[h][sl] - ref_b)).astype(BF16)
                kj = k[h][:lo_r] * jnp.exp2(ref_b - bh[h][:lo_r])
                kj = jnp.concatenate([kj, jnp.zeros((C - lo_r, GLA_DK), F32)], axis=0).astype(BF16)
                wh = jnp.where(lane < lo_r, _dot_nt(qi, kj), wh)
            blocks[h].append(wh)

    for h in heads:
        v = v_refs[h][rows, :].astype(BF16)
        bl = bh[h][C - 1:C, :]
        st = state_ref[h]
        if maybe_first:
            st = jnp.where(c == 0, 0.0, st)
        o = _dot_nt((q[h] * jnp.exp2(bh[h])).astype(BF16), st.astype(BF16))
        a = jnp.concatenate(blocks[h], axis=0).astype(BF16)
        o_ref[rows, h * GLA_DV:(h + 1) * GLA_DV] = o + _dot(a, v)
        k_out = (k[h] * jnp.exp2(bl - bh[h])).astype(BF16)
        state_ref[h] = st * jnp.exp2(bl) + _dot_tn(v, k_out)


def _mixer_kernel(sink_ref, *refs, nb, n_blocks):
    nq = GLA_KWIDTH // COL_TILE
    attn_in, rest = refs[:5], refs[5:]
    q_refs, k_refs, v_refs = rest[:nq], rest[nq:2 * nq], rest[2 * nq:2 * nq + GLA_HEADS]
    la_ref, att_ref, go_ref, st_ref, state_ref = rest[2 * nq + GLA_HEADS:]
    step = pl.program_id(0)

    @pl.when(step < n_blocks)
    def _():
        i = lax.rem(step, nb)
        _attn_block(i, sink_ref, *attn_in, att_ref)
        per_block = BLOCK // GLA_CHUNK
        for half in range(per_block):
            _gla_chunk(i * per_block + half, pl.ds(half * GLA_CHUNK, GLA_CHUNK), q_refs, k_refs, v_refs, la_ref,
                       go_ref, state_ref, maybe_first=(half == 0))
        st_ref[...] = state_ref[...]

    @pl.when(step >= n_blocks)
    def _():
        att_ref[...] = jnp.zeros_like(att_ref)
        go_ref[...] = jnp.zeros_like(go_ref)


def _mixer_prompt(z, la, sinks, batch, nb, m_rows):
    n_blocks = batch * nb
    nq = GLA_KWIDTH // COL_TILE

    def cur(c):
        return lambda s, sink: (jnp.minimum(s, n_blocks - 1), c)

    def prev(c):
        return lambda s, sink: (jnp.maximum(jnp.minimum(s, n_blocks - 1) - 1, 0), c)

    def rows(c):
        return lambda s, sink: (s, c)

    return pl.pallas_call(
        functools.partial(_mixer_kernel, nb=nb, n_blocks=n_blocks),
        grid_spec=pltpu.PrefetchScalarGridSpec(
            num_scalar_prefetch=1,
            grid=(m_rows // BLOCK,),
            in_specs=(
                [pl.BlockSpec((BLOCK, ATT_WIDTH), cur(COL_QA // ATT_WIDTH)),
                 pl.BlockSpec((BLOCK, KV_WIDTH), cur(COL_KA // KV_WIDTH)),
                 pl.BlockSpec((BLOCK, KV_WIDTH), prev(COL_KA // KV_WIDTH)),
                 pl.BlockSpec((BLOCK, KV_WIDTH), cur(COL_VA // KV_WIDTH)),
                 pl.BlockSpec((BLOCK, KV_WIDTH), prev(COL_VA // KV_WIDTH))]
                + [pl.BlockSpec((BLOCK, COL_TILE), rows(COL_QG // COL_TILE + t)) for t in range(nq)]
                + [pl.BlockSpec((BLOCK, COL_TILE), rows(COL_KG // COL_TILE + t)) for t in range(nq)]
                + [pl.BlockSpec((BLOCK, GLA_DV), rows(COL_VG // GLA_DV + t)) for t in range(GLA_HEADS)]
                + [pl.BlockSpec((BLOCK, GLA_KWIDTH), rows(0))]
            ),
            out_specs=[
                pl.BlockSpec((BLOCK, ATT_WIDTH), rows(0)),
                pl.BlockSpec((BLOCK, GLA_WIDTH), rows(0)),
                pl.BlockSpec((None, GLA_HEADS, GLA_DV, GLA_DK),
                             lambda s, sink: (jnp.minimum(s // nb, batch - 1), 0, 0, 0)),
            ],
            scratch_shapes=[pltpu.VMEM((GLA_HEADS, GLA_DV, GLA_DK), F32)],
        ),
        out_shape=[
            jax.ShapeDtypeStruct((m_rows, ATT_WIDTH), BF16),
            jax.ShapeDtypeStruct((m_rows, GLA_WIDTH), F32),
            jax.ShapeDtypeStruct((batch, GLA_HEADS, GLA_DV, GLA_DK), F32),
        ],
        compiler_params=pltpu.CompilerParams(
            dimension_semantics=("arbitrary",), vmem_limit_bytes=VMEM_LIMIT),
        name="mixer_prompt",
    )(sinks, *([z] * (5 + 2 * nq + GLA_HEADS)), la)


def _sample_kernel(qa_ref, zs_ref, la_ref, kbuf_ref, vbuf_ref, sink_ref, s0_ref,
                   att_ref, go_ref, st_ref, *, dec_seq):
    tile = range(qa_ref.shape[0])
    for kv in range(ATT_KV_HEADS):
        for b in tile:
            att_ref[b, kv] = _sample_attn(qa_ref.at[b], zs_ref.at[b], kbuf_ref.at[b], vbuf_ref.at[b], sink_ref,
                                          kv, dec_seq)
    decay = [_sample_decay(la_ref.at[b], dec_seq) for b in tile]
    for h in range(GLA_HEADS):
        for b in tile:
            o, st = _sample_gla_head(zs_ref.at[b], s0_ref.at[b], decay[b], h, dec_seq)
            go_ref[b, :, h * GLA_DV:(h + 1) * GLA_DV] = o
            st_ref[b, h] = st


def _sample_attn(qa_ref, zs_ref, kbuf_ref, vbuf_ref, sink_ref, kv, S):
    R = S * ATT_GROUP
    t_row = lax.broadcasted_iota(jnp.int32, (R, 1), 0) // ATT_GROUP
    c_idx = lax.broadcasted_iota(jnp.int32, (R, kbuf_ref.shape[0]), 1)
    mask_c = c_idx >= t_row
    ks = slice(kv * ATT_HEAD_DIM, (kv + 1) * ATT_HEAD_DIM)
    q = qa_ref[kv]
    sink = sink_ref[kv]
    sc = _dot_nt(q.astype(BF16), kbuf_ref[:, ks].astype(BF16))
    sc = jnp.where(mask_c, sc, -jnp.inf)
    mx = jnp.maximum(jnp.max(sc, axis=-1, keepdims=True), sink)
    kn = zs_ref[:, COL_KA + kv * ATT_HEAD_DIM:COL_KA + (kv + 1) * ATT_HEAD_DIM]
    vn = zs_ref[:, COL_VA + kv * ATT_HEAD_DIM:COL_VA + (kv + 1) * ATT_HEAD_DIM]
    sn = []
    for s in range(S):
        v = jnp.sum(q * kn[s:s + 1], axis=-1, keepdims=True)
        v = jnp.where(t_row >= s, v, -jnp.inf)
        sn.append(v)
        mx = jnp.maximum(mx, v)
    pc = jnp.exp(sc - mx)
    den = jnp.sum(pc, axis=-1, keepdims=True) + jnp.exp(sink - mx)
    o = _dot(pc.astype(BF16), vbuf_ref[:, ks].astype(BF16))
    for s in range(S):
        pn = jnp.exp(sn[s] - mx)
        den = den + pn
        o = o + pn * vn[s:s + 1]
    return o / den


def _sample_decay(la_ref, S):
    rows8 = la_ref.shape[0]
    rid = lax.broadcasted_iota(jnp.int32, (rows8, 1), 0)
    la = la_ref[...]
    run = la[0:1]
    b_all = jnp.where(rid == 0, run, 0.0)
    for t in range(1, S):
        run = run + la[t:t + 1]
        b_all = jnp.where(rid == t, run, b_all)
    return b_all, rid, (rid < S).astype(F32)


def _sample_gla_head(zs_ref, s0_ref, decay, h, S):
    b_all, rid, real = decay
    scale = GLA_DK ** -0.5
    q = zs_ref[:, COL_QG + h * GLA_DK:COL_QG + (h + 1) * GLA_DK] * scale
    k = zs_ref[:, COL_KG + h * GLA_DK:COL_KG + (h + 1) * GLA_DK]
    v = zs_ref[:, COL_VG + h * GLA_DV:COL_VG + (h + 1) * GLA_DV]
    bh = b_all[:, h * GLA_DK:(h + 1) * GLA_DK] * real
    bl = bh[S - 1:S]
    s0 = s0_ref[h]
    k_out = k * jnp.exp(jnp.minimum(bl - bh, 0.0)) * real
    q_in = q * jnp.exp(bh) * real
    o = _dot(q_in.astype(BF16), s0.astype(BF16))
    for s in range(S):
        zed = q * k[s:s + 1] * jnp.exp(jnp.minimum(bh - bh[s:s + 1], 0.0))
        w = jnp.sum(zed, axis=-1, keepdims=True)
        w = jnp.where((rid >= s) & (rid < S), w, 0.0)
        o = o + w * v[s:s + 1]
    e_rows = jnp.broadcast_to(jnp.exp(bl), (GLA_DK, GLA_DK)).T
    decay_rows = jnp.concatenate([e_rows] * (GLA_DV // GLA_DK), axis=1)
    return o, decay_rows * s0 + _dot_tn(k_out.astype(BF16), v.astype(BF16))


def _sample_mixer(qa_s, zs8, la8, kbuf, vbuf, sink_rows, s0, dec_seq, l):
    nb = zs8.shape[0]
    rows8 = zs8.shape[1]
    R = dec_seq * ATT_GROUP
    wb = kbuf.shape[2]
    return pl.pallas_call(
        functools.partial(_sample_kernel, dec_seq=dec_seq),
        grid=(nb // SAMPLE_TILE,),
        in_specs=[
            pl.BlockSpec((SAMPLE_TILE, ATT_KV_HEADS, R, ATT_HEAD_DIM), lambda b: (b, 0, 0, 0)),
            pl.BlockSpec((SAMPLE_TILE, rows8, Z_WIDTH), lambda b: (b, 0, 0)),
            pl.BlockSpec((SAMPLE_TILE, rows8, GLA_KWIDTH), lambda b: (b, 0, 0)),
            pl.BlockSpec((None, SAMPLE_TILE, wb, KV_WIDTH), lambda b: (l, b, 0, 0)),
            pl.BlockSpec((None, SAMPLE_TILE, wb, KV_WIDTH), lambda b: (l, b, 0, 0)),
            pl.BlockSpec((ATT_KV_HEADS, R, 1), lambda b: (0, 0, 0)),
            pl.BlockSpec((None, SAMPLE_TILE, GLA_HEADS, GLA_DK, GLA_DV), lambda b: (l, b, 0, 0, 0)),
        ],
        out_specs=[
            pl.BlockSpec((SAMPLE_TILE, ATT_KV_HEADS, R, ATT_HEAD_DIM), lambda b: (b, 0, 0, 0)),
            pl.BlockSpec((SAMPLE_TILE, rows8, GLA_WIDTH), lambda b: (b, 0, 0)),
            pl.BlockSpec((SAMPLE_TILE, GLA_HEADS, GLA_DK, GLA_DV), lambda b: (b, 0, 0, 0)),
        ],
        out_shape=[
            jax.ShapeDtypeStruct((nb, ATT_KV_HEADS, R, ATT_HEAD_DIM), F32),
            jax.ShapeDtypeStruct((nb, rows8, GLA_WIDTH), F32),
            jax.ShapeDtypeStruct((nb, GLA_HEADS, GLA_DK, GLA_DV), F32),
        ],
        compiler_params=pltpu.CompilerParams(
            dimension_semantics=("parallel",), vmem_limit_bytes=VMEM_LIMIT),
        name="sample_mixer",
    )(qa_s, zs8, la8, kbuf, vbuf, sink_rows, s0)


def _merge_kernel(*refs):
    x_ref, att_ref, go_ref = refs[:3]
    rg_refs = refs[3:3 + GLA_HEADS]
    ggla_ref, wo_ref, gpost_ref, o_ref = refs[3 + GLA_HEADS:]
    parts = []
    for h in range(GLA_HEADS):
        vs = slice(h * GLA_DV, (h + 1) * GLA_DV)
        rg = rg_refs[h][...]
        parts.append((_rms(go_ref[:, vs], ggla_ref[...]) * (rg * jax.nn.sigmoid(rg))).astype(BF16))
    mixed = jnp.concatenate([att_ref[...]] + parts, axis=-1)
    o_ref[...] = x_ref[...] + _rms(_dot(mixed, wo_ref[...]), gpost_ref[...])


def _merge(x, att, go, z, g_gla, w_out, gains, l):
    m = x.shape[0]
    return pl.pallas_call(
        _merge_kernel,
        grid=(m // TM_MERGE,),
        in_specs=(
            [pl.BlockSpec((TM_MERGE, D_MODEL), lambda i: (i, 0)),
             pl.BlockSpec((TM_MERGE, ATT_WIDTH), lambda i: (i, 0)),
             pl.BlockSpec((TM_MERGE, GLA_WIDTH), lambda i: (i, 0))]
            + [pl.BlockSpec((TM_MERGE, GLA_DV), lambda i, h=h: (i, COL_RG // GLA_DV + h)) for h in range(GLA_HEADS)]
            + [pl.BlockSpec((None, 1, GLA_DV), lambda i: (l, 0, 0)),
               pl.BlockSpec((None, ATT_WIDTH + GLA_WIDTH, D_MODEL), lambda i: (l, 0, 0)),
               _gain_spec(l, 3)]
        ),
        out_specs=pl.BlockSpec((TM_MERGE, D_MODEL), lambda i: (i, 0)),
        out_shape=jax.ShapeDtypeStruct((m, D_MODEL), F32),
        compiler_params=pltpu.CompilerParams(
            dimension_semantics=("parallel",), vmem_limit_bytes=VMEM_LIMIT),
        name="merge",
    )(x, att, go, *([z] * GLA_HEADS), g_gla, w_out, gains)


def _copy_kernel(*refs):
    y_ref = refs[-1]
    for t, x_ref in enumerate(refs[:-1]):
        y_ref[t * BLOCK:(t + 1) * BLOCK, :] = x_ref[...]


def _extract_prompt(x, batch, seq, lp):
    skip = (lp - seq) // BLOCK
    return pl.pallas_call(
        _copy_kernel,
        grid=(batch, seq // (COPY_BLOCKS * BLOCK)),
        in_specs=[pl.BlockSpec((BLOCK, x.shape[1]),
                               lambda b, k, t=t: (b * (lp // BLOCK) + skip + COPY_BLOCKS * k + t, 0))
                  for t in range(COPY_BLOCKS)],
        out_specs=pl.BlockSpec((None, COPY_BLOCKS * BLOCK, x.shape[1]), lambda b, k: (b, k, 0)),
        out_shape=jax.ShapeDtypeStruct((batch, seq, x.shape[1]), x.dtype),
        compiler_params=pltpu.CompilerParams(dimension_semantics=("parallel", "parallel")),
        name="extract_prompt",
    )(*([x] * COPY_BLOCKS))


def kernel(x_prompt, x_sample, cache_k_win, cache_v_win, state_gla, meta_tokens, norm_gains,
           w_ffn_gate, w_ffn_up, w_ffn_down, w_in, w_gate_up, b_gate, attn_sinks, gla_norm, w_out):
    batch, seq, _ = x_prompt.shape
    dec_batch, dec_seq, _ = x_sample.shape
    depth = norm_gains.shape[0]
    lp = seq + BLOCK
    nb = lp // BLOCK
    mp = batch * lp
    ms = dec_batch * dec_seq
    m_pad = -(-(mp + ms) // TM) * TM
    wb = cache_k_win.shape[2]
    rows8 = 8
    assert dec_seq <= rows8 and lp % BLOCK == 0 and dec_batch % SAMPLE_TILE == 0

    head = jnp.concatenate([jnp.zeros((PAD, D_MODEL), F32), meta_tokens.astype(F32)], axis=0)
    pieces = []
    for b in range(batch):
        pieces += [head, x_prompt[b]]
    pieces += [x_sample.reshape(ms, D_MODEL), jnp.zeros((m_pad - mp - ms, D_MODEL), F32)]
    x = jnp.concatenate(pieces, axis=0)

    wg, wu, wd = w_ffn_gate, w_ffn_up, w_ffn_down
    wo = w_out.astype(BF16)
    w_in_t = jnp.swapaxes(w_in, 1, 2)
    col = jnp.arange(Z_WIDTH)
    colscale = jnp.where((col >= COL_QA) & (col < COL_QA + ATT_WIDTH), ATT_HEAD_DIM ** -0.5, 1.0).astype(F32)[None]
    wa_p = jnp.pad(w_gate_up, ((0, 0), (0, LR_PAD - GLA_RANK), (0, 0))).astype(BF16)
    ba = b_gate.reshape(depth, 1, GLA_KWIDTH)
    g_gla = gla_norm.reshape(depth, 1, GLA_DV)
    gains = norm_gains.reshape(depth, 6, 1, D_MODEL)
    kbuf = cache_k_win.reshape(depth, dec_batch, wb, KV_WIDTH)
    vbuf = cache_v_win.reshape(depth, dec_batch, wb, KV_WIDTH)
    tail = jnp.zeros((m_pad - mp - ms, GLA_WIDTH), F32)

    pk, pv, ps, sk, sv, ss = [], [], [], [], [], []
    for l in range(depth):
        x = _ffn(x, gains, wg, wu, wd, l, 0)

        z, la = _proj(x, gains, w_in_t, colscale, wa_p, ba, l)
        att, go, st_p = _mixer_prompt(z, la, attn_sinks[l], batch, nb, m_pad)

        zs = z[mp:mp + ms].reshape(dec_batch, dec_seq, Z_WIDTH)
        zs8 = jnp.pad(zs, ((0, 0), (0, rows8 - dec_seq), (0, 0)))
        la8 = jnp.pad(la[mp:mp + ms].reshape(dec_batch, dec_seq, GLA_KWIDTH), ((0, 0), (0, rows8 - dec_seq), (0, 0)))
        qa_s = zs[:, :, COL_QA:COL_QA + ATT_WIDTH].reshape(dec_batch, dec_seq, ATT_KV_HEADS, ATT_GROUP, ATT_HEAD_DIM)
        qa_s = qa_s.transpose(0, 2, 1, 3, 4).reshape(dec_batch, ATT_KV_HEADS, dec_seq * ATT_GROUP, ATT_HEAD_DIM)
        sink_rows = jnp.tile(attn_sinks[l].reshape(ATT_KV_HEADS, 1, ATT_GROUP), (1, dec_seq, 1))
        sink_rows = sink_rows.reshape(ATT_KV_HEADS, dec_seq * ATT_GROUP, 1)
        att_s, go_s, st_s = _sample_mixer(qa_s, zs8, la8, kbuf, vbuf, sink_rows, state_gla, dec_seq, l)
        att_s = att_s.reshape(dec_batch, ATT_KV_HEADS, dec_seq, ATT_GROUP, ATT_HEAD_DIM)
        att_s = att_s.transpose(0, 2, 1, 3, 4).reshape(ms, ATT_WIDTH)

        att = lax.dynamic_update_slice(att, jnp.concatenate([att_s, tail], axis=0).astype(BF16), (mp, 0))
        go = lax.dynamic_update_slice(go, jnp.concatenate([go_s[:, :dec_seq].reshape(ms, GLA_WIDTH), tail], axis=0), (mp, 0))
        x = _merge(x, att, go, z, g_gla, wo, gains, l)

        win = [z[b * lp + lp - WINDOW:(b + 1) * lp, COL_KA:COL_VA + KV_WIDTH] for b in range(batch)]
        win = jnp.stack(win).reshape(batch, WINDOW, 2, ATT_KV_HEADS, ATT_HEAD_DIM)
        pk.append(win[:, :, 0])
        pv.append(win[:, :, 1])
        ps.append(st_p.transpose(0, 1, 3, 2))
        kn = zs[:, :, COL_KA:COL_KA + KV_WIDTH].reshape(dec_batch, dec_seq, ATT_KV_HEADS, ATT_HEAD_DIM)
        vn = zs[:, :, COL_VA:COL_VA + KV_WIDTH].reshape(dec_batch, dec_seq, ATT_KV_HEADS, ATT_HEAD_DIM)
        sk.append(jnp.concatenate([cache_k_win[l], kn], axis=1)[:, -wb:])
        sv.append(jnp.concatenate([cache_v_win[l], vn], axis=1)[:, -wb:])
        ss.append(st_s)

        x = _ffn(x, gains, wg, wu, wd, l, 1)

    y_prompt = _extract_prompt(x, batch, seq, lp)
    y_sample = x[mp:mp + ms].reshape(dec_batch, dec_seq, D_MODEL)
    return (y_prompt, y_sample, jnp.stack(pk), jnp.stack(pv), jnp.stack(ps),
            jnp.stack(sk), jnp.stack(sv), jnp.stack(ss))
```
